```python
import math
import jax, jax.numpy as jnp
from jax import lax
import numpy as np

D_MODEL = 1024
BATCH = 32
SEQ = 2048
DEPTH = 4
DEC_BATCH = 2
DEC_SEQ = 16384
PAST_LEN = 128

GRID_W = 64
Q_BLOCK = 128
D_MIX = D_MODEL
A_WIDTH = D_MIX // 2
A_V_DIM = 128
A_HEAD_DIM = A_V_DIM // 2
A_HEADS = A_WIDTH // A_V_DIM
B_WIDTH = D_MIX - A_WIDTH
B_HEAD_DIM = 64
B_HEADS = B_WIDTH // B_HEAD_DIM
B_KV_HEADS = 2
B_GROUP = B_HEADS // B_KV_HEADS
NUM_BUCKETS = 32
MAX_DISTANCE = 128
ROPE_THETA = 10000.0
EPS = 1e-6
SPLIT_WIDTHS = (A_HEADS * 2 * A_HEAD_DIM, A_HEADS * 2 * A_HEAD_DIM, A_HEADS * A_V_DIM, A_WIDTH,
                B_HEADS * B_HEAD_DIM, B_KV_HEADS * B_HEAD_DIM, B_KV_HEADS * B_HEAD_DIM, B_WIDTH)
D_IN = sum(SPLIT_WIDTHS)
SPLIT_POINTS = tuple(sum(SPLIT_WIDTHS[:i + 1]) for i in range(len(SPLIT_WIDTHS) - 1))

kernel_name = "hymba_diffattn_gqa_axial_encoder"


def rms_norm(x, g):
    xf = x.astype(jnp.float32)
    y = xf * lax.rsqrt(jnp.mean(xf * xf, axis=-1, keepdims=True) + EPS)
    return (y * g.astype(jnp.float32)).astype(x.dtype)


def t5_bucket(rel):
    half = NUM_BUCKETS // 2
    max_exact = half // 2
    ret = jnp.where(rel > 0, half, 0)
    n = jnp.abs(rel)
    nf = jnp.maximum(n, 1).astype(jnp.float32)
    large = max_exact + (jnp.log(nf / max_exact) / math.log(MAX_DISTANCE / max_exact)
                         * (half - max_exact)).astype(jnp.int32)
    large = jnp.minimum(large, half - 1)
    return ret + jnp.where(n < max_exact, n, large)


def relative_bias_by_offset(rel_table, seq_len):
    offsets = jnp.arange(-(seq_len - 1), seq_len, dtype=jnp.int32)
    return rel_table[t5_bucket(offsets)].T.astype(jnp.float32)


def axial_rope_tables(seq_len):
    rows = seq_len // GRID_W
    row = jnp.repeat(jnp.arange(rows), GRID_W).astype(jnp.float32)
    col = jnp.tile(jnp.arange(GRID_W), rows).astype(jnp.float32)
    axis_dim = B_HEAD_DIM // 2
    inv_freq = ROPE_THETA ** (-jnp.arange(0, axis_dim, 2, dtype=jnp.float32) / axis_dim)
    ang_r = row[:, None] * inv_freq[None, :]
    ang_c = col[:, None] * inv_freq[None, :]
    ang = jnp.concatenate([ang_r, ang_r, ang_c, ang_c], axis=-1)
    return jnp.cos(ang), jnp.sin(ang)


def apply_axial_rope(x, cos, sin):
    xf = x.astype(jnp.float32)
    x1, x2, x3, x4 = jnp.split(xf, 4, axis=-1)
    rot = jnp.concatenate([-x2, x1, -x4, x3], axis=-1)
    return (xf * cos[:, None, :] + rot * sin[:, None, :]).astype(x.dtype)


def to_blocks(t):
    b, s = t.shape[:2]
    return jnp.moveaxis(t.reshape((b, s // Q_BLOCK, Q_BLOCK) + t.shape[2:]), 1, 0)


def from_blocks(t):
    nb, b, q = t.shape[:3]
    return jnp.moveaxis(t, 0, 1).reshape((b, nb * q) + t.shape[3:])


def diff_attention(q, k, v, lam, bias_off):
    s = q.shape[1]
    scale = A_HEAD_DIM ** -0.5
    k_pos = jnp.arange(s, dtype=jnp.int32)

    def block(args):
        qb, start = args
        q_pos = start + jnp.arange(Q_BLOCK, dtype=jnp.int32)
        idx = k_pos[None, :] - q_pos[:, None] + (s - 1)
        bias = bias_off[:, idx]
        logits = jnp.einsum('bqhcd,bkhcd->bhcqk', qb, k).astype(jnp.float32) * scale \
            + bias[None, :, None]
        p = jax.nn.softmax(logits, axis=-1)
        w = p[:, :, 0] - lam * p[:, :, 1]
        return jnp.einsum('bhqk,bkhe->bqhe', w.astype(v.dtype), v)

    starts = jnp.arange(s // Q_BLOCK, dtype=jnp.int32) * Q_BLOCK
    out = lax.map(block, (to_blocks(q), starts))
    return from_blocks(out)


def gqa_attention(q, k, v):
    b, s = q.shape[:2]
    scale = B_HEAD_DIM ** -0.5
    qg = q.reshape(b, s, B_KV_HEADS, B_GROUP, B_HEAD_DIM)

    def block(qb):
        logits = jnp.einsum('bqngd,bknd->bngqk', qb, k).astype(jnp.float32) * scale
        p = jax.nn.softmax(logits, axis=-1)
        return jnp.einsum('bngqk,bknd->bqngd', p.astype(v.dtype), v)

    out = lax.map(block, to_blocks(qg))
    return from_blocks(out).reshape(b, s, B_HEADS * B_HEAD_DIM)


def encoder_layer(x, c_act, layer_idx, norm_g, w_ada, b_ada, w_in, lam_q1, lam_k1, lam_q2, lam_k2,
                  subln_g, q_norm_g, k_norm_g, w_out, bias_off, cos, sin):
    b, s, _ = x.shape
    mod = c_act @ w_ada + b_ada
    shift, scale, gate = jnp.split(mod[:, None, :], 3, axis=-1)
    h = rms_norm(x, norm_g) * (1 + scale) + shift
    proj = h @ w_in
    qa, ka, va, ga, qb, kb, vb, gb = jnp.split(proj, SPLIT_POINTS, axis=-1)

    lam_init = 0.8 - 0.6 * math.exp(-0.3 * layer_idx)
    lam = (jnp.exp(jnp.sum(lam_q1.astype(jnp.float32) * lam_k1.astype(jnp.float32)))
           - jnp.exp(jnp.sum(lam_q2.astype(jnp.float32) * lam_k2.astype(jnp.float32))) + lam_init)
    oa = diff_attention(qa.reshape(b, s, A_HEADS, 2, A_HEAD_DIM),
                        ka.reshape(b, s, A_HEADS, 2, A_HEAD_DIM),
                        va.reshape(b, s, A_HEADS, A_V_DIM), lam, bias_off)
    oa = (rms_norm(oa, subln_g) * (1 - lam_init)).reshape(b, s, A_WIDTH)

    qb = apply_axial_rope(rms_norm(qb.reshape(b, s, B_HEADS, B_HEAD_DIM), q_norm_g), cos, sin)
    kb = apply_axial_rope(rms_norm(kb.reshape(b, s, B_KV_HEADS, B_HEAD_DIM), k_norm_g), cos, sin)
    ob = gqa_attention(qb, kb, vb.reshape(b, s, B_KV_HEADS, B_HEAD_DIM))

    mixed = jnp.concatenate([oa * jax.nn.silu(ga), ob * jax.nn.silu(gb)], axis=-1)
    return x + gate * (mixed @ w_out)


def run_trunk(x, c, rel_table, norm_g, w_ada, b_ada, w_in, lam_q1, lam_k1, lam_q2, lam_k2,
              subln_g, q_norm_g, k_norm_g, w_out, final_g):
    s = x.shape[1]
    bias_off = relative_bias_by_offset(rel_table, s)
    cos, sin = axial_rope_tables(s)
    c_act = jax.nn.silu(c)
    for l in range(DEPTH):
        x = encoder_layer(x, c_act, l, norm_g[l], w_ada[l], b_ada[l], w_in[l], lam_q1[l], lam_k1[l],
                          lam_q2[l], lam_k2[l], subln_g[l], q_norm_g[l], k_norm_g[l], w_out[l],
                          bias_off, cos, sin)
    return rms_norm(x, final_g)


def setup_inputs(seed: int = 0) -> dict:
    key = jax.random.key(seed)
    ks = jax.random.split(key, 20)
    f32 = jnp.float32
    nrm = lambda k, shape, s: jax.random.normal(k, shape, f32) * s
    return {
        "x_prompt": nrm(ks[0], (BATCH, SEQ, D_MODEL), 1.0),
        "x_sample": nrm(ks[1], (DEC_BATCH, DEC_SEQ, D_MODEL), 1.0),
        "c_prompt": nrm(ks[2], (BATCH, D_MODEL), 1.0),
        "c_sample": nrm(ks[3], (DEC_BATCH, D_MODEL), 1.0),
        "rel_table": nrm(ks[4], (NUM_BUCKETS, A_HEADS), 0.5),
        "norm_g": 1.0 + nrm(ks[5], (DEPTH, D_MODEL), 0.02),
        "w_ada": nrm(ks[6], (DEPTH, D_MODEL, 3 * D_MODEL), 0.5 * D_MODEL ** -0.5),
        "b_ada": nrm(ks[7], (DEPTH, 3 * D_MODEL), 0.02),
        "w_in": nrm(ks[8], (DEPTH, D_MODEL, D_IN), D_MODEL ** -0.5),
        "lam_q1": nrm(ks[9], (DEPTH, A_HEAD_DIM), 0.1),
        "lam_k1": nrm(ks[10], (DEPTH, A_HEAD_DIM), 0.1),
        "lam_q2": nrm(ks[11], (DEPTH, A_HEAD_DIM), 0.1),
        "lam_k2": nrm(ks[12], (DEPTH, A_HEAD_DIM), 0.1),
        "subln_g": 1.0 + nrm(ks[13], (DEPTH, A_V_DIM), 0.02),
        "q_norm_g": 1.0 + nrm(ks[14], (DEPTH, B_HEAD_DIM), 0.02),
        "k_norm_g": 1.0 + nrm(ks[15], (DEPTH, B_HEAD_DIM), 0.02),
        "w_out": nrm(ks[16], (DEPTH, D_MIX, D_MODEL), D_MIX ** -0.5),
        "final_g": 1.0 + nrm(ks[17], (D_MODEL,), 0.02),
    }


def reference(x_prompt, x_sample, c_prompt, c_sample, rel_table, norm_g, w_ada, b_ada, w_in,
              lam_q1, lam_k1, lam_q2, lam_k2, subln_g, q_norm_g, k_norm_g, w_out, final_g):
    y_prompt = run_trunk(x_prompt, c_prompt, rel_table, norm_g, w_ada, b_ada, w_in, lam_q1, lam_k1,
                         lam_q2, lam_k2, subln_g, q_norm_g, k_norm_g, w_out, final_g)
    y_sample = run_trunk(x_sample, c_sample, rel_table, norm_g, w_ada, b_ada, w_in, lam_q1, lam_k1,
                         lam_q2, lam_k2, subln_g, q_norm_g, k_norm_g, w_out, final_g)
    return (y_prompt, y_sample)
```

```python
import functools
import math

import jax
import jax.numpy as jnp
from jax import lax
from jax.experimental import pallas as pl
from jax.experimental.pallas import tpu as pltpu

F32 = jnp.float32
BF16 = jnp.bfloat16

D_MODEL = 1024
DEPTH = 4
GRID_W = 64
A_WIDTH = 512
A_V_DIM = 128
A_HEAD_DIM = 64
A_HEADS = 4
B_WIDTH = 512
B_HEAD_DIM = 64
B_HEADS = 8
B_KV_HEADS = 2
B_GROUP = 4
NUM_BUCKETS = 32
MAX_DISTANCE = 128
ROPE_THETA = 10000.0
EPS = 1e-6

LOG2E = 1.4426950408889634
Q_SCALE = (A_HEAD_DIM ** -0.5) * LOG2E
M_INIT = -1e30
ONES_ROWS = 16
VMEM_LIMIT_BYTES = 56 * 1024 * 1024

_C_QA, _C_KA, _C_VA, _C_GA, _C_QB, _C_KB, _C_VB, _C_GB = 0, 512, 1024, 1536, 2048, 2560, 2688, 2816
D_IN = 3328
N_NAT = 1536
N_TR = 1792


def _rms_rows(x, g):
    ms = jnp.mean(x * x, axis=-1, keepdims=True)
    return x * lax.rsqrt(ms + EPS) * g


def _mod_kernel(c_ref, w_ref, b_ref, o_ref):
    c = c_ref[...]
    c_act = (c * jax.nn.sigmoid(c)).astype(BF16)
    o_ref[0] = jnp.dot(c_act, w_ref[0], preferred_element_type=F32) + b_ref[0]


def _adaln_mod(c_all, w_ada_bf, b_ada):
    rows = c_all.shape[0]
    return pl.pallas_call(
        _mod_kernel,
        grid=(DEPTH, 3),
        in_specs=[
            pl.BlockSpec((rows, D_MODEL), lambda l, j: (0, 0)),
            pl.BlockSpec((1, D_MODEL, D_MODEL), lambda l, j: (l, 0, j)),
            pl.BlockSpec((1, 1, D_MODEL), lambda l, j: (l, 0, j)),
        ],
        out_specs=pl.BlockSpec((1, rows, D_MODEL), lambda l, j: (l, 0, j)),
        out_shape=jax.ShapeDtypeStruct((DEPTH, rows, 3 * D_MODEL), F32),
        name="adaln_mod",
    )(c_all, w_ada_bf, b_ada.reshape(DEPTH, 1, 3 * D_MODEL))


def _lam_kernel(q1_ref, k1_ref, q2_ref, k2_ref, init_ref, o_ref):
    s1 = jnp.sum(q1_ref[...] * k1_ref[...], axis=-1, keepdims=True)
    s2 = jnp.sum(q2_ref[...] * k2_ref[...], axis=-1, keepdims=True)
    lam = jnp.exp(s1) - jnp.exp(s2) + init_ref[...][:, 0:1]
    o_ref[...] = jnp.broadcast_to(lam, o_ref.shape)


def _lambdas(lam_q1, lam_k1, lam_q2, lam_k2):
    init = jnp.asarray([[0.8 - 0.6 * math.exp(-0.3 * l)] * 128 for l in range(DEPTH)], F32)
    out = pl.pallas_call(
        _lam_kernel,
        out_shape=jax.ShapeDtypeStruct((DEPTH, 128), F32),
        name="diff_lambda",
    )(lam_q1, lam_k1, lam_q2, lam_k2, init)
    return out[:, 0]


def _rope_t(x, cos, sin):
    rot = jnp.concatenate([-x[16:32], x[0:16], -x[48:64], x[32:48]], axis=0)
    return x * cos + rot * sin


def _norm_t(x, g):
    ms = jnp.mean(x * x, axis=0, keepdims=True)
    return x * lax.rsqrt(ms + EPS) * g


def _inproj_kernel(x_ref, mod_ref, ng_ref, wn_ref, wt_ref, cos_ref, sin_ref, qg_ref, kg_ref,
                   ka_ref, gate_ref, qat_ref, vat_ref, qbt_ref, kb_ref, vbt_ref):
    tm = x_ref.shape[1]
    x = x_ref[0]
    shift = mod_ref[0, 0:1, :]
    scale = mod_ref[0, 1:2, :]
    h = (_rms_rows(x, ng_ref[...]) * (1.0 + scale) + shift).astype(BF16)

    pn = jnp.dot(h, wn_ref[...], preferred_element_type=F32)
    ka_ref[0] = pn[:, 0:512].astype(BF16)
    g = pn[:, 512:N_NAT]
    gate_ref[0] = g * jax.nn.sigmoid(g)

    pt = lax.dot_general(wt_ref[...], h, (((1,), (1,)), ((), ())), preferred_element_type=F32)
    qat_ref[0] = (pt[0:512] * Q_SCALE).astype(BF16)
    ones = jnp.ones((ONES_ROWS, tm), BF16)
    for hh in range(A_HEADS):
        vat_ref[0, hh, 0:A_V_DIM, :] = pt[512 + hh * A_V_DIM:512 + (hh + 1) * A_V_DIM].astype(BF16)
        vat_ref[0, hh, A_V_DIM:A_V_DIM + ONES_ROWS, :] = ones

    cos = cos_ref[...]
    sin = sin_ref[...]
    qg = qg_ref[...]
    for hh in range(B_HEADS):
        xh = pt[1024 + hh * 64:1024 + (hh + 1) * 64]
        qbt_ref[0, hh * 64:(hh + 1) * 64, :] = (_rope_t(_norm_t(xh, qg), cos, sin) * Q_SCALE).astype(BF16)
    kg = kg_ref[...]
    kparts = []
    for n in range(B_KV_HEADS):
        xh = pt[1536 + n * 64:1536 + (n + 1) * 64]
        kparts.append(_rope_t(_norm_t(xh, kg), cos, sin))
    kb_ref[0] = jnp.concatenate(kparts, axis=0).T.astype(BF16)
    for n in range(B_KV_HEADS):
        vbt_ref[0, n, 0:64, :] = pt[1664 + n * 64:1664 + (n + 1) * 64].astype(BF16)
        vbt_ref[0, n, 64:64 + ONES_ROWS, :] = ones


def _inproj(x, mod, ng, wn, wt, cos_t, sin_t, qg_b, kg_b, tm):
    b, s, _ = x.shape
    grid = (b, s // tm)
    const2 = lambda i, j: (0, 0)
    return pl.pallas_call(
        _inproj_kernel,
        grid=grid,
        in_specs=[
            pl.BlockSpec((1, tm, D_MODEL), lambda i, j: (i, j, 0)),
            pl.BlockSpec((1, 3, D_MODEL), lambda i, j: (i, 0, 0)),
            pl.BlockSpec((1, D_MODEL), const2),
            pl.BlockSpec((D_MODEL, N_NAT), const2),
            pl.BlockSpec((N_TR, D_MODEL), const2),
            pl.BlockSpec((B_HEAD_DIM, tm), lambda i, j: (0, j)),
            pl.BlockSpec((B_HEAD_DIM, tm), lambda i, j: (0, j)),
            pl.BlockSpec((B_HEAD_DIM, tm), const2),
            pl.BlockSpec((B_HEAD_DIM, tm), const2),
        ],
        out_specs=[
            pl.BlockSpec((1, tm, A_WIDTH), lambda i, j: (i, j, 0)),
            pl.BlockSpec((1, tm, D_MODEL), lambda i, j: (i, j, 0)),
            pl.BlockSpec((1, A_WIDTH, tm), lambda i, j: (i, 0, j)),
            pl.BlockSpec((1, A_HEADS, A_V_DIM + ONES_ROWS, tm), lambda i, j: (i, 0, 0, j)),
            pl.BlockSpec((1, B_WIDTH, tm), lambda i, j: (i, 0, j)),
            pl.BlockSpec((1, tm, B_KV_HEADS * B_HEAD_DIM), lambda i, j: (i, j, 0)),
            pl.BlockSpec((1, B_KV_HEADS, B_HEAD_DIM + ONES_ROWS, tm), lambda i, j: (i, 0, 0, j)),
        ],
        out_shape=[
            jax.ShapeDtypeStruct((b, s, A_WIDTH), BF16),
            jax.ShapeDtypeStruct((b, s, D_MODEL), F32),
            jax.ShapeDtypeStruct((b, A_WIDTH, s), BF16),
            jax.ShapeDtypeStruct((b, A_HEADS, A_V_DIM + ONES_ROWS, s), BF16),
            jax.ShapeDtypeStruct((b, B_WIDTH, s), BF16),
            jax.ShapeDtypeStruct((b, s, B_KV_HEADS * B_HEAD_DIM), BF16),
            jax.ShapeDtypeStruct((b, B_KV_HEADS, B_HEAD_DIM + ONES_ROWS, s), BF16),
        ],
        compiler_params=pltpu.CompilerParams(
            dimension_semantics=("arbitrary", "arbitrary"), vmem_limit_bytes=VMEM_LIMIT_BYTES),
        name="inproj",
    )(x, mod, ng, wn, wt, cos_t, sin_t, qg_b, kg_b)


def _softmax_step(s, m, shift, v_chunk, acc_ref):
    mc = jnp.max(s, axis=0, keepdims=True)
    if shift is not None:
        mc = mc + shift
    mn = jnp.maximum(m, mc)
    alpha = jnp.exp2(m - mn)
    sub = mn if shift is None else mn - shift
    p = jnp.exp2(s - sub).astype(BF16)
    acc_ref[...] = acc_ref[...] * alpha + jnp.dot(v_chunk, p, preferred_element_type=F32)
    return mn


def _attn_a_kernel(scal_ref, qt_ref, k_ref, vt_ref, bias_ref, sg_ref, o_ref, wq_ref, acc_ref,
                   *, tq, tk, n_chunks, ratio, out_scale):
    h = pl.program_id(1)
    qi = pl.program_id(2)
    q = qt_ref[0]
    zero = jnp.zeros((A_HEAD_DIM, tq), BF16)
    wq_ref[0:64, 0:tq] = q[0:64]
    wq_ref[0:64, tq:2 * tq] = zero
    wq_ref[64:128, 0:tq] = zero
    wq_ref[64:128, tq:2 * tq] = q[64:128]
    acc_ref[...] = jnp.zeros(acc_ref.shape, F32)

    lam = scal_ref[0]
    c_neg = scal_ref[1 + h]
    c_pos = scal_ref[1 + A_HEADS + h]
    log_ratio = ratio.bit_length() - 1
    c_own = lax.shift_right_logical(qi, log_ratio)
    r = lax.bitwise_and(qi, ratio - 1)
    lo = jnp.maximum(c_own - 1, 0)
    hi = jnp.minimum(c_own + 1, n_chunks - 1)

    def logits(c):
        start = pl.multiple_of(c * tk, tk)
        s = jnp.dot(k_ref[0, pl.ds(start, tk), :], wq_ref[...], preferred_element_type=F32)
        return s, vt_ref[0, 0, :, pl.ds(start, tk)]

    def far(shift):
        def body(c, m):
            s, vc = logits(c)
            return _softmax_step(s, m, shift, vc, acc_ref)
        return body

    def near(c, m):
        s, vc = logits(c)
        t = bias_ref[0, (c - c_own + 1) * ratio + r]
        s = s + jnp.concatenate([t, t], axis=1)
        return _softmax_step(s, m, None, vc, acc_ref)

    m = jnp.full((1, 2 * tq), M_INIT, F32)
    m = lax.fori_loop(0, lo, far(c_neg), m)
    m = lax.fori_loop(lo, hi + 1, near, m)
    m = lax.fori_loop(hi + 1, n_chunks, far(c_pos), m)

    acc = acc_ref[...]
    o = acc[0:A_V_DIM, :] / acc[A_V_DIM:A_V_DIM + 1, :]
    w = o[:, 0:tq] - lam * o[:, tq:2 * tq]
    y = _norm_t(w, sg_ref[...]) * out_scale
    o_ref[0] = y.T


def _attn_a(scal, qat, ka, vat, bias_tiles, sg_b, tq, tk, out_scale):
    b, s, _ = ka.shape
    n_chunks = s // tk
    ratio = tk // tq
    kern = functools.partial(_attn_a_kernel, tq=tq, tk=tk, n_chunks=n_chunks, ratio=ratio, out_scale=out_scale)
    return pl.pallas_call(
        kern,
        grid=(b, A_HEADS, s // tq),
        in_specs=[
            pl.BlockSpec(memory_space=pltpu.SMEM),
            pl.BlockSpec((1, 2 * A_HEAD_DIM, tq), lambda i, hh, j: (i, hh, j)),
            pl.BlockSpec((1, s, 2 * A_HEAD_DIM), lambda i, hh, j: (i, 0, hh)),
            pl.BlockSpec((1, 1, A_V_DIM + ONES_ROWS, s), lambda i, hh, j: (i, hh, 0, 0)),
            pl.BlockSpec((1, 3 * ratio, tk, tq), lambda i, hh, j: (hh, 0, 0, 0)),
            pl.BlockSpec((A_V_DIM, tq), lambda i, hh, j: (0, 0)),
        ],
        out_specs=pl.BlockSpec((1, tq, A_V_DIM), lambda i, hh, j: (i, j, hh)),
        out_shape=jax.ShapeDtypeStruct((b, s, A_WIDTH), F32),
        scratch_shapes=[
            pltpu.VMEM((2 * A_HEAD_DIM, 2 * tq), BF16),
            pltpu.VMEM((A_V_DIM + ONES_ROWS, 2 * tq), F32),
        ],
        compiler_params=pltpu.CompilerParams(
            dimension_semantics=("arbitrary", "arbitrary", "arbitrary"), vmem_limit_bytes=VMEM_LIMIT_BYTES),
        name="attn_a",
    )(scal, qat, ka, vat, bias_tiles, sg_b)


def _attn_b_kernel(qt_ref, k_ref, vt_ref, o_ref, wq_ref, acc0_ref, acc1_ref, *, tq, tk, n_chunks):
    half = B_GROUP * tq
    wq_ref[...] = jnp.zeros(wq_ref.shape, BF16)
    for hh in range(B_HEADS):
        n = hh // B_GROUP
        wq_ref[n * 64:(n + 1) * 64, hh * tq:(hh + 1) * tq] = qt_ref[0, hh * 64:(hh + 1) * 64, :]
    acc0_ref[...] = jnp.zeros(acc0_ref.shape, F32)
    acc1_ref[...] = jnp.zeros(acc1_ref.shape, F32)

    def body(c, m):
        start = pl.multiple_of(c * tk, tk)
        s = jnp.dot(k_ref[0, pl.ds(start, tk), :], wq_ref[...], preferred_element_type=F32)
        mc = jnp.max(s, axis=0, keepdims=True)
        mn = jnp.maximum(m, mc)
        alpha = jnp.exp2(m - mn)
        p = jnp.exp2(s - mn).astype(BF16)
        v0 = vt_ref[0, 0, :, pl.ds(start, tk)]
        v1 = vt_ref[0, 1, :, pl.ds(start, tk)]
        acc0_ref[...] = acc0_ref[...] * alpha[:, 0:half] + jnp.dot(v0, p[:, 0:half], preferred_element_type=F32)
        acc1_ref[...] = acc1_ref[...] * alpha[:, half:] + jnp.dot(v1, p[:, half:], preferred_element_type=F32)
        return mn

    lax.fori_loop(0, n_chunks, body, jnp.full((1, B_HEADS * tq), M_INIT, F32))

    parts = []
    for acc_ref in (acc0_ref, acc1_ref):
        acc = acc_ref[...]
        o = acc[0:64, :] / acc[64:65, :]
        for g in range(B_GROUP):
            parts.append(o[:, g * tq:(g + 1) * tq])
    o_ref[0] = jnp.concatenate(parts, axis=0).T


def _attn_b(qbt, kb, vbt, tq, tk):
    b, s, _ = kb.shape
    kern = functools.partial(_attn_b_kernel, tq=tq, tk=tk, n_chunks=s // tk)
    return pl.pallas_call(
        kern,
        grid=(b, s // tq),
        in_specs=[
            pl.BlockSpec((1, B_WIDTH, tq), lambda i, j: (i, 0, j)),
            pl.BlockSpec((1, s, B_KV_HEADS * B_HEAD_DIM), lambda i, j: (i, 0, 0)),
            pl.BlockSpec((1, B_KV_HEADS, B_HEAD_DIM + ONES_ROWS, s), lambda i, j: (i, 0, 0, 0)),
        ],
        out_specs=pl.BlockSpec((1, tq, B_WIDTH), lambda i, j: (i, j, 0)),
        out_shape=jax.ShapeDtypeStruct((b, s, B_WIDTH), F32),
        scratch_shapes=[
            pltpu.VMEM((B_KV_HEADS * B_HEAD_DIM, B_HEADS * tq), BF16),
            pltpu.VMEM((B_HEAD_DIM + ONES_ROWS, B_GROUP * tq), F32),
            pltpu.VMEM((B_HEAD_DIM + ONES_ROWS, B_GROUP * tq), F32),
        ],
        compiler_params=pltpu.CompilerParams(
            dimension_semantics=("arbitrary", "arbitrary"), vmem_limit_bytes=VMEM_LIMIT_BYTES),
        name="attn_b",
    )(qbt, kb, vbt)


def _outproj_kernel(oa_ref, ob_ref, gate_ref, x_ref, mod_ref, w_ref, fg_ref, o_ref, *, final):
    ma = (oa_ref[0] * gate_ref[0, :, 0:A_WIDTH]).astype(BF16)
    mb = (ob_ref[0] * gate_ref[0, :, A_WIDTH:D_MODEL]).astype(BF16)
    y = (jnp.dot(ma, w_ref[0:A_WIDTH, :], preferred_element_type=F32)
         + jnp.dot(mb, w_ref[A_WIDTH:D_MODEL, :], preferred_element_type=F32))
    xn = x_ref[0] + mod_ref[0, 2:3, :] * y
    if final:
        xn = _rms_rows(xn, fg_ref[...])
    o_ref[0] = xn


def _outproj(oa, ob, gate, x, mod, w, fg, tm, final):
    b, s, _ = x.shape
    tok = lambda i, j: (i, j, 0)
    return pl.pallas_call(
        functools.partial(_outproj_kernel, final=final),
        grid=(b, s // tm),
        in_specs=[
            pl.BlockSpec((1, tm, A_WIDTH), tok),
            pl.BlockSpec((1, tm, B_WIDTH), tok),
            pl.BlockSpec((1, tm, D_MODEL), tok),
            pl.BlockSpec((1, tm, D_MODEL), tok),
            pl.BlockSpec((1, 3, D_MODEL), lambda i, j: (i, 0, 0)),
            pl.BlockSpec((D_MODEL, D_MODEL), lambda i, j: (0, 0)),
            pl.BlockSpec((1, D_MODEL), lambda i, j: (0, 0)),
        ],
        out_specs=pl.BlockSpec((1, tm, D_MODEL), tok),
        out_shape=jax.ShapeDtypeStruct((b, s, D_MODEL), F32),
        compiler_params=pltpu.CompilerParams(
            dimension_semantics=("arbitrary", "arbitrary"), vmem_limit_bytes=VMEM_LIMIT_BYTES),
        name="outproj",
    )(oa, ob, gate, x, mod, w, fg)


def _t5_bucket(rel):
    half = NUM_BUCKETS // 2
    max_exact = half // 2
    ret = jnp.where(rel > 0, half, 0)
    n = jnp.abs(rel)
    nf = jnp.maximum(n, 1).astype(jnp.float32)
    large = max_exact + (jnp.log(nf / max_exact) / math.log(MAX_DISTANCE / max_exact)
                         * (half - max_exact)).astype(jnp.int32)
    large = jnp.minimum(large, half - 1)
    return ret + jnp.where(n < max_exact, n, large)


def _bias_tables(rel_table, s, tq, tk):
    ratio = tk // tq
    offsets = jnp.arange(-(s - 1), s, dtype=jnp.int32)
    boff = rel_table[_t5_bucket(offsets)].T.astype(F32) * LOG2E
    j = jnp.arange(tk, dtype=jnp.int32)[:, None]
    i = jnp.arange(tq, dtype=jnp.int32)[None, :]
    tiles = []
    for dc in (-1, 0, 1):
        for r in range(ratio):
            rel = dc * tk - r * tq + j - i
            tiles.append(boff[:, jnp.clip(rel + (s - 1), 0, 2 * s - 2)])
    return jnp.stack(tiles, axis=1), boff[:, 0], boff[:, 2 * s - 2]


def _rope_tables_t(s):
    rows = s // GRID_W
    row = jnp.repeat(jnp.arange(rows), GRID_W).astype(F32)
    col = jnp.tile(jnp.arange(GRID_W), rows).astype(F32)
    axis_dim = B_HEAD_DIM // 2
    inv_freq = ROPE_THETA ** (-jnp.arange(0, axis_dim, 2, dtype=F32) / axis_dim)
    ang_r = row[:, None] * inv_freq[None, :]
    ang_c = col[:, None] * inv_freq[None, :]
    ang = jnp.concatenate([ang_r, ang_r, ang_c, ang_c], axis=-1)
    return jnp.cos(ang).T, jnp.sin(ang).T


def _tile_sizes(s):
    tm = 512
    tk_a = 512 if s >= 8192 else 256
    return tm, (256, tk_a), (128, 512)


def _trunk(x, mod, lam, rel_table, norm_g, wn, wt, subln_g, q_norm_g, k_norm_g, w_out, final_g):
    b, s, _ = x.shape
    tm, (tq_a, tk_a), (tq_b, tk_b) = _tile_sizes(s)
    cos_t, sin_t = _rope_tables_t(s)
    bias_tiles, c_neg, c_pos = _bias_tables(rel_table, s, tq_a, tk_a)
    fg = final_g.reshape(1, D_MODEL)
    for l in range(DEPTH):
        lam_init = 0.8 - 0.6 * math.exp(-0.3 * l)
        qg_b = jnp.broadcast_to(q_norm_g[l][:, None], (B_HEAD_DIM, tm))
        kg_b = jnp.broadcast_to(k_norm_g[l][:, None], (B_HEAD_DIM, tm))
        sg_b = jnp.broadcast_to(subln_g[l][:, None], (A_V_DIM, tq_a))
        ka, gate, qat, vat, qbt, kb, vbt = _inproj(
            x, mod[l], norm_g[l].reshape(1, D_MODEL), wn[l], wt[l], cos_t, sin_t, qg_b, kg_b, tm)
        scal = jnp.concatenate([lam[l:l + 1], c_neg, c_pos]).astype(F32)
        oa = _attn_a(scal, qat, ka, vat, bias_tiles, sg_b, tq_a, tk_a, 1.0 - lam_init)
        ob = _attn_b(qbt, kb, vbt, tq_b, tk_b)
        x = _outproj(oa, ob, gate, x, mod[l], w_out[l], fg, tm, final=(l == DEPTH - 1))
    return x


def kernel(x_prompt, x_sample, c_prompt, c_sample, rel_table, norm_g, w_ada, b_ada, w_in, lam_q1, lam_k1, lam_q2,
           lam_k2, subln_g, q_norm_g, k_norm_g, w_out, final_g):
    bp = x_prompt.shape[0]
    bs = x_sample.shape[0]
    rows = -(-(bp + bs) // 8) * 8
    c_all = jnp.concatenate([c_prompt, c_sample, jnp.zeros((rows - bp - bs, D_MODEL), F32)], axis=0)
    mod_all = _adaln_mod(c_all, w_ada.astype(BF16), b_ada)
    mod_p = mod_all[:, :bp].reshape(DEPTH, bp, 3, D_MODEL)
    mod_s = mod_all[:, bp:bp + bs].reshape(DEPTH, bs, 3, D_MODEL)
    lam = _lambdas(lam_q1, lam_k1, lam_q2, lam_k2)

    w_bf = w_in.astype(BF16)
    wn = jnp.concatenate([w_bf[:, :, _C_KA:_C_VA], w_bf[:, :, _C_GA:_C_QB], w_bf[:, :, _C_GB:D_IN]], axis=2)
    wt = jnp.concatenate([w_bf[:, :, _C_QA:_C_KA], w_bf[:, :, _C_VA:_C_GA], w_bf[:, :, _C_QB:_C_GB]], axis=2)
    wt = jnp.swapaxes(wt, 1, 2)
    w_out_bf = w_out.astype(BF16)

    args = (rel_table, norm_g, wn, wt, subln_g, q_norm_g, k_norm_g, w_out_bf, final_g)
    y_prompt = _trunk(x_prompt, mod_p, lam, *args)
    y_sample = _trunk(x_sample, mod_s, lam, *args)
    return (y_prompt, y_sample)
```

```python
import functools
import math

import jax
import jax.numpy as jnp
from jax import lax
from jax.experimental import pallas as pl
from jax.experimental.pallas import tpu as pltpu

F32 = jnp.float32
BF16 = jnp.bfloat16

D_MODEL = 1024
DEPTH = 4
GRID_W = 64
A_WIDTH = 512
A_V_DIM = 128
A_HEAD_DIM = 64
A_HEADS = 4
B_WIDTH = 512
B_HEAD_DIM = 64
B_HEADS = 8
B_KV_HEADS = 2
B_GROUP = 4
NUM_BUCKETS = 32
MAX_DISTANCE = 128
ROPE_THETA = 10000.0
EPS = 1e-6

LOG2E = 1.4426950408889634
Q_SCALE = (A_HEAD_DIM ** -0.5) * LOG2E
M_INIT = -1e30
ONES_ROWS = 16
VMEM_LIMIT_BYTES = 56 * 1024 * 1024

_C_QA, _C_KA, _C_VA, _C_GA, _C_QB, _C_KB, _C_VB, _C_GB = 0, 512, 1024, 1536, 2048, 2560, 2688, 2816
D_IN = 3328
N_NAT = 1536
N_TR = 1792


def _rms_rows(x, g):
    ms = jnp.mean(x * x, axis=-1, keepdims=True)
    return x * lax.rsqrt(ms + EPS) * g


def _mod_kernel(c_ref, w_ref, b_ref, o_ref):
    c = c_ref[...]
    c_act = (c * jax.nn.sigmoid(c)).astype(BF16)
    o_ref[0] = jnp.dot(c_act, w_ref[0], preferred_element_type=F32) + b_ref[0]


def _adaln_mod(c_all, w_ada_bf, b_ada):
    rows = c_all.shape[0]
    return pl.pallas_call(
        _mod_kernel,
        grid=(DEPTH, 3),
        in_specs=[
            pl.BlockSpec((rows, D_MODEL), lambda l, j: (0, 0)),
            pl.BlockSpec((1, D_MODEL, D_MODEL), lambda l, j: (l, 0, j)),
            pl.BlockSpec((1, 1, D_MODEL), lambda l, j: (l, 0, j)),
        ],
        out_specs=pl.BlockSpec((1, rows, D_MODEL), lambda l, j: (l, 0, j)),
        out_shape=jax.ShapeDtypeStruct((DEPTH, rows, 3 * D_MODEL), F32),
        name="adaln_mod",
    )(c_all, w_ada_bf, b_ada.reshape(DEPTH, 1, 3 * D_MODEL))


def _lam_kernel(q1_ref, k1_ref, q2_ref, k2_ref, init_ref, o_ref):
    s1 = jnp.sum(q1_ref[...] * k1_ref[...], axis=-1, keepdims=True)
    s2 = jnp.sum(q2_ref[...] * k2_ref[...], axis=-1, keepdims=True)
    lam = jnp.exp(s1) - jnp.exp(s2) + init_ref[...][:, 0:1]
    o_ref[...] = jnp.broadcast_to(lam, o_ref.shape)


def _lambdas(lam_q1, lam_k1, lam_q2, lam_k2):
    init = jnp.asarray([[0.8 - 0.6 * math.exp(-0.3 * l)] * 128 for l in range(DEPTH)], F32)
    out = pl.pallas_call(
        _lam_kernel,
        out_shape=jax.ShapeDtypeStruct((DEPTH, 128), F32),
        name="diff_lambda",
    )(lam_q1, lam_k1, lam_q2, lam_k2, init)
    return out[:, 0]


def _rope_t(x, cos, sin):
    rot = jnp.concatenate([-x[16:32], x[0:16], -x[48:64], x[32:48]], axis=0)
    return x * cos + rot * sin


def _norm_t(x, g):
    ms = jnp.mean(x * x, axis=0, keepdims=True)
    return x * lax.rsqrt(ms + EPS) * g


def _inproj_kernel(x_ref, mod_ref, ng_ref, wn_ref, wt_ref, cos_ref, sin_ref, qg_ref, kg_ref,
                   ka_ref, gate_ref, qat_ref, vat_ref, qbt_ref, kb_ref, vbt_ref):
    tm = x_ref.shape[1]
    x = x_ref[0]
    shift = mod_ref[0, 0:1, :]
    scale = mod_ref[0, 1:2, :]
    h = (_rms_rows(x, ng_ref[...]) * (1.0 + scale) + shift).astype(BF16)

    pn = jnp.dot(h, wn_ref[...], preferred_element_type=F32)
    ka_ref[0] = pn[:, 0:512].astype(BF16)
    g = pn[:, 512:N_NAT]
    gate_ref[0] = g * jax.nn.sigmoid(g)

    pt = lax.dot_general(wt_ref[...], h, (((1,), (1,)), ((), ())), preferred_element_type=F32)
    qat_ref[0] = (pt[0:512] * Q_SCALE).astype(BF16)
    ones = jnp.ones((ONES_ROWS, tm), BF16)
    for hh in range(A_HEADS):
        vat_ref[0, hh, 0:A_V_DIM, :] = pt[512 + hh * A_V_DIM:512 + (hh + 1) * A_V_DIM].astype(BF16)
        vat_ref[0, hh, A_V_DIM:A_V_DIM + ONES_ROWS, :] = ones

    cos = cos_ref[...]
    sin = sin_ref[...]
    qg = qg_ref[...]
    for hh in range(B_HEADS):
        xh = pt[1024 + hh * 64:1024 + (hh + 1) * 64]
        qbt_ref[0, hh * 64:(hh + 1) * 64, :] = (_rope_t(_norm_t(xh, qg), cos, sin) * Q_SCALE).astype(BF16)
    kg = kg_ref[...]
    kparts = []
    for n in range(B_KV_HEADS):
        xh = pt[1536 + n * 64:1536 + (n + 1) * 64]
        kparts.append(_rope_t(_norm_t(xh, kg), cos, sin))
    kb_ref[0] = jnp.concatenate(kparts, axis=0).T.astype(BF16)
    for n in range(B_KV_HEADS):
        vbt_ref[0, n, 0:64, :] = pt[1664 + n * 64:1664 + (n + 1) * 64].astype(BF16)
        vbt_ref[0, n, 64:64 + ONES_ROWS, :] = ones


def _inproj(x, mod, ng, wn, wt, cos_t, sin_t, qg_b, kg_b, tm):
    b, s, _ = x.shape
    grid = (b, s // tm)
    const2 = lambda i, j: (0, 0)
    return pl.pallas_call(
        _inproj_kernel,
        grid=grid,
        in_specs=[
            pl.BlockSpec((1, tm, D_MODEL), lambda i, j: (i, j, 0)),
            pl.BlockSpec((1, 3, D_MODEL), lambda i, j: (i, 0, 0)),
            pl.BlockSpec((1, D_MODEL), const2),
            pl.BlockSpec((D_MODEL, N_NAT), const2),
            pl.BlockSpec((N_TR, D_MODEL), const2),
            pl.BlockSpec((B_HEAD_DIM, tm), lambda i, j: (0, j)),
            pl.BlockSpec((B_HEAD_DIM, tm), lambda i, j: (0, j)),
            pl.BlockSpec((B_HEAD_DIM, tm), const2),
            pl.BlockSpec((B_HEAD_DIM, tm), const2),
        ],
        out_specs=[
            pl.BlockSpec((1, tm, A_WIDTH), lambda i, j: (i, j, 0)),
            pl.BlockSpec((1, tm, D_MODEL), lambda i, j: (i, j, 0)),
            pl.BlockSpec((1, A_WIDTH, tm), lambda i, j: (i, 0, j)),
            pl.BlockSpec((1, A_HEADS, A_V_DIM + ONES_ROWS, tm), lambda i, j: (i, 0, 0, j)),
            pl.BlockSpec((1, B_WIDTH, tm), lambda i, j: (i, 0, j)),
            pl.BlockSpec((1, tm, B_KV_HEADS * B_HEAD_DIM), lambda i, j: (i, j, 0)),
            pl.BlockSpec((1, B_KV_HEADS, B_HEAD_DIM + ONES_ROWS, tm), lambda i, j: (i, 0, 0, j)),
        ],
        out_shape=[
            jax.ShapeDtypeStruct((b, s, A_WIDTH), BF16),
            jax.ShapeDtypeStruct((b, s, D_MODEL), F32),
            jax.ShapeDtypeStruct((b, A_WIDTH, s), BF16),
            jax.ShapeDtypeStruct((b, A_HEADS, A_V_DIM + ONES_ROWS, s), BF16),
            jax.ShapeDtypeStruct((b, B_WIDTH, s), BF16),
            jax.ShapeDtypeStruct((b, s, B_KV_HEADS * B_HEAD_DIM), BF16),
            jax.ShapeDtypeStruct((b, B_KV_HEADS, B_HEAD_DIM + ONES_ROWS, s), BF16),
        ],
        compiler_params=pltpu.CompilerParams(
            dimension_semantics=("arbitrary", "arbitrary"), vmem_limit_bytes=VMEM_LIMIT_BYTES),
        name="inproj",
    )(x, mod, ng, wn, wt, cos_t, sin_t, qg_b, kg_b)


def _attn_a_kernel(scal_ref, qt_ref, k_ref, vt_ref, bias_ref, sg_ref, o_ref, wq_ref, acc_ref, s0_ref, s1_ref,
                   *, tq, tk, n_chunks, ratio, unroll, out_scale):
    qi = pl.program_id(2)
    q = qt_ref[0]
    zero = jnp.zeros((A_HEAD_DIM, tq), BF16)
    wq_ref[0:64, 0:tq] = q[0:64]
    wq_ref[0:64, tq:2 * tq] = zero
    wq_ref[64:128, 0:tq] = zero
    wq_ref[64:128, tq:2 * tq] = q[64:128]
    acc_ref[...] = jnp.zeros(acc_ref.shape, F32)

    lam = scal_ref[0]
    log_ratio = ratio.bit_length() - 1
    c_own = lax.shift_right_logical(qi, log_ratio)
    r = lax.bitwise_and(qi, ratio - 1)

    def qk(c, s_ref):
        start = pl.multiple_of(c * tk, tk)
        d = c - c_own
        tile = jnp.where(d < -1, 3 * ratio, jnp.where(d > 1, 3 * ratio + 1, (d + 1) * ratio + r))
        t = bias_ref[0, tile]
        s = jnp.dot(k_ref[0, pl.ds(start, tk), :], wq_ref[...], preferred_element_type=F32)
        s = s + jnp.concatenate([t, t], axis=1)
        s_ref[...] = s
        return jnp.max(s, axis=0, keepdims=True)

    def softmax_pv(s_ref, mc, m, c):
        mn = jnp.maximum(m, mc)
        alpha = jnp.exp2(m - mn)
        p = jnp.exp2(s_ref[...] - mn).astype(BF16)
        start = pl.multiple_of(c * tk, tk)
        pv = jnp.dot(vt_ref[0, 0, :, pl.ds(start, tk)], p, preferred_element_type=F32)
        acc_ref[...] = acc_ref[...] * alpha + pv
        return mn

    bufs = (s0_ref, s1_ref)

    def body(i, carry):
        m, mc = carry
        c0 = unroll * i
        for u in range(unroll):
            mc_next = qk(jnp.minimum(c0 + u + 1, n_chunks - 1), bufs[(u + 1) % 2])
            m = softmax_pv(bufs[u % 2], mc, m, c0 + u)
            mc = mc_next
        return m, mc

    mc_first = qk(0, s0_ref)
    lax.fori_loop(0, n_chunks // unroll, body, (jnp.full((1, 2 * tq), M_INIT, F32), mc_first))

    acc = acc_ref[...]
    o = acc[0:A_V_DIM, :] / acc[A_V_DIM:A_V_DIM + 1, :]
    w = o[:, 0:tq] - lam * o[:, tq:2 * tq]
    y = _norm_t(w, sg_ref[...]) * out_scale
    o_ref[0] = y.T


def _attn_a(scal, qat, ka, vat, bias_tiles, sg_b, tq, tk, out_scale):
    b, s, _ = ka.shape
    n_chunks = s // tk
    ratio = tk // tq
    unroll = 4 if n_chunks % 4 == 0 else 2
    assert n_chunks % unroll == 0 and ratio * tq == tk and ratio & (ratio - 1) == 0
    kern = functools.partial(_attn_a_kernel, tq=tq, tk=tk, n_chunks=n_chunks, ratio=ratio, unroll=unroll,
                             out_scale=out_scale)
    return pl.pallas_call(
        kern,
        grid=(b, A_HEADS, s // tq),
        in_specs=[
            pl.BlockSpec(memory_space=pltpu.SMEM),
            pl.BlockSpec((1, 2 * A_HEAD_DIM, tq), lambda i, hh, j: (i, hh, j)),
            pl.BlockSpec((1, s, 2 * A_HEAD_DIM), lambda i, hh, j: (i, 0, hh)),
            pl.BlockSpec((1, 1, A_V_DIM + ONES_ROWS, s), lambda i, hh, j: (i, hh, 0, 0)),
            pl.BlockSpec((1, 3 * ratio + 2, tk, tq), lambda i, hh, j: (hh, 0, 0, 0)),
            pl.BlockSpec((A_V_DIM, tq), lambda i, hh, j: (0, 0)),
        ],
        out_specs=pl.BlockSpec((1, tq, A_V_DIM), lambda i, hh, j: (i, j, hh)),
        out_shape=jax.ShapeDtypeStruct((b, s, A_WIDTH), F32),
        scratch_shapes=[
            pltpu.VMEM((2 * A_HEAD_DIM, 2 * tq), BF16),
            pltpu.VMEM((A_V_DIM + ONES_ROWS, 2 * tq), F32),
            pltpu.VMEM((tk, 2 * tq), F32),
            pltpu.VMEM((tk, 2 * tq), F32),
        ],
        compiler_params=pltpu.CompilerParams(
            dimension_semantics=("arbitrary", "arbitrary", "arbitrary"), vmem_limit_bytes=VMEM_LIMIT_BYTES),
        name="attn_a",
    )(scal, qat, ka, vat, bias_tiles, sg_b)


def _attn_b_kernel(qt_ref, k_ref, vt_ref, o_ref, wq_ref, acc0_ref, acc1_ref, s0_ref, s1_ref,
                   *, tq, tk, n_chunks, unroll):
    half = B_GROUP * tq
    wq_ref[...] = jnp.zeros(wq_ref.shape, BF16)
    for hh in range(B_HEADS):
        n = hh // B_GROUP
        wq_ref[n * 64:(n + 1) * 64, hh * tq:(hh + 1) * tq] = qt_ref[0, hh * 64:(hh + 1) * 64, :]
    acc0_ref[...] = jnp.zeros(acc0_ref.shape, F32)
    acc1_ref[...] = jnp.zeros(acc1_ref.shape, F32)

    def qk(c, s_ref):
        start = pl.multiple_of(c * tk, tk)
        s = jnp.dot(k_ref[0, pl.ds(start, tk), :], wq_ref[...], preferred_element_type=F32)
        s_ref[...] = s
        return jnp.max(s, axis=0, keepdims=True)

    def softmax_pv(s_ref, mc, m, c):
        mn = jnp.maximum(m, mc)
        alpha = jnp.exp2(m - mn)
        p = jnp.exp2(s_ref[...] - mn).astype(BF16)
        start = pl.multiple_of(c * tk, tk)
        v0 = vt_ref[0, 0, :, pl.ds(start, tk)]
        v1 = vt_ref[0, 1, :, pl.ds(start, tk)]
        acc0_ref[...] = acc0_ref[...] * alpha[:, 0:half] + jnp.dot(v0, p[:, 0:half], preferred_element_type=F32)
        acc1_ref[...] = acc1_ref[...] * alpha[:, half:] + jnp.dot(v1, p[:, half:], preferred_element_type=F32)
        return mn

    bufs = (s0_ref, s1_ref)

    def body(i, carry):
        m, mc = carry
        c0 = unroll * i
        for u in range(unroll):
            mc_next = qk(jnp.minimum(c0 + u + 1, n_chunks - 1), bufs[(u + 1) % 2])
            m = softmax_pv(bufs[u % 2], mc, m, c0 + u)
            mc = mc_next
        return m, mc

    mc_first = qk(0, s0_ref)
    lax.fori_loop(0, n_chunks // unroll, body, (jnp.full((1, B_HEADS * tq), M_INIT, F32), mc_first))

    parts = []
    for acc_ref in (acc0_ref, acc1_ref):
        acc = acc_ref[...]
        o = acc[0:64, :] / acc[64:65, :]
        for g in range(B_GROUP):
            parts.append(o[:, g * tq:(g + 1) * tq])
    o_ref[0] = jnp.concatenate(parts, axis=0).T


def _attn_b(qbt, kb, vbt, tq, tk):
    b, s, _ = kb.shape
    n_chunks = s // tk
    unroll = 4 if n_chunks % 4 == 0 else 2
    assert n_chunks % unroll == 0
    kern = functools.partial(_attn_b_kernel, tq=tq, tk=tk, n_chunks=n_chunks, unroll=unroll)
    return pl.pallas_call(
        kern,
        grid=(b, s // tq),
        in_specs=[
            pl.BlockSpec((1, B_WIDTH, tq), lambda i, j: (i, 0, j)),
            pl.BlockSpec((1, s, B_KV_HEADS * B_HEAD_DIM), lambda i, j: (i, 0, 0)),
            pl.BlockSpec((1, B_KV_HEADS, B_HEAD_DIM + ONES_ROWS, s), lambda i, j: (i, 0, 0, 0)),
        ],
        out_specs=pl.BlockSpec((1, tq, B_WIDTH), lambda i, j: (i, j, 0)),
        out_shape=jax.ShapeDtypeStruct((b, s, B_WIDTH), F32),
        scratch_shapes=[
            pltpu.VMEM((B_KV_HEADS * B_HEAD_DIM, B_HEADS * tq), BF16),
            pltpu.VMEM((B_HEAD_DIM + ONES_ROWS, B_GROUP * tq), F32),
            pltpu.VMEM((B_HEAD_DIM + ONES_ROWS, B_GROUP * tq), F32),
            pltpu.VMEM((tk, B_HEADS * tq), F32),
            pltpu.VMEM((tk, B_HEADS * tq), F32),
        ],
        compiler_params=pltpu.CompilerParams(
            dimension_semantics=("arbitrary", "arbitrary"), vmem_limit_bytes=VMEM_LIMIT_BYTES),
        name="attn_b",
    )(qbt, kb, vbt)


def _outproj_kernel(oa_ref, ob_ref, gate_ref, x_ref, mod_ref, w_ref, fg_ref, o_ref, *, final):
    ma = (oa_ref[0] * gate_ref[0, :, 0:A_WIDTH]).astype(BF16)
    mb = (ob_ref[0] * gate_ref[0, :, A_WIDTH:D_MODEL]).astype(BF16)
    y = (jnp.dot(ma, w_ref[0:A_WIDTH, :], preferred_element_type=F32)
         + jnp.dot(mb, w_ref[A_WIDTH:D_MODEL, :], preferred_element_type=F32))
    xn = x_ref[0] + mod_ref[0, 2:3, :] * y
    if final:
        xn = _rms_rows(xn, fg_ref[...])
    o_ref[0] = xn


def _outproj(oa, ob, gate, x, mod, w, fg, tm, final):
    b, s, _ = x.shape
    tok = lambda i, j: (i, j, 0)
    return pl.pallas_call(
        functools.partial(_outproj_kernel, final=final),
        grid=(b, s // tm),
        in_specs=[
            pl.BlockSpec((1, tm, A_WIDTH), tok),
            pl.BlockSpec((1, tm, B_WIDTH), tok),
            pl.BlockSpec((1, tm, D_MODEL), tok),
            pl.BlockSpec((1, tm, D_MODEL), tok),
            pl.BlockSpec((1, 3, D_MODEL), lambda i, j: (i, 0, 0)),
            pl.BlockSpec((D_MODEL, D_MODEL), lambda i, j: (0, 0)),
            pl.BlockSpec((1, D_MODEL), lambda i, j: (0, 0)),
        ],
        out_specs=pl.BlockSpec((1, tm, D_MODEL), tok),
        out_shape=jax.ShapeDtypeStruct((b, s, D_MODEL), F32),
        compiler_params=pltpu.CompilerParams(
            dimension_semantics=("arbitrary", "arbitrary"), vmem_limit_bytes=VMEM_LIMIT_BYTES),
        name="outproj",
    )(oa, ob, gate, x, mod, w, fg)


def _t5_bucket(rel):
    half = NUM_BUCKETS // 2
    max_exact = half // 2
    ret = jnp.where(rel > 0, half, 0)
    n = jnp.abs(rel)
    nf = jnp.maximum(n, 1).astype(jnp.float32)
    large = max_exact + (jnp.log(nf / max_exact) / math.log(MAX_DISTANCE / max_exact)
                         * (half - max_exact)).astype(jnp.int32)
    large = jnp.minimum(large, half - 1)
    return ret + jnp.where(n < max_exact, n, large)


def _bias_tables(rel_table, s, tq, tk):
    ratio = tk // tq
    offsets = jnp.arange(-(s - 1), s, dtype=jnp.int32)
    boff = rel_table[_t5_bucket(offsets)].T.astype(F32) * LOG2E
    j = jnp.arange(tk, dtype=jnp.int32)[:, None]
    i = jnp.arange(tq, dtype=jnp.int32)[None, :]
    tiles = []
    for dc in (-1, 0, 1):
        for r in range(ratio):
            rel = dc * tk - r * tq + j - i
            tiles.append(boff[:, jnp.clip(rel + (s - 1), 0, 2 * s - 2)])
    assert tk >= MAX_DISTANCE and s - 1 >= MAX_DISTANCE
    for col in (0, 2 * s - 2):
        tiles.append(jnp.broadcast_to(boff[:, col][:, None, None], (A_HEADS, tk, tq)))
    return jnp.stack(tiles, axis=1)


def _rope_tables_t(s):
    rows = s // GRID_W
    row = jnp.repeat(jnp.arange(rows), GRID_W).astype(F32)
    col = jnp.tile(jnp.arange(GRID_W), rows).astype(F32)
    axis_dim = B_HEAD_DIM // 2
    inv_freq = ROPE_THETA ** (-jnp.arange(0, axis_dim, 2, dtype=F32) / axis_dim)
    ang_r = row[:, None] * inv_freq[None, :]
    ang_c = col[:, None] * inv_freq[None, :]
    ang = jnp.concatenate([ang_r, ang_r, ang_c, ang_c], axis=-1)
    return jnp.cos(ang).T, jnp.sin(ang).T


def _tile_sizes(s):
    tm = 512
    tk_a = 512 if s >= 8192 else 256
    return tm, (256, tk_a), (128, 256)


def _trunk(x, mod, lam, rel_table, norm_g, wn, wt, subln_g, q_norm_g, k_norm_g, w_out, final_g):
    b, s, _ = x.shape
    tm, (tq_a, tk_a), (tq_b, tk_b) = _tile_sizes(s)
    cos_t, sin_t = _rope_tables_t(s)
    bias_tiles = _bias_tables(rel_table, s, tq_a, tk_a)
    fg = final_g.reshape(1, D_MODEL)
    for l in range(DEPTH):
        lam_init = 0.8 - 0.6 * math.exp(-0.3 * l)
        qg_b = jnp.broadcast_to(q_norm_g[l][:, None], (B_HEAD_DIM, tm))
        kg_b = jnp.broadcast_to(k_norm_g[l][:, None], (B_HEAD_DIM, tm))
        sg_b = jnp.broadcast_to(subln_g[l][:, None], (A_V_DIM, tq_a))
        ka, gate, qat, vat, qbt, kb, vbt = _inproj(
            x, mod[l], norm_g[l].reshape(1, D_MODEL), wn[l], wt[l], cos_t, sin_t, qg_b, kg_b, tm)
        oa = _attn_a(lam[l:l + 1], qat, ka, vat, bias_tiles, sg_b, tq_a, tk_a, 1.0 - lam_init)
        ob = _attn_b(qbt, kb, vbt, tq_b, tk_b)
        x = _outproj(oa, ob, gate, x, mod[l], w_out[l], fg, tm, final=(l == DEPTH - 1))
    return x


def kernel(x_prompt, x_sample, c_prompt, c_sample, rel_table, norm_g, w_ada, b_ada, w_in, lam_q1, lam_k1, lam_q2,
           lam_k2, subln_g, q_norm_g, k_norm_g, w_out, final_g):
    bp = x_prompt.shape[0]
    bs = x_sample.shape[0]
    rows = -(-(bp + bs) // 8) * 8
    c_all = jnp.concatenate([c_prompt, c_sample, jnp.zeros((rows - bp - bs, D_MODEL), F32)], axis=0)
    mod_all = _adaln_mod(c_all, w_ada.astype(BF16), b_ada)
    mod_p = mod_all[:, :bp].reshape(DEPTH, bp, 3, D_MODEL)
    mod_s = mod_all[:, bp:bp + bs].reshape(DEPTH, bs, 3, D_MODEL)
    lam = _lambdas(lam_q1, lam_k1, lam_q2, lam_k2)

    w_bf = w_in.astype(BF16)
    wn = jnp.concatenate([w_bf[:, :, _C_KA:_C_VA], w_bf[:, :, _C_GA:_C_QB], w_bf[:, :, _C_GB:D_IN]], axis=2)
    wt = jnp.concatenate([w_bf[:, :, _C_QA:_C_KA], w_bf[:, :, _C_VA:_C_GA], w_bf[:, :, _C_QB:_C_GB]], axis=2)
    wt = jnp.swapaxes(wt, 1, 2)
    w_out_bf = w_out.astype(BF16)

    args = (rel_table, norm_g, wn, wt, subln_g, q_norm_g, k_norm_g, w_out_bf, final_g)
    y_prompt = _trunk(x_prompt, mod_p, lam, *args)
    y_sample = _trunk(x_sample, mod_s, lam, *args)
    return (y_prompt, y_sample)
```

```python
import functools
import math

import jax
import jax.numpy as jnp
from jax import lax
from jax.experimental import pallas as pl
from jax.experimental.pallas import tpu as pltpu

F32 = jnp.float32
BF16 = jnp.bfloat16

D_MODEL = 1024
DEPTH = 4
GRID_W = 64
A_WIDTH = 512
A_V_DIM = 128
A_HEAD_DIM = 64
A_HEADS = 4
B_WIDTH = 512
B_HEAD_DIM = 64
B_HEADS = 8
B_KV_HEADS = 2
B_GROUP = 4
NUM_BUCKETS = 32
MAX_DISTANCE = 128
ROPE_THETA = 10000.0
EPS = 1e-6

LOG2E = 1.4426950408889634
Q_SCALE = (A_HEAD_DIM ** -0.5) * LOG2E
M_INIT = -1e30
ONES_ROWS = 16
VMEM_LIMIT_BYTES = 56 * 1024 * 1024

_C_QA, _C_KA, _C_VA, _C_GA, _C_QB, _C_KB, _C_VB, _C_GB = 0, 512, 1024, 1536, 2048, 2560, 2688, 2816
D_IN = 3328
N_NAT = 1536
N_TR = 1792


def _rms_rows(x, g):
    ms = jnp.mean(x * x, axis=-1, keepdims=True)
    return x * lax.rsqrt(ms + EPS) * g


def _mod_kernel(c_ref, w_ref, b_ref, o_ref):
    c = c_ref[...]
    c_act = (c * jax.nn.sigmoid(c)).astype(BF16)
    o_ref[0] = jnp.dot(c_act, w_ref[0], preferred_element_type=F32) + b_ref[0]


def _adaln_mod(c_all, w_ada_bf, b_ada):
    rows = c_all.shape[0]
    return pl.pallas_call(
        _mod_kernel,
        grid=(DEPTH, 3),
        in_specs=[
            pl.BlockSpec((rows, D_MODEL), lambda l, j: (0, 0)),
            pl.BlockSpec((1, D_MODEL, D_MODEL), lambda l, j: (l, 0, j)),
            pl.BlockSpec((1, 1, D_MODEL), lambda l, j: (l, 0, j)),
        ],
        out_specs=pl.BlockSpec((1, rows, D_MODEL), lambda l, j: (l, 0, j)),
        out_shape=jax.ShapeDtypeStruct((DEPTH, rows, 3 * D_MODEL), F32),
        name="adaln_mod",
    )(c_all, w_ada_bf, b_ada.reshape(DEPTH, 1, 3 * D_MODEL))


def _lam_kernel(q1_ref, k1_ref, q2_ref, k2_ref, init_ref, o_ref):
    s1 = jnp.sum(q1_ref[...] * k1_ref[...], axis=-1, keepdims=True)
    s2 = jnp.sum(q2_ref[...] * k2_ref[...], axis=-1, keepdims=True)
    lam = jnp.exp(s1) - jnp.exp(s2) + init_ref[...][:, 0:1]
    o_ref[...] = jnp.broadcast_to(lam, o_ref.shape)


def _lambdas(lam_q1, lam_k1, lam_q2, lam_k2):
    init = jnp.asarray([[0.8 - 0.6 * math.exp(-0.3 * l)] * 128 for l in range(DEPTH)], F32)
    out = pl.pallas_call(
        _lam_kernel,
        out_shape=jax.ShapeDtypeStruct((DEPTH, 128), F32),
        name="diff_lambda",
    )(lam_q1, lam_k1, lam_q2, lam_k2, init)
    return out[:, 0]


def _rope_t(x, cos, sin):
    rot = jnp.concatenate([-x[16:32], x[0:16], -x[48:64], x[32:48]], axis=0)
    return x * cos + rot * sin


def _norm_t(x, g):
    ms = jnp.mean(x * x, axis=0, keepdims=True)
    return x * lax.rsqrt(ms + EPS) * g


def _inproj_kernel(x_ref, mod_ref, ng_ref, wn_ref, wt_ref, cos_ref, sin_ref, qg_ref, kg_ref,
                   ka_ref, gate_ref, qat_ref, vat_ref, qbt_ref, kb_ref, vbt_ref):
    tm = x_ref.shape[1]
    x = x_ref[0]
    shift = mod_ref[0, 0:1, :]
    scale = mod_ref[0, 1:2, :]
    h = (_rms_rows(x, ng_ref[...]) * (1.0 + scale) + shift).astype(BF16)

    pn = jnp.dot(h, wn_ref[...], preferred_element_type=F32)
    ka_ref[0] = pn[:, 0:512].astype(BF16)
    g = pn[:, 512:N_NAT]
    gate_ref[0] = g * jax.nn.sigmoid(g)

    pt = lax.dot_general(wt_ref[...], h, (((1,), (1,)), ((), ())), preferred_element_type=F32)
    qat_ref[0] = (pt[0:512] * Q_SCALE).astype(BF16)
    ones = jnp.ones((ONES_ROWS, tm), BF16)
    for hh in range(A_HEADS):
        vat_ref[0, hh, 0:A_V_DIM, :] = pt[512 + hh * A_V_DIM:512 + (hh + 1) * A_V_DIM].astype(BF16)
        vat_ref[0, hh, A_V_DIM:A_V_DIM + ONES_ROWS, :] = ones

    cos = cos_ref[...]
    sin = sin_ref[...]
    qg = qg_ref[...]
    for hh in range(B_HEADS):
        xh = pt[1024 + hh * 64:1024 + (hh + 1) * 64]
        qbt_ref[0, hh * 64:(hh + 1) * 64, :] = (_rope_t(_norm_t(xh, qg), cos, sin) * Q_SCALE).astype(BF16)
    kg = kg_ref[...]
    kparts = []
    for n in range(B_KV_HEADS):
        xh = pt[1536 + n * 64:1536 + (n + 1) * 64]
        kparts.append(_rope_t(_norm_t(xh, kg), cos, sin))
    kb_ref[0] = jnp.concatenate(kparts, axis=0).T.astype(BF16)
    for n in range(B_KV_HEADS):
        vbt_ref[0, n, 0:64, :] = pt[1664 + n * 64:1664 + (n + 1) * 64].astype(BF16)
        vbt_ref[0, n, 64:64 + ONES_ROWS, :] = ones


def _inproj(x, mod, ng, wn, wt, cos_t, sin_t, qg_b, kg_b, tm):
    b, s, _ = x.shape
    grid = (b, s // tm)
    const2 = lambda i, j: (0, 0)
    return pl.pallas_call(
        _inproj_kernel,
        grid=grid,
        in_specs=[
            pl.BlockSpec((1, tm, D_MODEL), lambda i, j: (i, j, 0)),
            pl.BlockSpec((1, 3, D_MODEL), lambda i, j: (i, 0, 0)),
            pl.BlockSpec((1, D_MODEL), const2),
            pl.BlockSpec((D_MODEL, N_NAT), const2),
            pl.BlockSpec((N_TR, D_MODEL), const2),
            pl.BlockSpec((B_HEAD_DIM, tm), lambda i, j: (0, j)),
            pl.BlockSpec((B_HEAD_DIM, tm), lambda i, j: (0, j)),
            pl.BlockSpec((B_HEAD_DIM, tm), const2),
            pl.BlockSpec((B_HEAD_DIM, tm), const2),
        ],
        out_specs=[
            pl.BlockSpec((1, tm, A_WIDTH), lambda i, j: (i, j, 0)),
            pl.BlockSpec((1, tm, D_MODEL), lambda i, j: (i, j, 0)),
            pl.BlockSpec((1, A_WIDTH, tm), lambda i, j: (i, 0, j)),
            pl.BlockSpec((1, A_HEADS, A_V_DIM + ONES_ROWS, tm), lambda i, j: (i, 0, 0, j)),
            pl.BlockSpec((1, B_WIDTH, tm), lambda i, j: (i, 0, j)),
            pl.BlockSpec((1, tm, B_KV_HEADS * B_HEAD_DIM), lambda i, j: (i, j, 0)),
            pl.BlockSpec((1, B_KV_HEADS, B_HEAD_DIM + ONES_ROWS, tm), lambda i, j: (i, 0, 0, j)),
        ],
        out_shape=[
            jax.ShapeDtypeStruct((b, s, A_WIDTH), BF16),
            jax.ShapeDtypeStruct((b, s, D_MODEL), F32),
            jax.ShapeDtypeStruct((b, A_WIDTH, s), BF16),
            jax.ShapeDtypeStruct((b, A_HEADS, A_V_DIM + ONES_ROWS, s), BF16),
            jax.ShapeDtypeStruct((b, B_WIDTH, s), BF16),
            jax.ShapeDtypeStruct((b, s, B_KV_HEADS * B_HEAD_DIM), BF16),
            jax.ShapeDtypeStruct((b, B_KV_HEADS, B_HEAD_DIM + ONES_ROWS, s), BF16),
        ],
        compiler_params=pltpu.CompilerParams(
            dimension_semantics=("arbitrary", "arbitrary"), vmem_limit_bytes=VMEM_LIMIT_BYTES),
        name="inproj",
    )(x, mod, ng, wn, wt, cos_t, sin_t, qg_b, kg_b)


def _attn_a_kernel(scal_ref, qt_ref, k_ref, vt_ref, bias_ref, sg_ref, o_ref, wq_ref, acc_ref,
                   s0_ref, s1_ref, p0_ref, p1_ref, *, tq, tk, n_chunks, ratio, unroll, out_scale):
    s_bufs = (s0_ref, s1_ref)
    p_bufs = (p0_ref, p1_ref)
    qi = pl.program_id(2)
    q = qt_ref[0]
    zero = jnp.zeros((A_HEAD_DIM, tq), BF16)
    wq_ref[0:64, 0:tq] = q[0:64]
    wq_ref[0:64, tq:2 * tq] = zero
    wq_ref[64:128, 0:tq] = zero
    wq_ref[64:128, tq:2 * tq] = q[64:128]
    acc_ref[...] = jnp.zeros(acc_ref.shape, F32)

    lam = scal_ref[0]
    log_ratio = ratio.bit_length() - 1
    c_own = lax.shift_right_logical(qi, log_ratio)
    r = lax.bitwise_and(qi, ratio - 1)

    def qk(c, s_ref):
        start = pl.multiple_of(c * tk, tk)
        d = c - c_own
        tile = jnp.where(d < -1, 3 * ratio, jnp.where(d > 1, 3 * ratio + 1, (d + 1) * ratio + r))
        t = bias_ref[0, tile]
        s = jnp.dot(k_ref[0, pl.ds(start, tk), :], wq_ref[...], preferred_element_type=F32)
        s = s + jnp.concatenate([t, t], axis=1)
        s_ref[...] = s
        return jnp.max(s, axis=0, keepdims=True)

    def probs(s_ref, p_ref, mc, m):
        mn = jnp.maximum(m, mc)
        p_ref[...] = jnp.exp2(s_ref[...] - mn).astype(BF16)
        return mn, jnp.exp2(m - mn)

    def pv(c, p_ref, alpha):
        start = pl.multiple_of(c * tk, tk)
        acc_ref[...] = acc_ref[...] * alpha + jnp.dot(vt_ref[0, 0, :, pl.ds(start, tk)], p_ref[...],
                                                      preferred_element_type=F32)

    def body(i, carry):
        m, mc, alpha_prev = carry
        c0 = unroll * i
        for u in range(unroll):
            c = c0 + u
            pv(jnp.maximum(c - 1, 0), p_bufs[(u + 1) % 2], alpha_prev)
            mc_next = qk(jnp.minimum(c + 1, n_chunks - 1), s_bufs[(u + 1) % 2])
            m, alpha_prev = probs(s_bufs[u % 2], p_bufs[u % 2], mc, m)
            mc = mc_next
        return m, mc, alpha_prev

    p_bufs[1][...] = jnp.zeros(p_bufs[1].shape, BF16)
    mc_first = qk(0, s_bufs[0])
    init = (jnp.full((1, 2 * tq), M_INIT, F32), mc_first, jnp.ones((1, 2 * tq), F32))
    _, _, alpha_last = lax.fori_loop(0, n_chunks // unroll, body, init)
    pv(n_chunks - 1, p_bufs[(n_chunks - 1) % 2], alpha_last)

    acc = acc_ref[...]
    o = acc[0:A_V_DIM, :] / acc[A_V_DIM:A_V_DIM + 1, :]
    w = o[:, 0:tq] - lam * o[:, tq:2 * tq]
    y = _norm_t(w, sg_ref[...]) * out_scale
    o_ref[0] = y.T


def _attn_a(scal, qat, ka, vat, bias_tiles, sg_b, tq, tk, out_scale):
    b, s, _ = ka.shape
    n_chunks = s // tk
    ratio = tk // tq
    unroll = 8 if n_chunks % 8 == 0 else 2
    assert n_chunks % unroll == 0 and ratio * tq == tk and ratio & (ratio - 1) == 0
    kern = functools.partial(_attn_a_kernel, tq=tq, tk=tk, n_chunks=n_chunks, ratio=ratio, unroll=unroll,
                             out_scale=out_scale)
    return pl.pallas_call(
        kern,
        grid=(b, A_HEADS, s // tq),
        in_specs=[
            pl.BlockSpec(memory_space=pltpu.SMEM),
            pl.BlockSpec((1, 2 * A_HEAD_DIM, tq), lambda i, hh, j: (i, hh, j)),
            pl.BlockSpec((1, s, 2 * A_HEAD_DIM), lambda i, hh, j: (i, 0, hh)),
            pl.BlockSpec((1, 1, A_V_DIM + ONES_ROWS, s), lambda i, hh, j: (i, hh, 0, 0)),
            pl.BlockSpec((1, 3 * ratio + 2, tk, tq), lambda i, hh, j: (hh, 0, 0, 0)),
            pl.BlockSpec((A_V_DIM, tq), lambda i, hh, j: (0, 0)),
        ],
        out_specs=pl.BlockSpec((1, tq, A_V_DIM), lambda i, hh, j: (i, j, hh)),
        out_shape=jax.ShapeDtypeStruct((b, s, A_WIDTH), F32),
        scratch_shapes=[
            pltpu.VMEM((2 * A_HEAD_DIM, 2 * tq), BF16),
            pltpu.VMEM((A_V_DIM + ONES_ROWS, 2 * tq), F32),
        ] + [pltpu.VMEM((tk, 2 * tq), F32)] * 2 + [pltpu.VMEM((tk, 2 * tq), BF16)] * 2,
        compiler_params=pltpu.CompilerParams(
            dimension_semantics=("arbitrary", "arbitrary", "arbitrary"), vmem_limit_bytes=VMEM_LIMIT_BYTES),
        name="attn_a",
    )(scal, qat, ka, vat, bias_tiles, sg_b)


def _attn_b_kernel(qt_ref, k_ref, vt_ref, o_ref, wq_ref, acc0_ref, acc1_ref, s0_ref, s1_ref, p0_ref, p1_ref,
                   *, tq, tk, n_chunks, unroll):
    half = B_GROUP * tq
    lanes = B_HEADS * tq
    wq_ref[...] = jnp.zeros(wq_ref.shape, BF16)
    for hh in range(B_HEADS):
        n = hh // B_GROUP
        wq_ref[n * 64:(n + 1) * 64, hh * tq:(hh + 1) * tq] = qt_ref[0, hh * 64:(hh + 1) * 64, :]
    acc0_ref[...] = jnp.zeros(acc0_ref.shape, F32)
    acc1_ref[...] = jnp.zeros(acc1_ref.shape, F32)
    s_bufs = (s0_ref, s1_ref)
    p_bufs = (p0_ref, p1_ref)

    def qk(c, s_ref):
        start = pl.multiple_of(c * tk, tk)
        s = jnp.dot(k_ref[0, pl.ds(start, tk), :], wq_ref[...], preferred_element_type=F32)
        s_ref[...] = s
        return jnp.max(s, axis=0, keepdims=True)

    def probs(s_ref, p_ref, mc, m):
        mn = jnp.maximum(m, mc)
        p_ref[...] = jnp.exp2(s_ref[...] - mn).astype(BF16)
        return mn, jnp.exp2(m - mn)

    def pv(c, p_ref, alpha):
        start = pl.multiple_of(c * tk, tk)
        p = p_ref[...]
        v0 = vt_ref[0, 0, :, pl.ds(start, tk)]
        v1 = vt_ref[0, 1, :, pl.ds(start, tk)]
        acc0_ref[...] = acc0_ref[...] * alpha[:, 0:half] + jnp.dot(v0, p[:, 0:half], preferred_element_type=F32)
        acc1_ref[...] = acc1_ref[...] * alpha[:, half:] + jnp.dot(v1, p[:, half:], preferred_element_type=F32)

    def body(i, carry):
        m, mc, alpha_prev = carry
        c0 = unroll * i
        for u in range(unroll):
            c = c0 + u
            pv(jnp.maximum(c - 1, 0), p_bufs[(u + 1) % 2], alpha_prev)
            mc_next = qk(jnp.minimum(c + 1, n_chunks - 1), s_bufs[(u + 1) % 2])
            m, alpha_prev = probs(s_bufs[u % 2], p_bufs[u % 2], mc, m)
            mc = mc_next
        return m, mc, alpha_prev

    p1_ref[...] = jnp.zeros(p1_ref.shape, BF16)
    mc_first = qk(0, s0_ref)
    init = (jnp.full((1, lanes), M_INIT, F32), mc_first, jnp.ones((1, lanes), F32))
    _, _, alpha_last = lax.fori_loop(0, n_chunks // unroll, body, init)
    pv(n_chunks - 1, p_bufs[(n_chunks - 1) % 2], alpha_last)

    parts = []
    for acc_ref in (acc0_ref, acc1_ref):
        acc = acc_ref[...]
        o = acc[0:64, :] / acc[64:65, :]
        for g in range(B_GROUP):
            parts.append(o[:, g * tq:(g + 1) * tq])
    o_ref[0] = jnp.concatenate(parts, axis=0).T


def _attn_b(qbt, kb, vbt, tq, tk):
    b, s, _ = kb.shape
    n_chunks = s // tk
    unroll = 8 if n_chunks % 8 == 0 else 2
    assert n_chunks % unroll == 0
    kern = functools.partial(_attn_b_kernel, tq=tq, tk=tk, n_chunks=n_chunks, unroll=unroll)
    return pl.pallas_call(
        kern,
        grid=(b, s // tq),
        in_specs=[
            pl.BlockSpec((1, B_WIDTH, tq), lambda i, j: (i, 0, j)),
            pl.BlockSpec((1, s, B_KV_HEADS * B_HEAD_DIM), lambda i, j: (i, 0, 0)),
            pl.BlockSpec((1, B_KV_HEADS, B_HEAD_DIM + ONES_ROWS, s), lambda i, j: (i, 0, 0, 0)),
        ],
        out_specs=pl.BlockSpec((1, tq, B_WIDTH), lambda i, j: (i, j, 0)),
        out_shape=jax.ShapeDtypeStruct((b, s, B_WIDTH), F32),
        scratch_shapes=[
            pltpu.VMEM((B_KV_HEADS * B_HEAD_DIM, B_HEADS * tq), BF16),
            pltpu.VMEM((B_HEAD_DIM + ONES_ROWS, B_GROUP * tq), F32),
            pltpu.VMEM((B_HEAD_DIM + ONES_ROWS, B_GROUP * tq), F32),
        ] + [pltpu.VMEM((tk, B_HEADS * tq), F32)] * 2 + [pltpu.VMEM((tk, B_HEADS * tq), BF16)] * 2,
        compiler_params=pltpu.CompilerParams(
            dimension_semantics=("arbitrary", "arbitrary"), vmem_limit_bytes=VMEM_LIMIT_BYTES),
        name="attn_b",
    )(qbt, kb, vbt)


def _outproj_kernel(oa_ref, ob_ref, gate_ref, x_ref, mod_ref, w_ref, fg_ref, o_ref, *, final):
    ma = (oa_ref[0] * gate_ref[0, :, 0:A_WIDTH]).astype(BF16)
    mb = (ob_ref[0] * gate_ref[0, :, A_WIDTH:D_MODEL]).astype(BF16)
    y = (jnp.dot(ma, w_ref[0:A_WIDTH, :], preferred_element_type=F32)
         + jnp.dot(mb, w_ref[A_WIDTH:D_MODEL, :], preferred_element_type=F32))
    xn = x_ref[0] + mod_ref[0, 2:3, :] * y
    if final:
        xn = _rms_rows(xn, fg_ref[...])
    o_ref[0] = xn


def _outproj(oa, ob, gate, x, mod, w, fg, tm, final):
    b, s, _ = x.shape
    tok = lambda i, j: (i, j, 0)
    return pl.pallas_call(
        functools.partial(_outproj_kernel, final=final),
        grid=(b, s // tm),
        in_specs=[
            pl.BlockSpec((1, tm, A_WIDTH), tok),
            pl.BlockSpec((1, tm, B_WIDTH), tok),
            pl.BlockSpec((1, tm, D_MODEL), tok),
            pl.BlockSpec((1, tm, D_MODEL), tok),
            pl.BlockSpec((1, 3, D_MODEL), lambda i, j: (i, 0, 0)),
            pl.BlockSpec((D_MODEL, D_MODEL), lambda i, j: (0, 0)),
            pl.BlockSpec((1, D_MODEL), lambda i, j: (0, 0)),
        ],
        out_specs=pl.BlockSpec((1, tm, D_MODEL), tok),
        out_shape=jax.ShapeDtypeStruct((b, s, D_MODEL), F32),
        compiler_params=pltpu.CompilerParams(
            dimension_semantics=("arbitrary", "arbitrary"), vmem_limit_bytes=VMEM_LIMIT_BYTES),
        name="outproj",
    )(oa, ob, gate, x, mod, w, fg)


def _t5_bucket(rel):
    half = NUM_BUCKETS // 2
    max_exact = half // 2
    ret = jnp.where(rel > 0, half, 0)
    n = jnp.abs(rel)
    nf = jnp.maximum(n, 1).astype(jnp.float32)
    large = max_exact + (jnp.log(nf / max_exact) / math.log(MAX_DISTANCE / max_exact)
                         * (half - max_exact)).astype(jnp.int32)
    large = jnp.minimum(large, half - 1)
    return ret + jnp.where(n < max_exact, n, large)


def _bias_tables(rel_table, s, tq, tk):
    ratio = tk // tq
    offsets = jnp.arange(-(s - 1), s, dtype=jnp.int32)
    boff = rel_table[_t5_bucket(offsets)].T.astype(F32) * LOG2E
    span = tk + tq - 1
    pad = tk + tq
    boff_p = jnp.pad(boff, ((0, 0), (pad, pad)), mode="edge")
    tiles = []
    for dc in (-1, 0, 1):
        for r in range(ratio):
            first = dc * tk - r * tq - (tq - 1) + (s - 1) + pad
            u = jnp.flip(boff_p[:, first:first + span], axis=1)
            hankel = jnp.tile(u, (1, tk + 1))[:, :tk * (span + 1)].reshape(A_HEADS, tk, span + 1)[:, :, :tq]
            tiles.append(jnp.flip(hankel, axis=1))
    assert tk >= MAX_DISTANCE and s - 1 >= MAX_DISTANCE
    for col in (0, 2 * s - 2):
        tiles.append(jnp.broadcast_to(boff[:, col][:, None, None], (A_HEADS, tk, tq)))
    return jnp.stack(tiles, axis=1)


def _rope_tables_t(s):
    rows = s // GRID_W
    row = jnp.repeat(jnp.arange(rows), GRID_W).astype(F32)
    col = jnp.tile(jnp.arange(GRID_W), rows).astype(F32)
    axis_dim = B_HEAD_DIM // 2
    inv_freq = ROPE_THETA ** (-jnp.arange(0, axis_dim, 2, dtype=F32) / axis_dim)
    ang_r = row[:, None] * inv_freq[None, :]
    ang_c = col[:, None] * inv_freq[None, :]
    ang = jnp.concatenate([ang_r, ang_r, ang_c, ang_c], axis=-1)
    return jnp.cos(ang).T, jnp.sin(ang).T


def _tile_sizes(s):
    tm = 512
    tk_a = 512 if s >= 8192 else 256
    return tm, (256, tk_a), (128, 256)


def _trunk(x, mod, lam, rel_table, norm_g, wn, wt, subln_g, q_norm_g, k_norm_g, w_out, final_g):
    b, s, _ = x.shape
    tm, (tq_a, tk_a), (tq_b, tk_b) = _tile_sizes(s)
    cos_t, sin_t = _rope_tables_t(s)
    bias_tiles = _bias_tables(rel_table, s, tq_a, tk_a)
    fg = final_g.reshape(1, D_MODEL)
    for l in range(DEPTH):
        lam_init = 0.8 - 0.6 * math.exp(-0.3 * l)
        qg_b = jnp.broadcast_to(q_norm_g[l][:, None], (B_HEAD_DIM, tm))
        kg_b = jnp.broadcast_to(k_norm_g[l][:, None], (B_HEAD_DIM, tm))
        sg_b = jnp.broadcast_to(subln_g[l][:, None], (A_V_DIM, tq_a))
        ka, gate, qat, vat, qbt, kb, vbt = _inproj(
            x, mod[l], norm_g[l].reshape(1, D_MODEL), wn[l], wt[l], cos_t, sin_t, qg_b, kg_b, tm)
        oa = _attn_a(lam[l:l + 1], qat, ka, vat, bias_tiles, sg_b, tq_a, tk_a, 1.0 - lam_init)
        ob = _attn_b(qbt, kb, vbt, tq_b, tk_b)
        x = _outproj(oa, ob, gate, x, mod[l], w_out[l], fg, tm, final=(l == DEPTH - 1))
    return x


def kernel(x_prompt, x_sample, c_prompt, c_sample, rel_table, norm_g, w_ada, b_ada, w_in, lam_q1, lam_k1, lam_q2,
           lam_k2, subln_g, q_norm_g, k_norm_g, w_out, final_g):
    bp = x_prompt.shape[0]
    bs = x_sample.shape[0]
    rows = -(-(bp + bs) // 8) * 8
    c_all = jnp.concatenate([c_prompt, c_sample, jnp.zeros((rows - bp - bs, D_MODEL), F32)], axis=0)
    mod_all = _adaln_mod(c_all, w_ada.astype(BF16), b_ada)
    mod_p = mod_all[:, :bp].reshape(DEPTH, bp, 3, D_MODEL)
    mod_s = mod_all[:, bp:bp + bs].reshape(DEPTH, bs, 3, D_MODEL)
    lam = _lambdas(lam_q1, lam_k1, lam_q2, lam_k2)

    w_bf = w_in.astype(BF16)
    wn = jnp.concatenate([w_bf[:, :, _C_KA:_C_VA], w_bf[:, :, _C_GA:_C_QB], w_bf[:, :, _C_GB:D_IN]], axis=2)
    wt = jnp.concatenate([w_bf[:, :, _C_QA:_C_KA], w_bf[:, :, _C_VA:_C_GA], w_bf[:, :, _C_QB:_C_GB]], axis=2)
    wt = jnp.swapaxes(wt, 1, 2)
    w_out_bf = w_out.astype(BF16)

    args = (rel_table, norm_g, wn, wt, subln_g, q_norm_g, k_norm_g, w_out_bf, final_g)
    y_prompt = _trunk(x_prompt, mod_p, lam, *args)
    y_sample = _trunk(x_sample, mod_s, lam, *args)
    return (y_prompt, y_sample)
```

```python
import functools
import math

import jax
import jax.numpy as jnp
from jax import lax
from jax.experimental import pallas as pl
from jax.experimental.pallas import tpu as pltpu

F32 = jnp.float32
BF16 = jnp.bfloat16

D_MODEL = 1024
DEPTH = 4
GRID_W = 64
A_WIDTH = 512
A_V_DIM = 128
A_HEAD_DIM = 64
A_HEADS = 4
B_WIDTH = 512
B_HEAD_DIM = 64
B_HEADS = 8
B_KV_HEADS = 2
B_GROUP = 4
NUM_BUCKETS = 32
MAX_DISTANCE = 128
ROPE_THETA = 10000.0
EPS = 1e-6

LOG2E = 1.4426950408889634
Q_SCALE = (A_HEAD_DIM ** -0.5) * LOG2E
M_INIT = -1e30
ONES_ROWS = 16
VMEM_LIMIT_BYTES = 56 * 1024 * 1024

_C_QA, _C_KA, _C_VA, _C_GA, _C_QB, _C_KB, _C_VB, _C_GB = 0, 512, 1024, 1536, 2048, 2560, 2688, 2816
D_IN = 3328
N_NAT = 1536
N_TR = 1792


def _rms_rows(x, g):
    ms = jnp.mean(x * x, axis=-1, keepdims=True)
    return x * lax.rsqrt(ms + EPS) * g


def _mod_kernel(c_ref, w_ref, b_ref, o_ref):
    c = c_ref[...]
    c_act = (c * jax.nn.sigmoid(c)).astype(BF16)
    o_ref[0] = jnp.dot(c_act, w_ref[0], preferred_element_type=F32) + b_ref[0]


def _adaln_mod(c_all, w_ada_bf, b_ada):
    rows = c_all.shape[0]
    return pl.pallas_call(
        _mod_kernel,
        grid=(DEPTH, 3),
        in_specs=[
            pl.BlockSpec((rows, D_MODEL), lambda l, j: (0, 0)),
            pl.BlockSpec((1, D_MODEL, D_MODEL), lambda l, j: (l, 0, j)),
            pl.BlockSpec((1, 1, D_MODEL), lambda l, j: (l, 0, j)),
        ],
        out_specs=pl.BlockSpec((1, rows, D_MODEL), lambda l, j: (l, 0, j)),
        out_shape=jax.ShapeDtypeStruct((DEPTH, rows, 3 * D_MODEL), F32),
        name="adaln_mod",
    )(c_all, w_ada_bf, b_ada.reshape(DEPTH, 1, 3 * D_MODEL))


def _lam_kernel(q1_ref, k1_ref, q2_ref, k2_ref, init_ref, o_ref):
    s1 = jnp.sum(q1_ref[...] * k1_ref[...], axis=-1, keepdims=True)
    s2 = jnp.sum(q2_ref[...] * k2_ref[...], axis=-1, keepdims=True)
    lam = jnp.exp(s1) - jnp.exp(s2) + init_ref[...][:, 0:1]
    o_ref[...] = jnp.broadcast_to(lam, o_ref.shape)


def _lambdas(lam_q1, lam_k1, lam_q2, lam_k2):
    init = jnp.asarray([[0.8 - 0.6 * math.exp(-0.3 * l)] * 128 for l in range(DEPTH)], F32)
    out = pl.pallas_call(
        _lam_kernel,
        out_shape=jax.ShapeDtypeStruct((DEPTH, 128), F32),
        name="diff_lambda",
    )(lam_q1, lam_k1, lam_q2, lam_k2, init)
    return out[:, 0]


def _rope_t(x, cos, sin):
    rot = jnp.concatenate([-x[16:32], x[0:16], -x[48:64], x[32:48]], axis=0)
    return x * cos + rot * sin


def _norm_t(x, g):
    ms = jnp.mean(x * x, axis=0, keepdims=True)
    return x * lax.rsqrt(ms + EPS) * g


def _inproj_kernel(x_ref, mod_ref, ng_ref, wn_ref, wt_ref, cos_ref, sin_ref, qg_ref, kg_ref,
                   ka_ref, gate_ref, qat_ref, vat_ref, qbt_ref, kb_ref, vbt_ref):
    tm = x_ref.shape[1]
    x = x_ref[0]
    shift = mod_ref[0, 0:1, :]
    scale = mod_ref[0, 1:2, :]
    h = (_rms_rows(x, ng_ref[...]) * (1.0 + scale) + shift).astype(BF16)

    pn = jnp.dot(h, wn_ref[...], preferred_element_type=F32)
    ka_ref[0] = pn[:, 0:512].astype(BF16)
    g = pn[:, 512:N_NAT]
    gate_ref[0] = g * jax.nn.sigmoid(g)

    pt = lax.dot_general(wt_ref[...], h, (((1,), (1,)), ((), ())), preferred_element_type=F32)
    qat_ref[0] = (pt[0:512] * Q_SCALE).astype(BF16)
    ones = jnp.ones((ONES_ROWS, tm), BF16)
    for hh in range(A_HEADS):
        vat_ref[0, hh, 0:A_V_DIM, :] = pt[512 + hh * A_V_DIM:512 + (hh + 1) * A_V_DIM].astype(BF16)
        vat_ref[0, hh, A_V_DIM:A_V_DIM + ONES_ROWS, :] = ones

    cos = cos_ref[...]
    sin = sin_ref[...]
    qg = qg_ref[...]
    for hh in range(B_HEADS):
        xh = pt[1024 + hh * 64:1024 + (hh + 1) * 64]
        qbt_ref[0, hh * 64:(hh + 1) * 64, :] = (_rope_t(_norm_t(xh, qg), cos, sin) * Q_SCALE).astype(BF16)
    kg = kg_ref[...]
    kparts = []
    for n in range(B_KV_HEADS):
        xh = pt[1536 + n * 64:1536 + (n + 1) * 64]
        kparts.append(_rope_t(_norm_t(xh, kg), cos, sin))
    kb_ref[0] = jnp.concatenate(kparts, axis=0).T.astype(BF16)
    for n in range(B_KV_HEADS):
        vbt_ref[0, n, 0:64, :] = pt[1664 + n * 64:1664 + (n + 1) * 64].astype(BF16)
        vbt_ref[0, n, 64:64 + ONES_ROWS, :] = ones


def _inproj(x, mod, ng, wn, wt, cos_t, sin_t, qg_b, kg_b, tm):
    b, s, _ = x.shape
    grid = (b, s // tm)
    const2 = lambda i, j: (0, 0)
    return pl.pallas_call(
        _inproj_kernel,
        grid=grid,
        in_specs=[
            pl.BlockSpec((1, tm, D_MODEL), lambda i, j: (i, j, 0)),
            pl.BlockSpec((1, 3, D_MODEL), lambda i, j: (i, 0, 0)),
            pl.BlockSpec((1, D_MODEL), const2),
            pl.BlockSpec((D_MODEL, N_NAT), const2),
            pl.BlockSpec((N_TR, D_MODEL), const2),
            pl.BlockSpec((B_HEAD_DIM, tm), lambda i, j: (0, j)),
            pl.BlockSpec((B_HEAD_DIM, tm), lambda i, j: (0, j)),
            pl.BlockSpec((B_HEAD_DIM, tm), const2),
            pl.BlockSpec((B_HEAD_DIM, tm), const2),
        ],
        out_specs=[
            pl.BlockSpec((1, tm, A_WIDTH), lambda i, j: (i, j, 0)),
            pl.BlockSpec((1, tm, D_MODEL), lambda i, j: (i, j, 0)),
            pl.BlockSpec((1, A_WIDTH, tm), lambda i, j: (i, 0, j)),
            pl.BlockSpec((1, A_HEADS, A_V_DIM + ONES_ROWS, tm), lambda i, j: (i, 0, 0, j)),
            pl.BlockSpec((1, B_WIDTH, tm), lambda i, j: (i, 0, j)),
            pl.BlockSpec((1, tm, B_KV_HEADS * B_HEAD_DIM), lambda i, j: (i, j, 0)),
            pl.BlockSpec((1, B_KV_HEADS, B_HEAD_DIM + ONES_ROWS, tm), lambda i, j: (i, 0, 0, j)),
        ],
        out_shape=[
            jax.ShapeDtypeStruct((b, s, A_WIDTH), BF16),
            jax.ShapeDtypeStruct((b, s, D_MODEL), F32),
            jax.ShapeDtypeStruct((b, A_WIDTH, s), BF16),
            jax.ShapeDtypeStruct((b, A_HEADS, A_V_DIM + ONES_ROWS, s), BF16),
            jax.ShapeDtypeStruct((b, B_WIDTH, s), BF16),
            jax.ShapeDtypeStruct((b, s, B_KV_HEADS * B_HEAD_DIM), BF16),
            jax.ShapeDtypeStruct((b, B_KV_HEADS, B_HEAD_DIM + ONES_ROWS, s), BF16),
        ],
        compiler_params=pltpu.CompilerParams(
            dimension_semantics=("arbitrary", "arbitrary"), vmem_limit_bytes=VMEM_LIMIT_BYTES),
        name="inproj",
    )(x, mod, ng, wn, wt, cos_t, sin_t, qg_b, kg_b)


def _attn_a_kernel(scal_ref, qt_ref, k_ref, vt_ref, bias_ref, sg_ref, o_ref, wq_ref, acc_ref,
                   s0_ref, s1_ref, *, tq, tk, n_chunks, ratio, unroll, out_scale):
    s_bufs = (s0_ref, s1_ref)
    qi = pl.program_id(2)
    q = qt_ref[0]
    zero = jnp.zeros((A_HEAD_DIM, tq), BF16)
    wq_ref[0:64, 0:tq] = q[0:64]
    wq_ref[0:64, tq:2 * tq] = zero
    wq_ref[64:128, 0:tq] = zero
    wq_ref[64:128, tq:2 * tq] = q[64:128]
    acc_ref[...] = jnp.zeros(acc_ref.shape, F32)

    lam = scal_ref[0]
    log_ratio = ratio.bit_length() - 1
    c_own = lax.shift_right_logical(qi, log_ratio)
    r = lax.bitwise_and(qi, ratio - 1)

    def qk(c, s_ref):
        start = pl.multiple_of(c * tk, tk)
        d = c - c_own
        tile = jnp.where(d < -1, 3 * ratio, jnp.where(d > 1, 3 * ratio + 1, (d + 1) * ratio + r))
        t = bias_ref[0, tile]
        s = jnp.dot(k_ref[0, pl.ds(start, tk), :], wq_ref[...], preferred_element_type=F32)
        s = s + jnp.concatenate([t, t], axis=1)
        s_ref[...] = s
        return jnp.max(s, axis=0, keepdims=True)

    def softmax_pv(s_ref, mc, m, c):
        mn = jnp.maximum(m, mc)
        alpha = jnp.exp2(m - mn)
        p = jnp.exp2(s_ref[...] - mn).astype(BF16)
        start = pl.multiple_of(c * tk, tk)
        pv = jnp.dot(vt_ref[0, 0, :, pl.ds(start, tk)], p, preferred_element_type=F32)
        acc_ref[...] = acc_ref[...] * alpha + pv
        return mn

    def body(i, carry):
        m, mc = carry
        c0 = unroll * i
        for u in range(unroll):
            mc_next = qk(jnp.minimum(c0 + u + 1, n_chunks - 1), s_bufs[(u + 1) % 2])
            m = softmax_pv(s_bufs[u % 2], mc, m, c0 + u)
            mc = mc_next
        return m, mc

    mc_first = qk(0, s_bufs[0])
    lax.fori_loop(0, n_chunks // unroll, body, (jnp.full((1, 2 * tq), M_INIT, F32), mc_first))

    acc = acc_ref[...]
    o = acc[0:A_V_DIM, :] / acc[A_V_DIM:A_V_DIM + 1, :]
    w = o[:, 0:tq] - lam * o[:, tq:2 * tq]
    y = _norm_t(w, sg_ref[...]) * out_scale
    o_ref[0] = y.T


def _attn_a(scal, qat, ka, vat, bias_tiles, sg_b, tq, tk, out_scale):
    b, s, _ = ka.shape
    n_chunks = s // tk
    ratio = tk // tq
    unroll = 8 if n_chunks % 8 == 0 else 2
    assert n_chunks % unroll == 0 and ratio * tq == tk and ratio & (ratio - 1) == 0
    kern = functools.partial(_attn_a_kernel, tq=tq, tk=tk, n_chunks=n_chunks, ratio=ratio, unroll=unroll,
                             out_scale=out_scale)
    return pl.pallas_call(
        kern,
        grid=(b, A_HEADS, s // tq),
        in_specs=[
            pl.BlockSpec(memory_space=pltpu.SMEM),
            pl.BlockSpec((1, 2 * A_HEAD_DIM, tq), lambda i, hh, j: (i, hh, j)),
            pl.BlockSpec((1, s, 2 * A_HEAD_DIM), lambda i, hh, j: (i, 0, hh)),
            pl.BlockSpec((1, 1, A_V_DIM + ONES_ROWS, s), lambda i, hh, j: (i, hh, 0, 0)),
            pl.BlockSpec((1, 3 * ratio + 2, tk, tq), lambda i, hh, j: (hh, 0, 0, 0)),
            pl.BlockSpec((A_V_DIM, tq), lambda i, hh, j: (0, 0)),
        ],
        out_specs=pl.BlockSpec((1, tq, A_V_DIM), lambda i, hh, j: (i, j, hh)),
        out_shape=jax.ShapeDtypeStruct((b, s, A_WIDTH), F32),
        scratch_shapes=[
            pltpu.VMEM((2 * A_HEAD_DIM, 2 * tq), BF16),
            pltpu.VMEM((A_V_DIM + ONES_ROWS, 2 * tq), F32),
        ] + [pltpu.VMEM((tk, 2 * tq), F32)] * 2,
        compiler_params=pltpu.CompilerParams(
            dimension_semantics=("arbitrary", "arbitrary", "arbitrary"), vmem_limit_bytes=VMEM_LIMIT_BYTES),
        name="attn_a",
    )(scal, qat, ka, vat, bias_tiles, sg_b)


def _attn_b_kernel(qt_ref, k_ref, vt_ref, o_ref, wq_ref, acc0_ref, acc1_ref, s0_ref, s1_ref, p0_ref, p1_ref,
                   *, tq, tk, n_chunks, unroll):
    half = B_GROUP * tq
    lanes = B_HEADS * tq
    wq_ref[...] = jnp.zeros(wq_ref.shape, BF16)
    for hh in range(B_HEADS):
        n = hh // B_GROUP
        wq_ref[n * 64:(n + 1) * 64, hh * tq:(hh + 1) * tq] = qt_ref[0, hh * 64:(hh + 1) * 64, :]
    acc0_ref[...] = jnp.zeros(acc0_ref.shape, F32)
    acc1_ref[...] = jnp.zeros(acc1_ref.shape, F32)
    s_bufs = (s0_ref, s1_ref)
    p_bufs = (p0_ref, p1_ref)

    def qk(c, s_ref):
        start = pl.multiple_of(c * tk, tk)
        s = jnp.dot(k_ref[0, pl.ds(start, tk), :], wq_ref[...], preferred_element_type=F32)
        s_ref[...] = s
        return jnp.max(s, axis=0, keepdims=True)

    def probs(s_ref, p_ref, mc, m):
        mn = jnp.maximum(m, mc)
        p_ref[...] = jnp.exp2(s_ref[...] - mn).astype(BF16)
        return mn, jnp.exp2(m - mn)

    def pv(c, p_ref, alpha):
        start = pl.multiple_of(c * tk, tk)
        p = p_ref[...]
        v0 = vt_ref[0, 0, :, pl.ds(start, tk)]
        v1 = vt_ref[0, 1, :, pl.ds(start, tk)]
        acc0_ref[...] = acc0_ref[...] * alpha[:, 0:half] + jnp.dot(v0, p[:, 0:half], preferred_element_type=F32)
        acc1_ref[...] = acc1_ref[...] * alpha[:, half:] + jnp.dot(v1, p[:, half:], preferred_element_type=F32)

    def body(i, carry):
        m, mc, alpha_prev = carry
        c0 = unroll * i
        for u in range(unroll):
            c = c0 + u
            pv(jnp.maximum(c - 1, 0), p_bufs[(u + 1) % 2], alpha_prev)
            mc_next = qk(jnp.minimum(c + 1, n_chunks - 1), s_bufs[(u + 1) % 2])
            m, alpha_prev = probs(s_bufs[u % 2], p_bufs[u % 2], mc, m)
            mc = mc_next
        return m, mc, alpha_prev

    p1_ref[...] = jnp.zeros(p1_ref.shape, BF16)
    mc_first = qk(0, s0_ref)
    init = (jnp.full((1, lanes), M_INIT, F32), mc_first, jnp.ones((1, lanes), F32))
    _, _, alpha_last = lax.fori_loop(0, n_chunks // unroll, body, init)
    pv(n_chunks - 1, p_bufs[(n_chunks - 1) % 2], alpha_last)

    parts = []
    for acc_ref in (acc0_ref, acc1_ref):
        acc = acc_ref[...]
        o = acc[0:64, :] / acc[64:65, :]
        for g in range(B_GROUP):
            parts.append(o[:, g * tq:(g + 1) * tq])
    o_ref[0] = jnp.concatenate(parts, axis=0).T


def _attn_b(qbt, kb, vbt, tq, tk):
    b, s, _ = kb.shape
    n_chunks = s // tk
    unroll = 8 if n_chunks % 8 == 0 else 2
    assert n_chunks % unroll == 0
    kern = functools.partial(_attn_b_kernel, tq=tq, tk=tk, n_chunks=n_chunks, unroll=unroll)
    return pl.pallas_call(
        kern,
        grid=(b, s // tq),
        in_specs=[
            pl.BlockSpec((1, B_WIDTH, tq), lambda i, j: (i, 0, j)),
            pl.BlockSpec((1, s, B_KV_HEADS * B_HEAD_DIM), lambda i, j: (i, 0, 0)),
            pl.BlockSpec((1, B_KV_HEADS, B_HEAD_DIM + ONES_ROWS, s), lambda i, j: (i, 0, 0, 0)),
        ],
        out_specs=pl.BlockSpec((1, tq, B_WIDTH), lambda i, j: (i, j, 0)),
        out_shape=jax.ShapeDtypeStruct((b, s, B_WIDTH), F32),
        scratch_shapes=[
            pltpu.VMEM((B_KV_HEADS * B_HEAD_DIM, B_HEADS * tq), BF16),
            pltpu.VMEM((B_HEAD_DIM + ONES_ROWS, B_GROUP * tq), F32),
            pltpu.VMEM((B_HEAD_DIM + ONES_ROWS, B_GROUP * tq), F32),
        ] + [pltpu.VMEM((tk, B_HEADS * tq), F32)] * 2 + [pltpu.VMEM((tk, B_HEADS * tq), BF16)] * 2,
        compiler_params=pltpu.CompilerParams(
            dimension_semantics=("arbitrary", "arbitrary"), vmem_limit_bytes=VMEM_LIMIT_BYTES),
        name="attn_b",
    )(qbt, kb, vbt)


def _outproj_kernel(oa_ref, ob_ref, gate_ref, x_ref, mod_ref, w_ref, fg_ref, o_ref, *, final):
    ma = (oa_ref[0] * gate_ref[0, :, 0:A_WIDTH]).astype(BF16)
    mb = (ob_ref[0] * gate_ref[0, :, A_WIDTH:D_MODEL]).astype(BF16)
    y = (jnp.dot(ma, w_ref[0:A_WIDTH, :], preferred_element_type=F32)
         + jnp.dot(mb, w_ref[A_WIDTH:D_MODEL, :], preferred_element_type=F32))
    xn = x_ref[0] + mod_ref[0, 2:3, :] * y
    if final:
        xn = _rms_rows(xn, fg_ref[...])
    o_ref[0] = xn


def _outproj(oa, ob, gate, x, mod, w, fg, tm, final):
    b, s, _ = x.shape
    tok = lambda i, j: (i, j, 0)
    return pl.pallas_call(
        functools.partial(_outproj_kernel, final=final),
        grid=(b, s // tm),
        in_specs=[
            pl.BlockSpec((1, tm, A_WIDTH), tok),
            pl.BlockSpec((1, tm, B_WIDTH), tok),
            pl.BlockSpec((1, tm, D_MODEL), tok),
            pl.BlockSpec((1, tm, D_MODEL), tok),
            pl.BlockSpec((1, 3, D_MODEL), lambda i, j: (i, 0, 0)),
            pl.BlockSpec((D_MODEL, D_MODEL), lambda i, j: (0, 0)),
            pl.BlockSpec((1, D_MODEL), lambda i, j: (0, 0)),
        ],
        out_specs=pl.BlockSpec((1, tm, D_MODEL), tok),
        out_shape=jax.ShapeDtypeStruct((b, s, D_MODEL), F32),
        compiler_params=pltpu.CompilerParams(
            dimension_semantics=("arbitrary", "arbitrary"), vmem_limit_bytes=VMEM_LIMIT_BYTES),
        name="outproj",
    )(oa, ob, gate, x, mod, w, fg)


def _t5_bucket(rel):
    half = NUM_BUCKETS // 2
    max_exact = half // 2
    ret = jnp.where(rel > 0, half, 0)
    n = jnp.abs(rel)
    nf = jnp.maximum(n, 1).astype(jnp.float32)
    large = max_exact + (jnp.log(nf / max_exact) / math.log(MAX_DISTANCE / max_exact)
                         * (half - max_exact)).astype(jnp.int32)
    large = jnp.minimum(large, half - 1)
    return ret + jnp.where(n < max_exact, n, large)


def _bias_tables(rel_table, s, tq, tk):
    ratio = tk // tq
    offsets = jnp.arange(-(s - 1), s, dtype=jnp.int32)
    boff = rel_table[_t5_bucket(offsets)].T.astype(F32) * LOG2E
    span = tk + tq - 1
    pad = tk + tq
    boff_p = jnp.pad(boff, ((0, 0), (pad, pad)), mode="edge")
    tiles = []
    for dc in (-1, 0, 1):
        for r in range(ratio):
            first = dc * tk - r * tq - (tq - 1) + (s - 1) + pad
            u = jnp.flip(boff_p[:, first:first + span], axis=1)
            hankel = jnp.tile(u, (1, tk + 1))[:, :tk * (span + 1)].reshape(A_HEADS, tk, span + 1)[:, :, :tq]
            tiles.append(jnp.flip(hankel, axis=1))
    assert tk >= MAX_DISTANCE and s - 1 >= MAX_DISTANCE
    for col in (0, 2 * s - 2):
        tiles.append(jnp.broadcast_to(boff[:, col][:, None, None], (A_HEADS, tk, tq)))
    return jnp.stack(tiles, axis=1)


def _rope_tables_t(s):
    rows = s // GRID_W
    row = jnp.repeat(jnp.arange(rows), GRID_W).astype(F32)
    col = jnp.tile(jnp.arange(GRID_W), rows).astype(F32)
    axis_dim = B_HEAD_DIM // 2
    inv_freq = ROPE_THETA ** (-jnp.arange(0, axis_dim, 2, dtype=F32) / axis_dim)
    ang_r = row[:, None] * inv_freq[None, :]
    ang_c = col[:, None] * inv_freq[None, :]
    ang = jnp.concatenate([ang_r, ang_r, ang_c, ang_c], axis=-1)
    return jnp.cos(ang).T, jnp.sin(ang).T


def _tile_sizes(s):
    tm = 512
    tk_a = 512 if s >= 8192 else 256
    return tm, (256, tk_a), (128, 256)


def _trunk(x, mod, lam, rel_table, norm_g, wn, wt, subln_g, q_norm_g, k_norm_g, w_out, final_g):
    b, s, _ = x.shape
    tm, (tq_a, tk_a), (tq_b, tk_b) = _tile_sizes(s)
    cos_t, sin_t = _rope_tables_t(s)
    bias_tiles = _bias_tables(rel_table, s, tq_a, tk_a)
    fg = final_g.reshape(1, D_MODEL)
    for l in range(DEPTH):
        lam_init = 0.8 - 0.6 * math.exp(-0.3 * l)
        qg_b = jnp.broadcast_to(q_norm_g[l][:, None], (B_HEAD_DIM, tm))
        kg_b = jnp.broadcast_to(k_norm_g[l][:, None], (B_HEAD_DIM, tm))
        sg_b = jnp.broadcast_to(subln_g[l][:, None], (A_V_DIM, tq_a))
        ka, gate, qat, vat, qbt, kb, vbt = _inproj(
            x, mod[l], norm_g[l].reshape(1, D_MODEL), wn[l], wt[l], cos_t, sin_t, qg_b, kg_b, tm)
        oa = _attn_a(lam[l:l + 1], qat, ka, vat, bias_tiles, sg_b, tq_a, tk_a, 1.0 - lam_init)
        ob = _attn_b(qbt, kb, vbt, tq_b, tk_b)
        x = _outproj(oa, ob, gate, x, mod[l], w_out[l], fg, tm, final=(l == DEPTH - 1))
    return x


def kernel(x_prompt, x_sample, c_prompt, c_sample, rel_table, norm_g, w_ada, b_ada, w_in, lam_q1, lam_k1, lam_q2,
           lam_k2, subln_g, q_norm_g, k_norm_g, w_out, final_g):
    bp = x_prompt.shape[0]
    bs = x_sample.shape[0]
    rows = -(-(bp + bs) // 8) * 8
    c_all = jnp.concatenate([c_prompt, c_sample, jnp.zeros((rows - bp - bs, D_MODEL), F32)], axis=0)
    mod_all = _adaln_mod(c_all, w_ada.astype(BF16), b_ada)
    mod_p = mod_all[:, :bp].reshape(DEPTH, bp, 3, D_MODEL)
    mod_s = mod_all[:, bp:bp + bs].reshape(DEPTH, bs, 3, D_MODEL)
    lam = _lambdas(lam_q1, lam_k1, lam_q2, lam_k2)

    w_bf = w_in.astype(BF16)
    wn = jnp.concatenate([w_bf[:, :, _C_KA:_C_VA], w_bf[:, :, _C_GA:_C_QB], w_bf[:, :, _C_GB:D_IN]], axis=2)
    wt = jnp.concatenate([w_bf[:, :, _C_QA:_C_KA], w_bf[:, :, _C_VA:_C_GA], w_bf[:, :, _C_QB:_C_GB]], axis=2)
    wt = jnp.swapaxes(wt, 1, 2)
    w_out_bf = w_out.astype(BF16)

    args = (rel_table, norm_g, wn, wt, subln_g, q_norm_g, k_norm_g, w_out_bf, final_g)
    y_prompt = _trunk(x_prompt, mod_p, lam, *args)
    y_sample = _trunk(x_sample, mod_s, lam, *args)
    return (y_prompt, y_sample)
```

```python
import functools
import math

import jax
import jax.numpy as jnp
from jax import lax
from jax.experimental import pallas as pl
from jax.experimental.pallas import tpu as pltpu

F32 = jnp.float32
BF16 = jnp.bfloat16

D_MODEL = 1024
DEPTH = 4
GRID_W = 64
A_WIDTH = 512
A_V_DIM = 128
A_HEAD_DIM = 64
A_HEADS = 4
B_WIDTH = 512
B_HEAD_DIM = 64
B_HEADS = 8
B_KV_HEADS = 2
B_GROUP = 4
NUM_BUCKETS = 32
MAX_DISTANCE = 128
ROPE_THETA = 10000.0
EPS = 1e-6

LOG2E = 1.4426950408889634
Q_SCALE = (A_HEAD_DIM ** -0.5) * LOG2E
M_INIT = -1e30
ONES_ROWS = 16
VMEM_LIMIT_BYTES = 56 * 1024 * 1024
UNITS_PER_STEP = 4

_C_QA, _C_KA, _C_VA, _C_GA, _C_QB, _C_KB, _C_VB, _C_GB = 0, 512, 1024, 1536, 2048, 2560, 2688, 2816
D_IN = 3328
N_NAT = 1536
N_TR = 1792


def _rms_rows(x, g):
    ms = jnp.mean(x * x, axis=-1, keepdims=True)
    return x * lax.rsqrt(ms + EPS) * g


def _mod_kernel(c_ref, w_ref, b_ref, o_ref):
    c = c_ref[...]
    c_act = (c * jax.nn.sigmoid(c)).astype(BF16)
    o_ref[0] = jnp.dot(c_act, w_ref[0], preferred_element_type=F32) + b_ref[0]


def _adaln_mod(c_all, w_ada_bf, b_ada):
    rows = c_all.shape[0]
    return pl.pallas_call(
        _mod_kernel,
        grid=(DEPTH, 3),
        in_specs=[
            pl.BlockSpec((rows, D_MODEL), lambda l, j: (0, 0)),
            pl.BlockSpec((1, D_MODEL, D_MODEL), lambda l, j: (l, 0, j)),
            pl.BlockSpec((1, 1, D_MODEL), lambda l, j: (l, 0, j)),
        ],
        out_specs=pl.BlockSpec((1, rows, D_MODEL), lambda l, j: (l, 0, j)),
        out_shape=jax.ShapeDtypeStruct((DEPTH, rows, 3 * D_MODEL), F32),
        name="adaln_mod",
    )(c_all, w_ada_bf, b_ada.reshape(DEPTH, 1, 3 * D_MODEL))


def _lam_kernel(q1_ref, k1_ref, q2_ref, k2_ref, init_ref, o_ref):
    s1 = jnp.sum(q1_ref[...] * k1_ref[...], axis=-1, keepdims=True)
    s2 = jnp.sum(q2_ref[...] * k2_ref[...], axis=-1, keepdims=True)
    lam = jnp.exp(s1) - jnp.exp(s2) + init_ref[...][:, 0:1]
    o_ref[...] = jnp.broadcast_to(lam, o_ref.shape)


def _lambdas(lam_q1, lam_k1, lam_q2, lam_k2):
    init = jnp.asarray([[0.8 - 0.6 * math.exp(-0.3 * l)] * 128 for l in range(DEPTH)], F32)
    out = pl.pallas_call(
        _lam_kernel,
        out_shape=jax.ShapeDtypeStruct((DEPTH, 128), F32),
        name="diff_lambda",
    )(lam_q1, lam_k1, lam_q2, lam_k2, init)
    return out[:, 0]


def _rope_t(x, cos, sin):
    rot = jnp.concatenate([-x[16:32], x[0:16], -x[48:64], x[32:48]], axis=0)
    return x * cos + rot * sin


def _norm_t(x, g):
    ms = jnp.mean(x * x, axis=0, keepdims=True)
    return x * lax.rsqrt(ms + EPS) * g


def _inproj_kernel(x_ref, mod_ref, ng_ref, wn_ref, wt_ref, cos_ref, sin_ref, qg_ref, kg_ref,
                   ka_ref, gate_ref, qat_ref, vat_ref, qbt_ref, kb_ref, vbt_ref):
    tm = x_ref.shape[1]
    x = x_ref[0]
    shift = mod_ref[0, 0:1, :]
    scale = mod_ref[0, 1:2, :]
    h = (_rms_rows(x, ng_ref[...]) * (1.0 + scale) + shift).astype(BF16)

    pn = jnp.dot(h, wn_ref[...], preferred_element_type=F32)
    ka_ref[0] = pn[:, 0:512].astype(BF16)
    g = pn[:, 512:N_NAT]
    gate_ref[0] = g * jax.nn.sigmoid(g)

    pt = lax.dot_general(wt_ref[...], h, (((1,), (1,)), ((), ())), preferred_element_type=F32)
    qat_ref[0] = (pt[0:512] * Q_SCALE).astype(BF16)
    ones = jnp.ones((ONES_ROWS, tm), BF16)
    for hh in range(A_HEADS):
        vat_ref[0, hh, 0:A_V_DIM, :] = pt[512 + hh * A_V_DIM:512 + (hh + 1) * A_V_DIM].astype(BF16)
        vat_ref[0, hh, A_V_DIM:A_V_DIM + ONES_ROWS, :] = ones

    cos = cos_ref[...]
    sin = sin_ref[...]
    qg = qg_ref[...]
    for hh in range(B_HEADS):
        xh = pt[1024 + hh * 64:1024 + (hh + 1) * 64]
        qbt_ref[0, hh * 64:(hh + 1) * 64, :] = (_rope_t(_norm_t(xh, qg), cos, sin) * Q_SCALE).astype(BF16)
    kg = kg_ref[...]
    kparts = []
    for n in range(B_KV_HEADS):
        xh = pt[1536 + n * 64:1536 + (n + 1) * 64]
        kparts.append(_rope_t(_norm_t(xh, kg), cos, sin))
    kb_ref[0] = jnp.concatenate(kparts, axis=0).T.astype(BF16)
    for n in range(B_KV_HEADS):
        vbt_ref[0, n, 0:64, :] = pt[1664 + n * 64:1664 + (n + 1) * 64].astype(BF16)
        vbt_ref[0, n, 64:64 + ONES_ROWS, :] = ones


def _inproj(x, mod, ng, wn, wt, cos_t, sin_t, qg_b, kg_b, tm):
    b, s, _ = x.shape
    grid = (b, s // tm)
    const2 = lambda i, j: (0, 0)
    return pl.pallas_call(
        _inproj_kernel,
        grid=grid,
        in_specs=[
            pl.BlockSpec((1, tm, D_MODEL), lambda i, j: (i, j, 0)),
            pl.BlockSpec((1, 3, D_MODEL), lambda i, j: (i, 0, 0)),
            pl.BlockSpec((1, D_MODEL), const2),
            pl.BlockSpec((D_MODEL, N_NAT), const2),
            pl.BlockSpec((N_TR, D_MODEL), const2),
            pl.BlockSpec((B_HEAD_DIM, tm), lambda i, j: (0, j)),
            pl.BlockSpec((B_HEAD_DIM, tm), lambda i, j: (0, j)),
            pl.BlockSpec((B_HEAD_DIM, tm), const2),
            pl.BlockSpec((B_HEAD_DIM, tm), const2),
        ],
        out_specs=[
            pl.BlockSpec((1, tm, A_WIDTH), lambda i, j: (i, j, 0)),
            pl.BlockSpec((1, tm, D_MODEL), lambda i, j: (i, j, 0)),
            pl.BlockSpec((1, A_WIDTH, tm), lambda i, j: (i, 0, j)),
            pl.BlockSpec((1, A_HEADS, A_V_DIM + ONES_ROWS, tm), lambda i, j: (i, 0, 0, j)),
            pl.BlockSpec((1, B_WIDTH, tm), lambda i, j: (i, 0, j)),
            pl.BlockSpec((1, tm, B_KV_HEADS * B_HEAD_DIM), lambda i, j: (i, j, 0)),
            pl.BlockSpec((1, B_KV_HEADS, B_HEAD_DIM + ONES_ROWS, tm), lambda i, j: (i, 0, 0, j)),
        ],
        out_shape=[
            jax.ShapeDtypeStruct((b, s, A_WIDTH), BF16),
            jax.ShapeDtypeStruct((b, s, D_MODEL), F32),
            jax.ShapeDtypeStruct((b, A_WIDTH, s), BF16),
            jax.ShapeDtypeStruct((b, A_HEADS, A_V_DIM + ONES_ROWS, s), BF16),
            jax.ShapeDtypeStruct((b, B_WIDTH, s), BF16),
            jax.ShapeDtypeStruct((b, s, B_KV_HEADS * B_HEAD_DIM), BF16),
            jax.ShapeDtypeStruct((b, B_KV_HEADS, B_HEAD_DIM + ONES_ROWS, s), BF16),
        ],
        compiler_params=pltpu.CompilerParams(
            dimension_semantics=("arbitrary", "arbitrary"), vmem_limit_bytes=VMEM_LIMIT_BYTES),
        name="inproj",
    )(x, mod, ng, wn, wt, cos_t, sin_t, qg_b, kg_b)


def _attn_a_kernel(scal_ref, qt_ref, k_ref, vt_ref, bias_ref, sg_ref, o_ref, *scratch,
                   tq, tk, n_chunks, ratio, unroll, units, out_scale):
    for unit in range(units):
        wq_ref, acc_ref, s0_ref, s1_ref = scratch[4 * unit:4 * unit + 4]
        _attn_a_unit(scal_ref, qt_ref, k_ref, vt_ref, bias_ref, sg_ref, o_ref, wq_ref, acc_ref, s0_ref, s1_ref,
                     unit, pl.program_id(2) * units + unit, tq=tq, tk=tk, n_chunks=n_chunks, ratio=ratio,
                     unroll=unroll, out_scale=out_scale)


def _attn_a_unit(scal_ref, qt_ref, k_ref, vt_ref, bias_ref, sg_ref, o_ref, wq_ref, acc_ref, s0_ref, s1_ref,
                 unit, qi, *, tq, tk, n_chunks, ratio, unroll, out_scale):
    s_bufs = (s0_ref, s1_ref)
    q = qt_ref[0, :, unit * tq:(unit + 1) * tq]
    zero = jnp.zeros((A_HEAD_DIM, tq), BF16)
    wq_ref[0:64, 0:tq] = q[0:64]
    wq_ref[0:64, tq:2 * tq] = zero
    wq_ref[64:128, 0:tq] = zero
    wq_ref[64:128, tq:2 * tq] = q[64:128]
    acc_ref[...] = jnp.zeros(acc_ref.shape, F32)

    lam = scal_ref[0]
    log_ratio = ratio.bit_length() - 1
    c_own = lax.shift_right_logical(qi, log_ratio)
    r = lax.bitwise_and(qi, ratio - 1)

    def qk(c, s_ref):
        start = pl.multiple_of(c * tk, tk)
        d = c - c_own
        tile = jnp.where(d < -1, 3 * ratio, jnp.where(d > 1, 3 * ratio + 1, (d + 1) * ratio + r))
        t = bias_ref[0, tile]
        s = jnp.dot(k_ref[0, pl.ds(start, tk), :], wq_ref[...], preferred_element_type=F32)
        s = s + jnp.concatenate([t, t], axis=1)
        s_ref[...] = s
        return jnp.max(s, axis=0, keepdims=True)

    def softmax_pv(s_ref, mc, m, c):
        mn = jnp.maximum(m, mc)
        alpha = jnp.exp2(m - mn)
        p = jnp.exp2(s_ref[...] - mn).astype(BF16)
        start = pl.multiple_of(c * tk, tk)
        pv = jnp.dot(vt_ref[0, 0, :, pl.ds(start, tk)], p, preferred_element_type=F32)
        acc_ref[...] = acc_ref[...] * alpha + pv
        return mn

    def body(i, carry):
        m, mc = carry
        c0 = unroll * i
        for u in range(unroll):
            mc_next = qk(jnp.minimum(c0 + u + 1, n_chunks - 1), s_bufs[(u + 1) % 2])
            m = softmax_pv(s_bufs[u % 2], mc, m, c0 + u)
            mc = mc_next
        return m, mc

    mc_first = qk(0, s_bufs[0])
    lax.fori_loop(0, n_chunks // unroll, body, (jnp.full((1, 2 * tq), M_INIT, F32), mc_first))

    acc = acc_ref[...]
    o = acc[0:A_V_DIM, :] / acc[A_V_DIM:A_V_DIM + 1, :]
    w = o[:, 0:tq] - lam * o[:, tq:2 * tq]
    y = _norm_t(w, sg_ref[...]) * out_scale
    o_ref[0, unit * tq:(unit + 1) * tq, :] = y.T


def _attn_a(scal, qat, ka, vat, bias_tiles, sg_b, tq, tk, out_scale):
    b, s, _ = ka.shape
    n_chunks = s // tk
    ratio = tk // tq
    unroll = next(u for u in (16, 8, 4, 2) if n_chunks % u == 0)
    units = UNITS_PER_STEP if n_chunks == unroll and (s // tq) % UNITS_PER_STEP == 0 else 1
    assert n_chunks % unroll == 0 and ratio * tq == tk and ratio & (ratio - 1) == 0
    kern = functools.partial(_attn_a_kernel, tq=tq, tk=tk, n_chunks=n_chunks, ratio=ratio, unroll=unroll,
                             units=units, out_scale=out_scale)
    return pl.pallas_call(
        kern,
        grid=(b, A_HEADS, s // (tq * units)),
        in_specs=[
            pl.BlockSpec(memory_space=pltpu.SMEM),
            pl.BlockSpec((1, 2 * A_HEAD_DIM, tq * units), lambda i, hh, j: (i, hh, j)),
            pl.BlockSpec((1, s, 2 * A_HEAD_DIM), lambda i, hh, j: (i, 0, hh)),
            pl.BlockSpec((1, 1, A_V_DIM + ONES_ROWS, s), lambda i, hh, j: (i, hh, 0, 0)),
            pl.BlockSpec((1, 3 * ratio + 2, tk, tq), lambda i, hh, j: (hh, 0, 0, 0)),
            pl.BlockSpec((A_V_DIM, tq), lambda i, hh, j: (0, 0)),
        ],
        out_specs=pl.BlockSpec((1, tq * units, A_V_DIM), lambda i, hh, j: (i, j, hh)),
        out_shape=jax.ShapeDtypeStruct((b, s, A_WIDTH), F32),
        scratch_shapes=[
            pltpu.VMEM((2 * A_HEAD_DIM, 2 * tq), BF16),
            pltpu.VMEM((A_V_DIM + ONES_ROWS, 2 * tq), F32),
            pltpu.VMEM((tk, 2 * tq), F32),
            pltpu.VMEM((tk, 2 * tq), F32),
        ] * units,
        compiler_params=pltpu.CompilerParams(
            dimension_semantics=("arbitrary", "arbitrary", "arbitrary"), vmem_limit_bytes=VMEM_LIMIT_BYTES),
        name="attn_a",
    )(scal, qat, ka, vat, bias_tiles, sg_b)


def _attn_b_kernel(qt_ref, k_ref, vt_ref, o_ref, *scratch, tq, tk, n_chunks, unroll, units):
    for unit in range(units):
        _attn_b_unit(qt_ref, k_ref, vt_ref, o_ref, *scratch[7 * unit:7 * unit + 7], unit,
                     tq=tq, tk=tk, n_chunks=n_chunks, unroll=unroll)


def _attn_b_unit(qt_ref, k_ref, vt_ref, o_ref, wq_ref, acc0_ref, acc1_ref, s0_ref, s1_ref, p0_ref, p1_ref, unit,
                 *, tq, tk, n_chunks, unroll):
    half = B_GROUP * tq
    lanes = B_HEADS * tq
    wq_ref[...] = jnp.zeros(wq_ref.shape, BF16)
    for hh in range(B_HEADS):
        n = hh // B_GROUP
        wq_ref[n * 64:(n + 1) * 64, hh * tq:(hh + 1) * tq] = qt_ref[0, hh * 64:(hh + 1) * 64,
                                                                    unit * tq:(unit + 1) * tq]
    acc0_ref[...] = jnp.zeros(acc0_ref.shape, F32)
    acc1_ref[...] = jnp.zeros(acc1_ref.shape, F32)
    s_bufs = (s0_ref, s1_ref)
    p_bufs = (p0_ref, p1_ref)

    def qk(c, s_ref):
        start = pl.multiple_of(c * tk, tk)
        s = jnp.dot(k_ref[0, pl.ds(start, tk), :], wq_ref[...], preferred_element_type=F32)
        s_ref[...] = s
        return jnp.max(s, axis=0, keepdims=True)

    def probs(s_ref, p_ref, mc, m):
        mn = jnp.maximum(m, mc)
        p_ref[...] = jnp.exp2(s_ref[...] - mn).astype(BF16)
        return mn, jnp.exp2(m - mn)

    def pv(c, p_ref, alpha):
        start = pl.multiple_of(c * tk, tk)
        p = p_ref[...]
        v0 = vt_ref[0, 0, :, pl.ds(start, tk)]
        v1 = vt_ref[0, 1, :, pl.ds(start, tk)]
        acc0_ref[...] = acc0_ref[...] * alpha[:, 0:half] + jnp.dot(v0, p[:, 0:half], preferred_element_type=F32)
        acc1_ref[...] = acc1_ref[...] * alpha[:, half:] + jnp.dot(v1, p[:, half:], preferred_element_type=F32)

    def body(i, carry):
        m, mc, alpha_prev = carry
        c0 = unroll * i
        for u in range(unroll):
            c = c0 + u
            pv(jnp.maximum(c - 1, 0), p_bufs[(u + 1) % 2], alpha_prev)
            mc_next = qk(jnp.minimum(c + 1, n_chunks - 1), s_bufs[(u + 1) % 2])
            m, alpha_prev = probs(s_bufs[u % 2], p_bufs[u % 2], mc, m)
            mc = mc_next
        return m, mc, alpha_prev

    p1_ref[...] = jnp.zeros(p1_ref.shape, BF16)
    mc_first = qk(0, s0_ref)
    init = (jnp.full((1, lanes), M_INIT, F32), mc_first, jnp.ones((1, lanes), F32))
    _, _, alpha_last = lax.fori_loop(0, n_chunks // unroll, body, init)
    pv(n_chunks - 1, p_bufs[(n_chunks - 1) % 2], alpha_last)

    parts = []
    for acc_ref in (acc0_ref, acc1_ref):
        acc = acc_ref[...]
        o = acc[0:64, :] / acc[64:65, :]
        for g in range(B_GROUP):
            parts.append(o[:, g * tq:(g + 1) * tq])
    o_ref[0, unit * tq:(unit + 1) * tq, :] = jnp.concatenate(parts, axis=0).T


def _attn_b(qbt, kb, vbt, tq, tk):
    b, s, _ = kb.shape
    n_chunks = s // tk
    unroll = next(u for u in (16, 8, 4, 2) if n_chunks % u == 0)
    units = UNITS_PER_STEP if n_chunks == unroll and (s // tq) % UNITS_PER_STEP == 0 else 1
    kern = functools.partial(_attn_b_kernel, tq=tq, tk=tk, n_chunks=n_chunks, unroll=unroll, units=units)
    return pl.pallas_call(
        kern,
        grid=(b, s // (tq * units)),
        in_specs=[
            pl.BlockSpec((1, B_WIDTH, tq * units), lambda i, j: (i, 0, j)),
            pl.BlockSpec((1, s, B_KV_HEADS * B_HEAD_DIM), lambda i, j: (i, 0, 0)),
            pl.BlockSpec((1, B_KV_HEADS, B_HEAD_DIM + ONES_ROWS, s), lambda i, j: (i, 0, 0, 0)),
        ],
        out_specs=pl.BlockSpec((1, tq * units, B_WIDTH), lambda i, j: (i, j, 0)),
        out_shape=jax.ShapeDtypeStruct((b, s, B_WIDTH), F32),
        scratch_shapes=[
            pltpu.VMEM((B_KV_HEADS * B_HEAD_DIM, B_HEADS * tq), BF16),
            pltpu.VMEM((B_HEAD_DIM + ONES_ROWS, B_GROUP * tq), F32),
            pltpu.VMEM((B_HEAD_DIM + ONES_ROWS, B_GROUP * tq), F32),
            pltpu.VMEM((tk, B_HEADS * tq), F32),
            pltpu.VMEM((tk, B_HEADS * tq), F32),
            pltpu.VMEM((tk, B_HEADS * tq), BF16),
            pltpu.VMEM((tk, B_HEADS * tq), BF16),
        ] * units,
        compiler_params=pltpu.CompilerParams(
            dimension_semantics=("arbitrary", "arbitrary"), vmem_limit_bytes=VMEM_LIMIT_BYTES),
        name="attn_b",
    )(qbt, kb, vbt)


def _outproj_kernel(oa_ref, ob_ref, gate_ref, x_ref, mod_ref, w_ref, fg_ref, o_ref, *, final):
    ma = (oa_ref[0] * gate_ref[0, :, 0:A_WIDTH]).astype(BF16)
    mb = (ob_ref[0] * gate_ref[0, :, A_WIDTH:D_MODEL]).astype(BF16)
    y = (jnp.dot(ma, w_ref[0:A_WIDTH, :], preferred_element_type=F32)
         + jnp.dot(mb, w_ref[A_WIDTH:D_MODEL, :], preferred_element_type=F32))
    xn = x_ref[0] + mod_ref[0, 2:3, :] * y
    if final:
        xn = _rms_rows(xn, fg_ref[...])
    o_ref[0] = xn


def _outproj(oa, ob, gate, x, mod, w, fg, tm, final):
    b, s, _ = x.shape
    tok = lambda i, j: (i, j, 0)
    return pl.pallas_call(
        functools.partial(_outproj_kernel, final=final),
        grid=(b, s // tm),
        in_specs=[
            pl.BlockSpec((1, tm, A_WIDTH), tok),
            pl.BlockSpec((1, tm, B_WIDTH), tok),
            pl.BlockSpec((1, tm, D_MODEL), tok),
            pl.BlockSpec((1, tm, D_MODEL), tok),
            pl.BlockSpec((1, 3, D_MODEL), lambda i, j: (i, 0, 0)),
            pl.BlockSpec((D_MODEL, D_MODEL), lambda i, j: (0, 0)),
            pl.BlockSpec((1, D_MODEL), lambda i, j: (0, 0)),
        ],
        out_specs=pl.BlockSpec((1, tm, D_MODEL), tok),
        out_shape=jax.ShapeDtypeStruct((b, s, D_MODEL), F32),
        compiler_params=pltpu.CompilerParams(
            dimension_semantics=("arbitrary", "arbitrary"), vmem_limit_bytes=VMEM_LIMIT_BYTES),
        name="outproj",
    )(oa, ob, gate, x, mod, w, fg)


def _t5_bucket(rel):
    half = NUM_BUCKETS // 2
    max_exact = half // 2
    ret = jnp.where(rel > 0, half, 0)
    n = jnp.abs(rel)
    nf = jnp.maximum(n, 1).astype(jnp.float32)
    large = max_exact + (jnp.log(nf / max_exact) / math.log(MAX_DISTANCE / max_exact)
                         * (half - max_exact)).astype(jnp.int32)
    large = jnp.minimum(large, half - 1)
    return ret + jnp.where(n < max_exact, n, large)


def _bias_tables(rel_table, s, tq, tk):
    ratio = tk // tq
    offsets = jnp.arange(-(s - 1), s, dtype=jnp.int32)
    boff = rel_table[_t5_bucket(offsets)].T.astype(F32) * LOG2E
    span = tk + tq - 1
    pad = tk + tq
    boff_p = jnp.pad(boff, ((0, 0), (pad, pad)), mode="edge")
    tiles = []
    for dc in (-1, 0, 1):
        for r in range(ratio):
            first = dc * tk - r * tq - (tq - 1) + (s - 1) + pad
            u = jnp.flip(boff_p[:, first:first + span], axis=1)
            hankel = jnp.tile(u, (1, tk + 1))[:, :tk * (span + 1)].reshape(A_HEADS, tk, span + 1)[:, :, :tq]
            tiles.append(jnp.flip(hankel, axis=1))
    assert tk >= MAX_DISTANCE and s - 1 >= MAX_DISTANCE
    for col in (0, 2 * s - 2):
        tiles.append(jnp.broadcast_to(boff[:, col][:, None, None], (A_HEADS, tk, tq)))
    return jnp.stack(tiles, axis=1)


def _rope_tables_t(s):
    rows = s // GRID_W
    row = jnp.repeat(jnp.arange(rows), GRID_W).astype(F32)
    col = jnp.tile(jnp.arange(GRID_W), rows).astype(F32)
    axis_dim = B_HEAD_DIM // 2
    inv_freq = ROPE_THETA ** (-jnp.arange(0, axis_dim, 2, dtype=F32) / axis_dim)
    ang_r = row[:, None] * inv_freq[None, :]
    ang_c = col[:, None] * inv_freq[None, :]
    ang = jnp.concatenate([ang_r, ang_r, ang_c, ang_c], axis=-1)
    return jnp.cos(ang).T, jnp.sin(ang).T


def _tile_sizes(s):
    tm = 512
    tk_a = 512
    return tm, (256, tk_a), (128, 256)


def _trunk(x, mod, lam, rel_table, norm_g, wn, wt, subln_g, q_norm_g, k_norm_g, w_out, final_g):
    b, s, _ = x.shape
    tm, (tq_a, tk_a), (tq_b, tk_b) = _tile_sizes(s)
    cos_t, sin_t = _rope_tables_t(s)
    bias_tiles = _bias_tables(rel_table, s, tq_a, tk_a)
    fg = final_g.reshape(1, D_MODEL)
    for l in range(DEPTH):
        lam_init = 0.8 - 0.6 * math.exp(-0.3 * l)
        qg_b = jnp.broadcast_to(q_norm_g[l][:, None], (B_HEAD_DIM, tm))
        kg_b = jnp.broadcast_to(k_norm_g[l][:, None], (B_HEAD_DIM, tm))
        sg_b = jnp.broadcast_to(subln_g[l][:, None], (A_V_DIM, tq_a))
        ka, gate, qat, vat, qbt, kb, vbt = _inproj(
            x, mod[l], norm_g[l].reshape(1, D_MODEL), wn[l], wt[l], cos_t, sin_t, qg_b, kg_b, tm)
        oa = _attn_a(lam[l:l + 1], qat, ka, vat, bias_tiles, sg_b, tq_a, tk_a, 1.0 - lam_init)
        ob = _attn_b(qbt, kb, vbt, tq_b, tk_b)
        x = _outproj(oa, ob, gate, x, mod[l], w_out[l], fg, tm, final=(l == DEPTH - 1))
    return x


def kernel(x_prompt, x_sample, c_prompt, c_sample, rel_table, norm_g, w_ada, b_ada, w_in, lam_q1, lam_k1, lam_q2,
           lam_k2, subln_g, q_norm_g, k_norm_g, w_out, final_g):
    bp = x_prompt.shape[0]
    bs = x_sample.shape[0]
    rows = -(-(bp + bs) // 8) * 8
    c_all = jnp.concatenate([c_prompt, c_sample, jnp.zeros((rows - bp - bs, D_MODEL), F32)], axis=0)
    mod_all = _adaln_mod(c_all, w_ada.astype(BF16), b_ada)
    mod_p = mod_all[:, :bp].reshape(DEPTH, bp, 3, D_MODEL)
    mod_s = mod_all[:, bp:bp + bs].reshape(DEPTH, bs, 3, D_MODEL)
    lam = _lambdas(lam_q1, lam_k1, lam_q2, lam_k2)

    w_bf = w_in.astype(BF16)
    wn = jnp.concatenate([w_bf[:, :, _C_KA:_C_VA], w_bf[:, :, _C_GA:_C_QB], w_bf[:, :, _C_GB:D_IN]], axis=2)
    wt = jnp.concatenate([w_bf[:, :, _C_QA:_C_KA], w_bf[:, :, _C_VA:_C_GA], w_bf[:, :, _C_QB:_C_GB]], axis=2)
    wt = jnp.swapaxes(wt, 1, 2)
    w_out_bf = w_out.astype(BF16)

    args = (rel_table, norm_g, wn, wt, subln_g, q_norm_g, k_norm_g, w_out_bf, final_g)
    y_prompt = _trunk(x_prompt, mod_p, lam, *args)
    y_sample = _trunk(x_sample, mod_s, lam, *args)
    return (y_prompt, y_sample)
```

```python
import functools
import math

import jax
import jax.numpy as jnp
from jax import lax
from jax.experimental import pallas as pl
from jax.experimental.pallas import tpu as pltpu

F32 = jnp.float32
BF16 = jnp.bfloat16

D_MODEL = 1024
DEPTH = 4
GRID_W = 64
A_WIDTH = 512
A_V_DIM = 128
A_HEAD_DIM = 64
A_HEADS = 4
B_WIDTH = 512
B_HEAD_DIM = 64
B_HEADS = 8
B_KV_HEADS = 2
B_GROUP = 4
NUM_BUCKETS = 32
MAX_DISTANCE = 128
ROPE_THETA = 10000.0
EPS = 1e-6

LOG2E = 1.4426950408889634
Q_SCALE = (A_HEAD_DIM ** -0.5) * LOG2E
M_INIT = -1e30
ONES_ROWS = 16
VMEM_LIMIT_BYTES = 56 * 1024 * 1024
UNITS_PER_STEP = 4

_C_QA, _C_KA, _C_VA, _C_GA, _C_QB, _C_KB, _C_VB, _C_GB = 0, 512, 1024, 1536, 2048, 2560, 2688, 2816
D_IN = 3328
N_NAT = 1536
N_TR = 1792


def _rms_rows(x, g):
    ms = jnp.mean(x * x, axis=-1, keepdims=True)
    return x * lax.rsqrt(ms + EPS) * g


def _mod_kernel(c_ref, w_ref, b_ref, o_ref):
    c = c_ref[...]
    c_act = (c * jax.nn.sigmoid(c)).astype(BF16)
    o_ref[0] = jnp.dot(c_act, w_ref[0], preferred_element_type=F32) + b_ref[0]


def _adaln_mod(c_all, w_ada_bf, b_ada):
    rows = c_all.shape[0]
    return pl.pallas_call(
        _mod_kernel,
        grid=(DEPTH, 3),
        in_specs=[
            pl.BlockSpec((rows, D_MODEL), lambda l, j: (0, 0)),
            pl.BlockSpec((1, D_MODEL, D_MODEL), lambda l, j: (l, 0, j)),
            pl.BlockSpec((1, 1, D_MODEL), lambda l, j: (l, 0, j)),
        ],
        out_specs=pl.BlockSpec((1, rows, D_MODEL), lambda l, j: (l, 0, j)),
        out_shape=jax.ShapeDtypeStruct((DEPTH, rows, 3 * D_MODEL), F32),
        name="adaln_mod",
    )(c_all, w_ada_bf, b_ada.reshape(DEPTH, 1, 3 * D_MODEL))


def _lam_kernel(q1_ref, k1_ref, q2_ref, k2_ref, init_ref, o_ref):
    s1 = jnp.sum(q1_ref[...] * k1_ref[...], axis=-1, keepdims=True)
    s2 = jnp.sum(q2_ref[...] * k2_ref[...], axis=-1, keepdims=True)
    lam = jnp.exp(s1) - jnp.exp(s2) + init_ref[...][:, 0:1]
    o_ref[...] = jnp.broadcast_to(lam, o_ref.shape)


def _lambdas(lam_q1, lam_k1, lam_q2, lam_k2):
    init = jnp.asarray([[0.8 - 0.6 * math.exp(-0.3 * l)] * 128 for l in range(DEPTH)], F32)
    out = pl.pallas_call(
        _lam_kernel,
        out_shape=jax.ShapeDtypeStruct((DEPTH, 128), F32),
        name="diff_lambda",
    )(lam_q1, lam_k1, lam_q2, lam_k2, init)
    return out[:, 0]


def _rope_t(x, cos, sin):
    rot = jnp.concatenate([-x[16:32], x[0:16], -x[48:64], x[32:48]], axis=0)
    return x * cos + rot * sin


def _norm_t(x, g):
    ms = jnp.mean(x * x, axis=0, keepdims=True)
    return x * lax.rsqrt(ms + EPS) * g


def _inproj_kernel(x_ref, mod_ref, ng_ref, wn_ref, wt_ref, cos_ref, sin_ref, qg_ref, kg_ref,
                   ka_ref, gate_ref, qat_ref, vat_ref, qbt_ref, kb_ref, vbt_ref):
    tm = x_ref.shape[1]
    x = x_ref[0]
    shift = mod_ref[0, 0:1, :]
    scale = mod_ref[0, 1:2, :]
    h = (_rms_rows(x, ng_ref[...]) * (1.0 + scale) + shift).astype(BF16)

    pn = jnp.dot(h, wn_ref[...], preferred_element_type=F32)
    ka_ref[0] = pn[:, 0:512].astype(BF16)
    g = pn[:, 512:N_NAT]
    gate_ref[0] = g * jax.nn.sigmoid(g)

    pt = lax.dot_general(wt_ref[...], h, (((1,), (1,)), ((), ())), preferred_element_type=F32)
    qat_ref[0] = (pt[0:512] * Q_SCALE).astype(BF16)
    ones = jnp.ones((ONES_ROWS, tm), BF16)
    for hh in range(A_HEADS):
        vat_ref[0, hh, 0:A_V_DIM, :] = pt[512 + hh * A_V_DIM:512 + (hh + 1) * A_V_DIM].astype(BF16)
        vat_ref[0, hh, A_V_DIM:A_V_DIM + ONES_ROWS, :] = ones

    cos = cos_ref[...]
    sin = sin_ref[...]
    qg = qg_ref[...]
    for hh in range(B_HEADS):
        xh = pt[1024 + hh * 64:1024 + (hh + 1) * 64]
        qbt_ref[0, hh * 64:(hh + 1) * 64, :] = (_rope_t(_norm_t(xh, qg), cos, sin) * Q_SCALE).astype(BF16)
    kg = kg_ref[...]
    kparts = []
    for n in range(B_KV_HEADS):
        xh = pt[1536 + n * 64:1536 + (n + 1) * 64]
        kparts.append(_rope_t(_norm_t(xh, kg), cos, sin))
    kb_ref[0] = jnp.concatenate(kparts, axis=0).T.astype(BF16)
    for n in range(B_KV_HEADS):
        vbt_ref[0, n, 0:64, :] = pt[1664 + n * 64:1664 + (n + 1) * 64].astype(BF16)
        vbt_ref[0, n, 64:64 + ONES_ROWS, :] = ones


def _inproj(x, mod, ng, wn, wt, cos_t, sin_t, qg_b, kg_b, tm):
    b, s, _ = x.shape
    grid = (b, s // tm)
    const2 = lambda i, j: (0, 0)
    return pl.pallas_call(
        _inproj_kernel,
        grid=grid,
        in_specs=[
            pl.BlockSpec((1, tm, D_MODEL), lambda i, j: (i, j, 0)),
            pl.BlockSpec((1, 3, D_MODEL), lambda i, j: (i, 0, 0)),
            pl.BlockSpec((1, D_MODEL), const2),
            pl.BlockSpec((D_MODEL, N_NAT), const2),
            pl.BlockSpec((N_TR, D_MODEL), const2),
            pl.BlockSpec((B_HEAD_DIM, tm), lambda i, j: (0, j)),
            pl.BlockSpec((B_HEAD_DIM, tm), lambda i, j: (0, j)),
            pl.BlockSpec((B_HEAD_DIM, tm), const2),
            pl.BlockSpec((B_HEAD_DIM, tm), const2),
        ],
        out_specs=[
            pl.BlockSpec((1, tm, A_WIDTH), lambda i, j: (i, j, 0)),
            pl.BlockSpec((1, tm, D_MODEL), lambda i, j: (i, j, 0)),
            pl.BlockSpec((1, A_WIDTH, tm), lambda i, j: (i, 0, j)),
            pl.BlockSpec((1, A_HEADS, A_V_DIM + ONES_ROWS, tm), lambda i, j: (i, 0, 0, j)),
            pl.BlockSpec((1, B_WIDTH, tm), lambda i, j: (i, 0, j)),
            pl.BlockSpec((1, tm, B_KV_HEADS * B_HEAD_DIM), lambda i, j: (i, j, 0)),
            pl.BlockSpec((1, B_KV_HEADS, B_HEAD_DIM + ONES_ROWS, tm), lambda i, j: (i, 0, 0, j)),
        ],
        out_shape=[
            jax.ShapeDtypeStruct((b, s, A_WIDTH), BF16),
            jax.ShapeDtypeStruct((b, s, D_MODEL), F32),
            jax.ShapeDtypeStruct((b, A_WIDTH, s), BF16),
            jax.ShapeDtypeStruct((b, A_HEADS, A_V_DIM + ONES_ROWS, s), BF16),
            jax.ShapeDtypeStruct((b, B_WIDTH, s), BF16),
            jax.ShapeDtypeStruct((b, s, B_KV_HEADS * B_HEAD_DIM), BF16),
            jax.ShapeDtypeStruct((b, B_KV_HEADS, B_HEAD_DIM + ONES_ROWS, s), BF16),
        ],
        compiler_params=pltpu.CompilerParams(
            dimension_semantics=("arbitrary", "arbitrary"), vmem_limit_bytes=VMEM_LIMIT_BYTES),
        name="inproj",
    )(x, mod, ng, wn, wt, cos_t, sin_t, qg_b, kg_b)


def _attn_a_kernel(scal_ref, qt_ref, k_ref, vt_ref, bias_ref, sg_ref, o_ref, *scratch,
                   tq, tk, n_chunks, ratio, unroll, units, out_scale):
    for unit in range(units):
        wq_ref, acc_ref, s0_ref, s1_ref = scratch[4 * unit:4 * unit + 4]
        _attn_a_unit(scal_ref, qt_ref, k_ref, vt_ref, bias_ref, sg_ref, o_ref, wq_ref, acc_ref, s0_ref, s1_ref,
                     unit, pl.program_id(2) * units + unit, tq=tq, tk=tk, n_chunks=n_chunks, ratio=ratio,
                     unroll=unroll, out_scale=out_scale)


def _attn_a_unit(scal_ref, qt_ref, k_ref, vt_ref, bias_ref, sg_ref, o_ref, wq_ref, acc_ref, s0_ref, s1_ref,
                 unit, qi, *, tq, tk, n_chunks, ratio, unroll, out_scale):
    s_bufs = (s0_ref, s1_ref)
    q = qt_ref[0, :, unit * tq:(unit + 1) * tq]
    zero = jnp.zeros((A_HEAD_DIM, tq), BF16)
    wq_ref[0:64, 0:tq] = q[0:64]
    wq_ref[0:64, tq:2 * tq] = zero
    wq_ref[64:128, 0:tq] = zero
    wq_ref[64:128, tq:2 * tq] = q[64:128]
    acc_ref[...] = jnp.zeros(acc_ref.shape, F32)

    lam = scal_ref[0]
    log_ratio = ratio.bit_length() - 1
    c_own = lax.shift_right_logical(qi, log_ratio)
    r = lax.bitwise_and(qi, ratio - 1)

    def qk(c, s_ref):
        start = pl.multiple_of(c * tk, tk)
        d = c - c_own
        tile = jnp.where(d < -1, 3 * ratio, jnp.where(d > 1, 3 * ratio + 1, (d + 1) * ratio + r))
        t = bias_ref[0, tile]
        s = jnp.dot(k_ref[0, pl.ds(start, tk), :], wq_ref[...], preferred_element_type=F32)
        s = s + jnp.concatenate([t, t], axis=1)
        s_ref[...] = s
        return jnp.max(s, axis=0, keepdims=True)

    def softmax_pv(s_ref, mc, m, c):
        mn = jnp.maximum(m, mc)
        alpha = jnp.exp2(m - mn)
        p = jnp.exp2(s_ref[...] - mn).astype(BF16)
        start = pl.multiple_of(c * tk, tk)
        pv = jnp.dot(vt_ref[0, 0, :, pl.ds(start, tk)], p, preferred_element_type=F32)
        acc_ref[...] = acc_ref[...] * alpha + pv
        return mn

    def body(i, carry):
        m, mc = carry
        c0 = unroll * i
        for u in range(unroll):
            mc_next = qk(jnp.minimum(c0 + u + 1, n_chunks - 1), s_bufs[(u + 1) % 2])
            m = softmax_pv(s_bufs[u % 2], mc, m, c0 + u)
            mc = mc_next
        return m, mc

    mc_first = qk(0, s_bufs[0])
    lax.fori_loop(0, n_chunks // unroll, body, (jnp.full((1, 2 * tq), M_INIT, F32), mc_first))

    acc = acc_ref[...]
    o = acc[0:A_V_DIM, :] * (1.0 / acc[A_V_DIM:A_V_DIM + 1, :])
    w = o[:, 0:tq] - lam * o[:, tq:2 * tq]
    y = _norm_t(w, sg_ref[...]) * out_scale
    o_ref[0, unit * tq:(unit + 1) * tq, :] = y.T


def _attn_a(scal, qat, ka, vat, bias_tiles, sg_b, tq, tk, out_scale):
    b, s, _ = ka.shape
    n_chunks = s // tk
    ratio = tk // tq
    unroll = next(u for u in (16, 8, 4, 2) if n_chunks % u == 0)
    units = UNITS_PER_STEP if n_chunks == unroll and (s // tq) % UNITS_PER_STEP == 0 else 1
    assert n_chunks % unroll == 0 and ratio * tq == tk and ratio & (ratio - 1) == 0
    kern = functools.partial(_attn_a_kernel, tq=tq, tk=tk, n_chunks=n_chunks, ratio=ratio, unroll=unroll,
                             units=units, out_scale=out_scale)
    return pl.pallas_call(
        kern,
        grid=(b, A_HEADS, s // (tq * units)),
        in_specs=[
            pl.BlockSpec(memory_space=pltpu.SMEM),
            pl.BlockSpec((1, 2 * A_HEAD_DIM, tq * units), lambda i, hh, j: (i, hh, j)),
            pl.BlockSpec((1, s, 2 * A_HEAD_DIM), lambda i, hh, j: (i, 0, hh)),
            pl.BlockSpec((1, 1, A_V_DIM + ONES_ROWS, s), lambda i, hh, j: (i, hh, 0, 0)),
            pl.BlockSpec((1, 3 * ratio + 2, tk, tq), lambda i, hh, j: (hh, 0, 0, 0)),
            pl.BlockSpec((A_V_DIM, tq), lambda i, hh, j: (0, 0)),
        ],
        out_specs=pl.BlockSpec((1, tq * units, A_V_DIM), lambda i, hh, j: (i, j, hh)),
        out_shape=jax.ShapeDtypeStruct((b, s, A_WIDTH), F32),
        scratch_shapes=[
            pltpu.VMEM((2 * A_HEAD_DIM, 2 * tq), BF16),
            pltpu.VMEM((A_V_DIM + ONES_ROWS, 2 * tq), F32),
            pltpu.VMEM((tk, 2 * tq), F32),
            pltpu.VMEM((tk, 2 * tq), F32),
        ] * units,
        compiler_params=pltpu.CompilerParams(
            dimension_semantics=("arbitrary", "arbitrary", "arbitrary"), vmem_limit_bytes=VMEM_LIMIT_BYTES),
        name="attn_a",
    )(scal, qat, ka, vat, bias_tiles, sg_b)


def _attn_b_kernel(qt_ref, k_ref, vt_ref, o_ref, *scratch, tq, tk, n_chunks, unroll, units):
    for unit in range(units):
        _attn_b_unit(qt_ref, k_ref, vt_ref, o_ref, *scratch[5 * unit:5 * unit + 5], unit,
                     tq=tq, tk=tk, n_chunks=n_chunks, unroll=unroll)


def _attn_b_unit(qt_ref, k_ref, vt_ref, o_ref, wq_ref, acc0_ref, acc1_ref, s0_ref, s1_ref, unit,
                 *, tq, tk, n_chunks, unroll):
    half = B_GROUP * tq
    lanes = B_HEADS * tq
    wq_ref[...] = jnp.zeros(wq_ref.shape, BF16)
    for hh in range(B_HEADS):
        n = hh // B_GROUP
        wq_ref[n * 64:(n + 1) * 64, hh * tq:(hh + 1) * tq] = qt_ref[0, hh * 64:(hh + 1) * 64,
                                                                    unit * tq:(unit + 1) * tq]
    acc0_ref[...] = jnp.zeros(acc0_ref.shape, F32)
    acc1_ref[...] = jnp.zeros(acc1_ref.shape, F32)
    s_bufs = (s0_ref, s1_ref)

    def qk(c, s_ref):
        start = pl.multiple_of(c * tk, tk)
        s = jnp.dot(k_ref[0, pl.ds(start, tk), :], wq_ref[...], preferred_element_type=F32)
        s_ref[...] = s
        return jnp.max(s, axis=0, keepdims=True)

    def softmax_pv(s_ref, mc, m, c):
        mn = jnp.maximum(m, mc)
        alpha = jnp.exp2(m - mn)
        p = jnp.exp2(s_ref[...] - mn).astype(BF16)
        start = pl.multiple_of(c * tk, tk)
        v0 = vt_ref[0, 0, :, pl.ds(start, tk)]
        v1 = vt_ref[0, 1, :, pl.ds(start, tk)]
        acc0_ref[...] = acc0_ref[...] * alpha[:, 0:half] + jnp.dot(v0, p[:, 0:half], preferred_element_type=F32)
        acc1_ref[...] = acc1_ref[...] * alpha[:, half:] + jnp.dot(v1, p[:, half:], preferred_element_type=F32)
        return mn

    def body(i, carry):
        m, mc = carry
        c0 = unroll * i
        for u in range(unroll):
            mc_next = qk(jnp.minimum(c0 + u + 1, n_chunks - 1), s_bufs[(u + 1) % 2])
            m = softmax_pv(s_bufs[u % 2], mc, m, c0 + u)
            mc = mc_next
        return m, mc

    mc_first = qk(0, s0_ref)
    lax.fori_loop(0, n_chunks // unroll, body, (jnp.full((1, lanes), M_INIT, F32), mc_first))

    parts = []
    for acc_ref in (acc0_ref, acc1_ref):
        acc = acc_ref[...]
        o = acc[0:64, :] * (1.0 / acc[64:65, :])
        for g in range(B_GROUP):
            parts.append(o[:, g * tq:(g + 1) * tq])
    o_ref[0, unit * tq:(unit + 1) * tq, :] = jnp.concatenate(parts, axis=0).T


def _attn_b(qbt, kb, vbt, tq, tk):
    b, s, _ = kb.shape
    n_chunks = s // tk
    unroll = next(u for u in (16, 8, 4, 2) if n_chunks % u == 0)
    units = UNITS_PER_STEP if n_chunks == unroll and (s // tq) % UNITS_PER_STEP == 0 else 1
    kern = functools.partial(_attn_b_kernel, tq=tq, tk=tk, n_chunks=n_chunks, unroll=unroll, units=units)
    return pl.pallas_call(
        kern,
        grid=(b, s // (tq * units)),
        in_specs=[
            pl.BlockSpec((1, B_WIDTH, tq * units), lambda i, j: (i, 0, j)),
            pl.BlockSpec((1, s, B_KV_HEADS * B_HEAD_DIM), lambda i, j: (i, 0, 0)),
            pl.BlockSpec((1, B_KV_HEADS, B_HEAD_DIM + ONES_ROWS, s), lambda i, j: (i, 0, 0, 0)),
        ],
        out_specs=pl.BlockSpec((1, tq * units, B_WIDTH), lambda i, j: (i, j, 0)),
        out_shape=jax.ShapeDtypeStruct((b, s, B_WIDTH), F32),
        scratch_shapes=[
            pltpu.VMEM((B_KV_HEADS * B_HEAD_DIM, B_HEADS * tq), BF16),
            pltpu.VMEM((B_HEAD_DIM + ONES_ROWS, B_GROUP * tq), F32),
            pltpu.VMEM((B_HEAD_DIM + ONES_ROWS, B_GROUP * tq), F32),
            pltpu.VMEM((tk, B_HEADS * tq), F32),
            pltpu.VMEM((tk, B_HEADS * tq), F32),
        ] * units,
        compiler_params=pltpu.CompilerParams(
            dimension_semantics=("arbitrary", "arbitrary"), vmem_limit_bytes=VMEM_LIMIT_BYTES),
        name="attn_b",
    )(qbt, kb, vbt)


def _outproj_kernel(oa_ref, ob_ref, gate_ref, x_ref, mod_ref, w_ref, fg_ref, o_ref, *, final):
    ma = (oa_ref[0] * gate_ref[0, :, 0:A_WIDTH]).astype(BF16)
    mb = (ob_ref[0] * gate_ref[0, :, A_WIDTH:D_MODEL]).astype(BF16)
    y = (jnp.dot(ma, w_ref[0:A_WIDTH, :], preferred_element_type=F32)
         + jnp.dot(mb, w_ref[A_WIDTH:D_MODEL, :], preferred_element_type=F32))
    xn = x_ref[0] + mod_ref[0, 2:3, :] * y
    if final:
        xn = _rms_rows(xn, fg_ref[...])
    o_ref[0] = xn


def _outproj(oa, ob, gate, x, mod, w, fg, tm, final):
    b, s, _ = x.shape
    tok = lambda i, j: (i, j, 0)
    return pl.pallas_call(
        functools.partial(_outproj_kernel, final=final),
        grid=(b, s // tm),
        in_specs=[
            pl.BlockSpec((1, tm, A_WIDTH), tok),
            pl.BlockSpec((1, tm, B_WIDTH), tok),
            pl.BlockSpec((1, tm, D_MODEL), tok),
            pl.BlockSpec((1, tm, D_MODEL), tok),
            pl.BlockSpec((1, 3, D_MODEL), lambda i, j: (i, 0, 0)),
            pl.BlockSpec((D_MODEL, D_MODEL), lambda i, j: (0, 0)),
            pl.BlockSpec((1, D_MODEL), lambda i, j: (0, 0)),
        ],
        out_specs=pl.BlockSpec((1, tm, D_MODEL), tok),
        out_shape=jax.ShapeDtypeStruct((b, s, D_MODEL), F32),
        compiler_params=pltpu.CompilerParams(
            dimension_semantics=("arbitrary", "arbitrary"), vmem_limit_bytes=VMEM_LIMIT_BYTES),
        name="outproj",
    )(oa, ob, gate, x, mod, w, fg)


def _t5_bucket(rel):
    half = NUM_BUCKETS // 2
    max_exact = half // 2
    ret = jnp.where(rel > 0, half, 0)
    n = jnp.abs(rel)
    nf = jnp.maximum(n, 1).astype(jnp.float32)
    large = max_exact + (jnp.log(nf / max_exact) / math.log(MAX_DISTANCE / max_exact)
                         * (half - max_exact)).astype(jnp.int32)
    large = jnp.minimum(large, half - 1)
    return ret + jnp.where(n < max_exact, n, large)


def _bias_tables(rel_table, s, tq, tk):
    ratio = tk // tq
    offsets = jnp.arange(-(s - 1), s, dtype=jnp.int32)
    boff = rel_table[_t5_bucket(offsets)].T.astype(F32) * LOG2E
    span = tk + tq - 1
    pad = tk + tq
    boff_p = jnp.pad(boff, ((0, 0), (pad, pad)), mode="edge")
    tiles = []
    for dc in (-1, 0, 1):
        for r in range(ratio):
            first = dc * tk - r * tq - (tq - 1) + (s - 1) + pad
            u = jnp.flip(boff_p[:, first:first + span], axis=1)
            hankel = jnp.tile(u, (1, tk + 1))[:, :tk * (span + 1)].reshape(A_HEADS, tk, span + 1)[:, :, :tq]
            tiles.append(jnp.flip(hankel, axis=1))
    assert tk >= MAX_DISTANCE and s - 1 >= MAX_DISTANCE
    for col in (0, 2 * s - 2):
        tiles.append(jnp.broadcast_to(boff[:, col][:, None, None], (A_HEADS, tk, tq)))
    return jnp.stack(tiles, axis=1)


def _rope_tables_t(s):
    rows = s // GRID_W
    row = jnp.repeat(jnp.arange(rows), GRID_W).astype(F32)
    col = jnp.tile(jnp.arange(GRID_W), rows).astype(F32)
    axis_dim = B_HEAD_DIM // 2
    inv_freq = ROPE_THETA ** (-jnp.arange(0, axis_dim, 2, dtype=F32) / axis_dim)
    ang_r = row[:, None] * inv_freq[None, :]
    ang_c = col[:, None] * inv_freq[None, :]
    ang = jnp.concatenate([ang_r, ang_r, ang_c, ang_c], axis=-1)
    return jnp.cos(ang).T, jnp.sin(ang).T


def _tile_sizes(s):
    tm = 512
    tk_a = 512
    return tm, (256, tk_a), (128, 256)


def _trunk(x, mod, lam, rel_table, norm_g, wn, wt, subln_g, q_norm_g, k_norm_g, w_out, final_g):
    b, s, _ = x.shape
    tm, (tq_a, tk_a), (tq_b, tk_b) = _tile_sizes(s)
    cos_t, sin_t = _rope_tables_t(s)
    bias_tiles = _bias_tables(rel_table, s, tq_a, tk_a)
    fg = final_g.reshape(1, D_MODEL)
    for l in range(DEPTH):
        lam_init = 0.8 - 0.6 * math.exp(-0.3 * l)
        qg_b = jnp.broadcast_to(q_norm_g[l][:, None], (B_HEAD_DIM, tm))
        kg_b = jnp.broadcast_to(k_norm_g[l][:, None], (B_HEAD_DIM, tm))
        sg_b = jnp.broadcast_to(subln_g[l][:, None], (A_V_DIM, tq_a))
        ka, gate, qat, vat, qbt, kb, vbt = _inproj(
            x, mod[l], norm_g[l].reshape(1, D_MODEL), wn[l], wt[l], cos_t, sin_t, qg_b, kg_b, tm)
        oa = _attn_a(lam[l:l + 1], qat, ka, vat, bias_tiles, sg_b, tq_a, tk_a, 1.0 - lam_init)
        ob = _attn_b(qbt, kb, vbt, tq_b, tk_b)
        x = _outproj(oa, ob, gate, x, mod[l], w_out[l], fg, tm, final=(l == DEPTH - 1))
    return x


def kernel(x_prompt, x_sample, c_prompt, c_sample, rel_table, norm_g, w_ada, b_ada, w_in, lam_q1, lam_k1, lam_q2,
           lam_k2, subln_g, q_norm_g, k_norm_g, w_out, final_g):
    bp = x_prompt.shape[0]
    bs = x_sample.shape[0]
    rows = -(-(bp + bs) // 8) * 8
    c_all = jnp.concatenate([c_prompt, c_sample, jnp.zeros((rows - bp - bs, D_MODEL), F32)], axis=0)
    mod_all = _adaln_mod(c_all, w_ada.astype(BF16), b_ada)
    mod_p = mod_all[:, :bp].reshape(DEPTH, bp, 3, D_MODEL)
    mod_s = mod_all[:, bp:bp + bs].reshape(DEPTH, bs, 3, D_MODEL)
    lam = _lambdas(lam_q1, lam_k1, lam_q2, lam_k2)

    w_bf = w_in.astype(BF16)
    wn = jnp.concatenate([w_bf[:, :, _C_KA:_C_VA], w_bf[:, :, _C_GA:_C_QB], w_bf[:, :, _C_GB:D_IN]], axis=2)
    wt = jnp.concatenate([w_bf[:, :, _C_QA:_C_KA], w_bf[:, :, _C_VA:_C_GA], w_bf[:, :, _C_QB:_C_GB]], axis=2)
    wt = jnp.swapaxes(wt, 1, 2)
    w_out_bf = w_out.astype(BF16)

    args = (rel_table, norm_g, wn, wt, subln_g, q_norm_g, k_norm_g, w_out_bf, final_g)
    y_prompt = _trunk(x_prompt, mod_p, lam, *args)
    y_sample = _trunk(x_sample, mod_s, lam, *args)
    return (y_prompt, y_sample)
```

```python
import functools
import math

import jax
import jax.numpy as jnp
from jax import lax
from jax.experimental import pallas as pl
from jax.experimental.pallas import tpu as pltpu

F32 = jnp.float32
BF16 = jnp.bfloat16

D_MODEL = 1024
DEPTH = 4
GRID_W = 64
A_WIDTH = 512
A_V_DIM = 128
A_HEAD_DIM = 64
A_HEADS = 4
B_WIDTH = 512
B_HEAD_DIM = 64
B_HEADS = 8
B_KV_HEADS = 2
B_GROUP = 4
NUM_BUCKETS = 32
MAX_DISTANCE = 128
ROPE_THETA = 10000.0
EPS = 1e-6

LOG2E = 1.4426950408889634
Q_SCALE = (A_HEAD_DIM ** -0.5) * LOG2E
M_INIT = -1e30
ONES_ROWS = 16
VMEM_LIMIT_BYTES = 56 * 1024 * 1024
UNITS_PER_STEP = 4

_C_QA, _C_KA, _C_VA, _C_GA, _C_QB, _C_KB, _C_VB, _C_GB = 0, 512, 1024, 1536, 2048, 2560, 2688, 2816
D_IN = 3328
N_NAT = 1536
N_TR = 1792


def _rms_rows(x, g):
    ms = jnp.mean(x * x, axis=-1, keepdims=True)
    return x * lax.rsqrt(ms + EPS) * g


def _mod_kernel(c_ref, w_ref, b_ref, o_ref):
    c = c_ref[...]
    c_act = (c * jax.nn.sigmoid(c)).astype(BF16)
    o_ref[0] = jnp.dot(c_act, w_ref[0], preferred_element_type=F32) + b_ref[0]


def _adaln_mod(c_all, w_ada_bf, b_ada):
    rows = c_all.shape[0]
    return pl.pallas_call(
        _mod_kernel,
        grid=(DEPTH, 3),
        in_specs=[
            pl.BlockSpec((rows, D_MODEL), lambda l, j: (0, 0)),
            pl.BlockSpec((1, D_MODEL, D_MODEL), lambda l, j: (l, 0, j)),
            pl.BlockSpec((1, 1, D_MODEL), lambda l, j: (l, 0, j)),
        ],
        out_specs=pl.BlockSpec((1, rows, D_MODEL), lambda l, j: (l, 0, j)),
        out_shape=jax.ShapeDtypeStruct((DEPTH, rows, 3 * D_MODEL), F32),
        name="adaln_mod",
    )(c_all, w_ada_bf, b_ada.reshape(DEPTH, 1, 3 * D_MODEL))


def _lam_kernel(q1_ref, k1_ref, q2_ref, k2_ref, init_ref, o_ref):
    s1 = jnp.sum(q1_ref[...] * k1_ref[...], axis=-1, keepdims=True)
    s2 = jnp.sum(q2_ref[...] * k2_ref[...], axis=-1, keepdims=True)
    lam = jnp.exp(s1) - jnp.exp(s2) + init_ref[...][:, 0:1]
    o_ref[...] = jnp.broadcast_to(lam, o_ref.shape)


def _lambdas(lam_q1, lam_k1, lam_q2, lam_k2):
    init = jnp.asarray([[0.8 - 0.6 * math.exp(-0.3 * l)] * 128 for l in range(DEPTH)], F32)
    out = pl.pallas_call(
        _lam_kernel,
        out_shape=jax.ShapeDtypeStruct((DEPTH, 128), F32),
        name="diff_lambda",
    )(lam_q1, lam_k1, lam_q2, lam_k2, init)
    return out[:, 0]


def _rope_t(x, cos, sin):
    rot = jnp.concatenate([-x[16:32], x[0:16], -x[48:64], x[32:48]], axis=0)
    return x * cos + rot * sin


def _norm_t(x, g):
    ms = jnp.mean(x * x, axis=0, keepdims=True)
    return x * lax.rsqrt(ms + EPS) * g


def _inproj_kernel(x_ref, mod_ref, ng_ref, wn_ref, wt_ref, cos_ref, sin_ref, qg_ref, kg_ref,
                   ka_ref, gate_ref, qat_ref, vat_ref, qbt_ref, kb_ref, vbt_ref):
    tm = x_ref.shape[1]
    x = x_ref[0]
    shift = mod_ref[0, 0:1, :]
    scale = mod_ref[0, 1:2, :]
    h = (_rms_rows(x, ng_ref[...]) * (1.0 + scale) + shift).astype(BF16)

    pn = jnp.dot(h, wn_ref[...], preferred_element_type=F32)
    ka_ref[0] = pn[:, 0:512].astype(BF16)
    g = pn[:, 512:N_NAT]
    gate_ref[0] = g * jax.nn.sigmoid(g)

    pt = lax.dot_general(wt_ref[...], h, (((1,), (1,)), ((), ())), preferred_element_type=F32)
    qat_ref[0] = (pt[0:512] * Q_SCALE).astype(BF16)
    ones = jnp.ones((ONES_ROWS, tm), BF16)
    for hh in range(A_HEADS):
        vat_ref[0, hh, 0:A_V_DIM, :] = pt[512 + hh * A_V_DIM:512 + (hh + 1) * A_V_DIM].astype(BF16)
        vat_ref[0, hh, A_V_DIM:A_V_DIM + ONES_ROWS, :] = ones

    cos = cos_ref[...]
    sin = sin_ref[...]
    qg = qg_ref[...]
    for hh in range(B_HEADS):
        xh = pt[1024 + hh * 64:1024 + (hh + 1) * 64]
        qbt_ref[0, hh * 64:(hh + 1) * 64, :] = (_rope_t(_norm_t(xh, qg), cos, sin) * Q_SCALE).astype(BF16)
    kg = kg_ref[...]
    kparts = []
    for n in range(B_KV_HEADS):
        xh = pt[1536 + n * 64:1536 + (n + 1) * 64]
        kparts.append(_rope_t(_norm_t(xh, kg), cos, sin))
    kb_ref[0] = jnp.concatenate(kparts, axis=0).T.astype(BF16)
    for n in range(B_KV_HEADS):
        vbt_ref[0, n, 0:64, :] = pt[1664 + n * 64:1664 + (n + 1) * 64].astype(BF16)
        vbt_ref[0, n, 64:64 + ONES_ROWS, :] = ones


def _inproj(x, mod, ng, wn, wt, cos_t, sin_t, qg_b, kg_b, tm):
    b, s, _ = x.shape
    grid = (b, s // tm)
    const2 = lambda i, j: (0, 0)
    return pl.pallas_call(
        _inproj_kernel,
        grid=grid,
        in_specs=[
            pl.BlockSpec((1, tm, D_MODEL), lambda i, j: (i, j, 0)),
            pl.BlockSpec((1, 3, D_MODEL), lambda i, j: (i, 0, 0)),
            pl.BlockSpec((1, D_MODEL), const2),
            pl.BlockSpec((D_MODEL, N_NAT), const2),
            pl.BlockSpec((N_TR, D_MODEL), const2),
            pl.BlockSpec((B_HEAD_DIM, tm), lambda i, j: (0, j)),
            pl.BlockSpec((B_HEAD_DIM, tm), lambda i, j: (0, j)),
            pl.BlockSpec((B_HEAD_DIM, tm), const2),
            pl.BlockSpec((B_HEAD_DIM, tm), const2),
        ],
        out_specs=[
            pl.BlockSpec((1, tm, A_WIDTH), lambda i, j: (i, j, 0)),
            pl.BlockSpec((1, tm, D_MODEL), lambda i, j: (i, j, 0)),
            pl.BlockSpec((1, A_WIDTH, tm), lambda i, j: (i, 0, j)),
            pl.BlockSpec((1, A_HEADS, A_V_DIM + ONES_ROWS, tm), lambda i, j: (i, 0, 0, j)),
            pl.BlockSpec((1, B_WIDTH, tm), lambda i, j: (i, 0, j)),
            pl.BlockSpec((1, tm, B_KV_HEADS * B_HEAD_DIM), lambda i, j: (i, j, 0)),
            pl.BlockSpec((1, B_KV_HEADS, B_HEAD_DIM + ONES_ROWS, tm), lambda i, j: (i, 0, 0, j)),
        ],
        out_shape=[
            jax.ShapeDtypeStruct((b, s, A_WIDTH), BF16),
            jax.ShapeDtypeStruct((b, s, D_MODEL), F32),
            jax.ShapeDtypeStruct((b, A_WIDTH, s), BF16),
            jax.ShapeDtypeStruct((b, A_HEADS, A_V_DIM + ONES_ROWS, s), BF16),
            jax.ShapeDtypeStruct((b, B_WIDTH, s), BF16),
            jax.ShapeDtypeStruct((b, s, B_KV_HEADS * B_HEAD_DIM), BF16),
            jax.ShapeDtypeStruct((b, B_KV_HEADS, B_HEAD_DIM + ONES_ROWS, s), BF16),
        ],
        compiler_params=pltpu.CompilerParams(
            dimension_semantics=("arbitrary", "arbitrary"), vmem_limit_bytes=VMEM_LIMIT_BYTES),
        name="inproj",
    )(x, mod, ng, wn, wt, cos_t, sin_t, qg_b, kg_b)


def _attn_a_kernel(scal_ref, qt_ref, k_ref, vt_ref, bias_ref, sg_ref, gate_ref, o_ref, *scratch,
                   tq, tk, n_chunks, ratio, unroll, units, out_scale):
    for unit in range(units):
        wq_ref, acc_ref, s0_ref, s1_ref = scratch[4 * unit:4 * unit + 4]
        _attn_a_unit(scal_ref, qt_ref, k_ref, vt_ref, bias_ref, sg_ref, gate_ref, o_ref, wq_ref, acc_ref, s0_ref, s1_ref,
                     unit, pl.program_id(2) * units + unit, tq=tq, tk=tk, n_chunks=n_chunks, ratio=ratio,
                     unroll=unroll, out_scale=out_scale)


def _attn_a_unit(scal_ref, qt_ref, k_ref, vt_ref, bias_ref, sg_ref, gate_ref, o_ref, wq_ref, acc_ref, s0_ref, s1_ref,
                 unit, qi, *, tq, tk, n_chunks, ratio, unroll, out_scale):
    s_bufs = (s0_ref, s1_ref)
    q = qt_ref[0, :, unit * tq:(unit + 1) * tq]
    zero = jnp.zeros((A_HEAD_DIM, tq), BF16)
    wq_ref[0:64, 0:tq] = q[0:64]
    wq_ref[0:64, tq:2 * tq] = zero
    wq_ref[64:128, 0:tq] = zero
    wq_ref[64:128, tq:2 * tq] = q[64:128]
    acc_ref[...] = jnp.zeros(acc_ref.shape, F32)

    lam = scal_ref[0]
    log_ratio = ratio.bit_length() - 1
    c_own = lax.shift_right_logical(qi, log_ratio)
    r = lax.bitwise_and(qi, ratio - 1)

    def qk(c, s_ref):
        start = pl.multiple_of(c * tk, tk)
        d = c - c_own
        tile = jnp.where(d < -1, 3 * ratio, jnp.where(d > 1, 3 * ratio + 1, (d + 1) * ratio + r))
        t = bias_ref[0, tile]
        s = jnp.dot(k_ref[0, pl.ds(start, tk), :], wq_ref[...], preferred_element_type=F32)
        s = s + jnp.concatenate([t, t], axis=1)
        s_ref[...] = s
        return jnp.max(s, axis=0, keepdims=True)

    def softmax_pv(s_ref, mc, m, c):
        mn = jnp.maximum(m, mc)
        alpha = jnp.exp2(m - mn)
        p = jnp.exp2(s_ref[...] - mn).astype(BF16)
        start = pl.multiple_of(c * tk, tk)
        pv = jnp.dot(vt_ref[0, 0, :, pl.ds(start, tk)], p, preferred_element_type=F32)
        acc_ref[...] = acc_ref[...] * alpha + pv
        return mn

    def body(i, carry):
        m, mc = carry
        c0 = unroll * i
        for u in range(unroll):
            mc_next = qk(jnp.minimum(c0 + u + 1, n_chunks - 1), s_bufs[(u + 1) % 2])
            m = softmax_pv(s_bufs[u % 2], mc, m, c0 + u)
            mc = mc_next
        return m, mc

    mc_first = qk(0, s_bufs[0])
    lax.fori_loop(0, n_chunks // unroll, body, (jnp.full((1, 2 * tq), M_INIT, F32), mc_first))

    acc = acc_ref[...]
    o = acc[0:A_V_DIM, :] * (1.0 / acc[A_V_DIM:A_V_DIM + 1, :])
    w = o[:, 0:tq] - lam * o[:, tq:2 * tq]
    y = _norm_t(w, sg_ref[...]) * out_scale
    rows = slice(unit * tq, (unit + 1) * tq)
    o_ref[0, rows, :] = (y.T * gate_ref[0, rows, :]).astype(BF16)


def _attn_a(scal, qat, ka, vat, bias_tiles, sg_b, gate, tq, tk, out_scale):
    b, s, _ = ka.shape
    n_chunks = s // tk
    ratio = tk // tq
    unroll = next(u for u in (16, 8, 4, 2) if n_chunks % u == 0)
    units = UNITS_PER_STEP if n_chunks == unroll and (s // tq) % UNITS_PER_STEP == 0 else 1
    assert n_chunks % unroll == 0 and ratio * tq == tk and ratio & (ratio - 1) == 0
    kern = functools.partial(_attn_a_kernel, tq=tq, tk=tk, n_chunks=n_chunks, ratio=ratio, unroll=unroll,
                             units=units, out_scale=out_scale)
    return pl.pallas_call(
        kern,
        grid=(b, A_HEADS, s // (tq * units)),
        in_specs=[
            pl.BlockSpec(memory_space=pltpu.SMEM),
            pl.BlockSpec((1, 2 * A_HEAD_DIM, tq * units), lambda i, hh, j: (i, hh, j)),
            pl.BlockSpec((1, s, 2 * A_HEAD_DIM), lambda i, hh, j: (i, 0, hh)),
            pl.BlockSpec((1, 1, A_V_DIM + ONES_ROWS, s), lambda i, hh, j: (i, hh, 0, 0)),
            pl.BlockSpec((1, 3 * ratio + 2, tk, tq), lambda i, hh, j: (hh, 0, 0, 0)),
            pl.BlockSpec((A_V_DIM, tq), lambda i, hh, j: (0, 0)),
            pl.BlockSpec((1, tq * units, A_V_DIM), lambda i, hh, j: (i, j, hh)),
        ],
        out_specs=pl.BlockSpec((1, tq * units, A_V_DIM), lambda i, hh, j: (i, j, hh)),
        out_shape=jax.ShapeDtypeStruct((b, s, A_WIDTH), BF16),
        scratch_shapes=[
            pltpu.VMEM((2 * A_HEAD_DIM, 2 * tq), BF16),
            pltpu.VMEM((A_V_DIM + ONES_ROWS, 2 * tq), F32),
            pltpu.VMEM((tk, 2 * tq), F32),
            pltpu.VMEM((tk, 2 * tq), F32),
        ] * units,
        compiler_params=pltpu.CompilerParams(
            dimension_semantics=("arbitrary", "arbitrary", "arbitrary"), vmem_limit_bytes=VMEM_LIMIT_BYTES),
        name="attn_a",
    )(scal, qat, ka, vat, bias_tiles, sg_b, gate)


def _attn_b_kernel(qt_ref, k_ref, vt_ref, gate_ref, o_ref, *scratch, tq, tk, n_chunks, unroll, units):
    for unit in range(units):
        _attn_b_unit(qt_ref, k_ref, vt_ref, gate_ref, o_ref, *scratch[5 * unit:5 * unit + 5], unit,
                     tq=tq, tk=tk, n_chunks=n_chunks, unroll=unroll)


def _attn_b_unit(qt_ref, k_ref, vt_ref, gate_ref, o_ref, wq_ref, acc0_ref, acc1_ref, s0_ref, s1_ref, unit,
                 *, tq, tk, n_chunks, unroll):
    half = B_GROUP * tq
    lanes = B_HEADS * tq
    wq_ref[...] = jnp.zeros(wq_ref.shape, BF16)
    for hh in range(B_HEADS):
        n = hh // B_GROUP
        wq_ref[n * 64:(n + 1) * 64, hh * tq:(hh + 1) * tq] = qt_ref[0, hh * 64:(hh + 1) * 64,
                                                                    unit * tq:(unit + 1) * tq]
    acc0_ref[...] = jnp.zeros(acc0_ref.shape, F32)
    acc1_ref[...] = jnp.zeros(acc1_ref.shape, F32)
    s_bufs = (s0_ref, s1_ref)

    def qk(c, s_ref):
        start = pl.multiple_of(c * tk, tk)
        s = jnp.dot(k_ref[0, pl.ds(start, tk), :], wq_ref[...], preferred_element_type=F32)
        s_ref[...] = s
        return jnp.max(s, axis=0, keepdims=True)

    def softmax_pv(s_ref, mc, m, c):
        mn = jnp.maximum(m, mc)
        alpha = jnp.exp2(m - mn)
        p = jnp.exp2(s_ref[...] - mn).astype(BF16)
        start = pl.multiple_of(c * tk, tk)
        v0 = vt_ref[0, 0, :, pl.ds(start, tk)]
        v1 = vt_ref[0, 1, :, pl.ds(start, tk)]
        acc0_ref[...] = acc0_ref[...] * alpha[:, 0:half] + jnp.dot(v0, p[:, 0:half], preferred_element_type=F32)
        acc1_ref[...] = acc1_ref[...] * alpha[:, half:] + jnp.dot(v1, p[:, half:], preferred_element_type=F32)
        return mn

    def body(i, carry):
        m, mc = carry
        c0 = unroll * i
        for u in range(unroll):
            mc_next = qk(jnp.minimum(c0 + u + 1, n_chunks - 1), s_bufs[(u + 1) % 2])
            m = softmax_pv(s_bufs[u % 2], mc, m, c0 + u)
            mc = mc_next
        return m, mc

    mc_first = qk(0, s0_ref)
    lax.fori_loop(0, n_chunks // unroll, body, (jnp.full((1, lanes), M_INIT, F32), mc_first))

    parts = []
    for acc_ref in (acc0_ref, acc1_ref):
        acc = acc_ref[...]
        o = acc[0:64, :] * (1.0 / acc[64:65, :])
        for g in range(B_GROUP):
            parts.append(o[:, g * tq:(g + 1) * tq])
    rows = slice(unit * tq, (unit + 1) * tq)
    out = jnp.concatenate(parts, axis=0).T
    o_ref[0, rows, :] = (out * gate_ref[0, rows, :]).astype(BF16)


def _attn_b(qbt, kb, vbt, gate, tq, tk):
    b, s, _ = kb.shape
    n_chunks = s // tk
    unroll = next(u for u in (16, 8, 4, 2) if n_chunks % u == 0)
    units = UNITS_PER_STEP if n_chunks == unroll and (s // tq) % UNITS_PER_STEP == 0 else 1
    kern = functools.partial(_attn_b_kernel, tq=tq, tk=tk, n_chunks=n_chunks, unroll=unroll, units=units)
    return pl.pallas_call(
        kern,
        grid=(b, s // (tq * units)),
        in_specs=[
            pl.BlockSpec((1, B_WIDTH, tq * units), lambda i, j: (i, 0, j)),
            pl.BlockSpec((1, s, B_KV_HEADS * B_HEAD_DIM), lambda i, j: (i, 0, 0)),
            pl.BlockSpec((1, B_KV_HEADS, B_HEAD_DIM + ONES_ROWS, s), lambda i, j: (i, 0, 0, 0)),
            pl.BlockSpec((1, tq * units, B_WIDTH), lambda i, j: (i, j, 1)),
        ],
        out_specs=pl.BlockSpec((1, tq * units, B_WIDTH), lambda i, j: (i, j, 0)),
        out_shape=jax.ShapeDtypeStruct((b, s, B_WIDTH), BF16),
        scratch_shapes=[
            pltpu.VMEM((B_KV_HEADS * B_HEAD_DIM, B_HEADS * tq), BF16),
            pltpu.VMEM((B_HEAD_DIM + ONES_ROWS, B_GROUP * tq), F32),
            pltpu.VMEM((B_HEAD_DIM + ONES_ROWS, B_GROUP * tq), F32),
            pltpu.VMEM((tk, B_HEADS * tq), F32),
            pltpu.VMEM((tk, B_HEADS * tq), F32),
        ] * units,
        compiler_params=pltpu.CompilerParams(
            dimension_semantics=("arbitrary", "arbitrary"), vmem_limit_bytes=VMEM_LIMIT_BYTES),
        name="attn_b",
    )(qbt, kb, vbt, gate)


def _outproj_kernel(ma_ref, mb_ref, x_ref, mod_ref, w_ref, fg_ref, o_ref, *, final):
    y = (jnp.dot(ma_ref[0], w_ref[0:A_WIDTH, :], preferred_element_type=F32)
         + jnp.dot(mb_ref[0], w_ref[A_WIDTH:D_MODEL, :], preferred_element_type=F32))
    xn = x_ref[0] + mod_ref[0, 2:3, :] * y
    if final:
        xn = _rms_rows(xn, fg_ref[...])
    o_ref[0] = xn


def _outproj(ma, mb, x, mod, w, fg, tm, final):
    b, s, _ = x.shape
    tok = lambda i, j: (i, j, 0)
    return pl.pallas_call(
        functools.partial(_outproj_kernel, final=final),
        grid=(b, s // tm),
        in_specs=[
            pl.BlockSpec((1, tm, A_WIDTH), tok),
            pl.BlockSpec((1, tm, B_WIDTH), tok),
            pl.BlockSpec((1, tm, D_MODEL), tok),
            pl.BlockSpec((1, 3, D_MODEL), lambda i, j: (i, 0, 0)),
            pl.BlockSpec((D_MODEL, D_MODEL), lambda i, j: (0, 0)),
            pl.BlockSpec((1, D_MODEL), lambda i, j: (0, 0)),
        ],
        out_specs=pl.BlockSpec((1, tm, D_MODEL), tok),
        out_shape=jax.ShapeDtypeStruct((b, s, D_MODEL), F32),
        compiler_params=pltpu.CompilerParams(
            dimension_semantics=("arbitrary", "arbitrary"), vmem_limit_bytes=VMEM_LIMIT_BYTES),
        name="outproj",
    )(ma, mb, x, mod, w, fg)


def _t5_bucket(rel):
    half = NUM_BUCKETS // 2
    max_exact = half // 2
    ret = jnp.where(rel > 0, half, 0)
    n = jnp.abs(rel)
    nf = jnp.maximum(n, 1).astype(jnp.float32)
    large = max_exact + (jnp.log(nf / max_exact) / math.log(MAX_DISTANCE / max_exact)
                         * (half - max_exact)).astype(jnp.int32)
    large = jnp.minimum(large, half - 1)
    return ret + jnp.where(n < max_exact, n, large)


def _bias_tables(rel_table, s, tq, tk):
    ratio = tk // tq
    offsets = jnp.arange(-(s - 1), s, dtype=jnp.int32)
    boff = rel_table[_t5_bucket(offsets)].T.astype(F32) * LOG2E
    span = tk + tq - 1
    pad = tk + tq
    boff_p = jnp.pad(boff, ((0, 0), (pad, pad)), mode="edge")
    tiles = []
    for dc in (-1, 0, 1):
        for r in range(ratio):
            first = dc * tk - r * tq - (tq - 1) + (s - 1) + pad
            u = jnp.flip(boff_p[:, first:first + span], axis=1)
            hankel = jnp.tile(u, (1, tk + 1))[:, :tk * (span + 1)].reshape(A_HEADS, tk, span + 1)[:, :, :tq]
            tiles.append(jnp.flip(hankel, axis=1))
    assert tk >= MAX_DISTANCE and s - 1 >= MAX_DISTANCE
    for col in (0, 2 * s - 2):
        tiles.append(jnp.broadcast_to(boff[:, col][:, None, None], (A_HEADS, tk, tq)))
    return jnp.stack(tiles, axis=1)


def _rope_tables_t(s):
    rows = s // GRID_W
    row = jnp.repeat(jnp.arange(rows), GRID_W).astype(F32)
    col = jnp.tile(jnp.arange(GRID_W), rows).astype(F32)
    axis_dim = B_HEAD_DIM // 2
    inv_freq = ROPE_THETA ** (-jnp.arange(0, axis_dim, 2, dtype=F32) / axis_dim)
    ang_r = row[:, None] * inv_freq[None, :]
    ang_c = col[:, None] * inv_freq[None, :]
    ang = jnp.concatenate([ang_r, ang_r, ang_c, ang_c], axis=-1)
    return jnp.cos(ang).T, jnp.sin(ang).T


def _tile_sizes(s):
    tm = 512
    tk_a = 512
    return tm, (256, tk_a), (128, 256)


def _trunk(x, mod, lam, rel_table, norm_g, wn, wt, subln_g, q_norm_g, k_norm_g, w_out, final_g):
    b, s, _ = x.shape
    tm, (tq_a, tk_a), (tq_b, tk_b) = _tile_sizes(s)
    cos_t, sin_t = _rope_tables_t(s)
    bias_tiles = _bias_tables(rel_table, s, tq_a, tk_a)
    fg = final_g.reshape(1, D_MODEL)
    for l in range(DEPTH):
        lam_init = 0.8 - 0.6 * math.exp(-0.3 * l)
        qg_b = jnp.broadcast_to(q_norm_g[l][:, None], (B_HEAD_DIM, tm))
        kg_b = jnp.broadcast_to(k_norm_g[l][:, None], (B_HEAD_DIM, tm))
        sg_b = jnp.broadcast_to(subln_g[l][:, None], (A_V_DIM, tq_a))
        ka, gate, qat, vat, qbt, kb, vbt = _inproj(
            x, mod[l], norm_g[l].reshape(1, D_MODEL), wn[l], wt[l], cos_t, sin_t, qg_b, kg_b, tm)
        ma = _attn_a(lam[l:l + 1], qat, ka, vat, bias_tiles, sg_b, gate, tq_a, tk_a, 1.0 - lam_init)
        mb = _attn_b(qbt, kb, vbt, gate, tq_b, tk_b)
        x = _outproj(ma, mb, x, mod[l], w_out[l], fg, tm, final=(l == DEPTH - 1))
    return x


def kernel(x_prompt, x_sample, c_prompt, c_sample, rel_table, norm_g, w_ada, b_ada, w_in, lam_q1, lam_k1, lam_q2,
           lam_k2, subln_g, q_norm_g, k_norm_g, w_out, final_g):
    bp = x_prompt.shape[0]
    bs = x_sample.shape[0]
    rows = -(-(bp + bs) // 8) * 8
    c_all = jnp.concatenate([c_prompt, c_sample, jnp.zeros((rows - bp - bs, D_MODEL), F32)], axis=0)
    mod_all = _adaln_mod(c_all, w_ada.astype(BF16), b_ada)
    mod_p = mod_all[:, :bp].reshape(DEPTH, bp, 3, D_MODEL)
    mod_s = mod_all[:, bp:bp + bs].reshape(DEPTH, bs, 3, D_MODEL)
    lam = _lambdas(lam_q1, lam_k1, lam_q2, lam_k2)

    w_bf = w_in.astype(BF16)
    wn = jnp.concatenate([w_bf[:, :, _C_KA:_C_VA], w_bf[:, :, _C_GA:_C_QB], w_bf[:, :, _C_GB:D_IN]], axis=2)
    wt = jnp.concatenate([w_bf[:, :, _C_QA:_C_KA], w_bf[:, :, _C_VA:_C_GA], w_bf[:, :, _C_QB:_C_GB]], axis=2)
    wt = jnp.swapaxes(wt, 1, 2)
    w_out_bf = w_out.astype(BF16)

    args = (rel_table, norm_g, wn, wt, subln_g, q_norm_g, k_norm_g, w_out_bf, final_g)
    y_prompt = _trunk(x_prompt, mod_p, lam, *args)
    y_sample = _trunk(x_sample, mod_s, lam, *args)
    return (y_prompt, y_sample)
```

```python
import functools
import math

import jax
import jax.numpy as jnp
from jax import lax
from jax.experimental import pallas as pl
from jax.experimental.pallas import tpu as pltpu

F32 = jnp.float32
BF16 = jnp.bfloat16

D_MODEL = 1024
DEPTH = 4
GRID_W = 64
A_WIDTH = 512
A_V_DIM = 128
A_HEAD_DIM = 64
A_HEADS = 4
B_WIDTH = 512
B_HEAD_DIM = 64
B_HEADS = 8
B_KV_HEADS = 2
B_GROUP = 4
NUM_BUCKETS = 32
MAX_DISTANCE = 128
ROPE_THETA = 10000.0
EPS = 1e-6

LOG2E = 1.4426950408889634
Q_SCALE = (A_HEAD_DIM ** -0.5) * LOG2E
M_INIT = -1e30
ONES_ROWS = 16
VMEM_LIMIT_BYTES = 56 * 1024 * 1024
UNITS_PER_STEP = 4

_C_QA, _C_KA, _C_VA, _C_GA, _C_QB, _C_KB, _C_VB, _C_GB = 0, 512, 1024, 1536, 2048, 2560, 2688, 2816
D_IN = 3328
N_NAT = 1536
N_TR = 1792


def _rms_rows(x, g):
    ms = jnp.mean(x * x, axis=-1, keepdims=True)
    return x * lax.rsqrt(ms + EPS) * g


def _mod_kernel(c_ref, w_ref, b_ref, o_ref):
    c = c_ref[...]
    c_act = (c * jax.nn.sigmoid(c)).astype(BF16)
    o_ref[0] = jnp.dot(c_act, w_ref[0], preferred_element_type=F32) + b_ref[0]


def _adaln_mod(c_all, w_ada_bf, b_ada):
    rows = c_all.shape[0]
    return pl.pallas_call(
        _mod_kernel,
        grid=(DEPTH, 3),
        in_specs=[
            pl.BlockSpec((rows, D_MODEL), lambda l, j: (0, 0)),
            pl.BlockSpec((1, D_MODEL, D_MODEL), lambda l, j: (l, 0, j)),
            pl.BlockSpec((1, 1, D_MODEL), lambda l, j: (l, 0, j)),
        ],
        out_specs=pl.BlockSpec((1, rows, D_MODEL), lambda l, j: (l, 0, j)),
        out_shape=jax.ShapeDtypeStruct((DEPTH, rows, 3 * D_MODEL), F32),
        name="adaln_mod",
    )(c_all, w_ada_bf, b_ada.reshape(DEPTH, 1, 3 * D_MODEL))


def _lam_kernel(q1_ref, k1_ref, q2_ref, k2_ref, init_ref, o_ref):
    s1 = jnp.sum(q1_ref[...] * k1_ref[...], axis=-1, keepdims=True)
    s2 = jnp.sum(q2_ref[...] * k2_ref[...], axis=-1, keepdims=True)
    lam = jnp.exp(s1) - jnp.exp(s2) + init_ref[...][:, 0:1]
    o_ref[...] = jnp.broadcast_to(lam, o_ref.shape)


def _lambdas(lam_q1, lam_k1, lam_q2, lam_k2):
    init = jnp.asarray([[0.8 - 0.6 * math.exp(-0.3 * l)] * 128 for l in range(DEPTH)], F32)
    out = pl.pallas_call(
        _lam_kernel,
        out_shape=jax.ShapeDtypeStruct((DEPTH, 128), F32),
        name="diff_lambda",
    )(lam_q1, lam_k1, lam_q2, lam_k2, init)
    return out[:, 0]


def _rope_t(x, cos, sin):
    rot = jnp.concatenate([-x[16:32], x[0:16], -x[48:64], x[32:48]], axis=0)
    return x * cos + rot * sin


def _norm_t(x, g):
    ms = jnp.mean(x * x, axis=0, keepdims=True)
    return x * lax.rsqrt(ms + EPS) * g


def _inproj_kernel(x_ref, mod_ref, ng_ref, wn_ref, wt_ref, cos_ref, sin_ref, qg_ref, kg_ref,
                   ka_ref, gate_ref, qat_ref, vat_ref, qbt_ref, kb_ref, vbt_ref):
    tm = x_ref.shape[1]
    x = x_ref[0]
    shift = mod_ref[0, 0:1, :]
    scale = mod_ref[0, 1:2, :]
    h = (_rms_rows(x, ng_ref[...]) * (1.0 + scale) + shift).astype(BF16)

    pn = jnp.dot(h, wn_ref[...], preferred_element_type=F32)
    ka_ref[0] = pn[:, 0:512].astype(BF16)
    g = pn[:, 512:N_NAT]
    gate_ref[0] = g * jax.nn.sigmoid(g)

    pt = lax.dot_general(wt_ref[...], h, (((1,), (1,)), ((), ())), preferred_element_type=F32)
    qat_ref[0] = (pt[0:512] * Q_SCALE).astype(BF16)
    ones = jnp.ones((ONES_ROWS, tm), BF16)
    for hh in range(A_HEADS):
        vat_ref[0, hh, 0:A_V_DIM, :] = pt[512 + hh * A_V_DIM:512 + (hh + 1) * A_V_DIM].astype(BF16)
        vat_ref[0, hh, A_V_DIM:A_V_DIM + ONES_ROWS, :] = ones

    cos = cos_ref[...]
    sin = sin_ref[...]
    qg = qg_ref[...]
    for hh in range(B_HEADS):
        xh = pt[1024 + hh * 64:1024 + (hh + 1) * 64]
        qbt_ref[0, hh * 64:(hh + 1) * 64, :] = (_rope_t(_norm_t(xh, qg), cos, sin) * Q_SCALE).astype(BF16)
    kg = kg_ref[...]
    kparts = []
    for n in range(B_KV_HEADS):
        xh = pt[1536 + n * 64:1536 + (n + 1) * 64]
        kparts.append(_rope_t(_norm_t(xh, kg), cos, sin))
    kb_ref[0] = jnp.concatenate(kparts, axis=0).T.astype(BF16)
    for n in range(B_KV_HEADS):
        vbt_ref[0, n, 0:64, :] = pt[1664 + n * 64:1664 + (n + 1) * 64].astype(BF16)
        vbt_ref[0, n, 64:64 + ONES_ROWS, :] = ones


def _inproj(x, mod, ng, wn, wt, cos_t, sin_t, qg_b, kg_b, tm):
    b, s, _ = x.shape
    grid = (b, s // tm)
    const2 = lambda i, j: (0, 0)
    return pl.pallas_call(
        _inproj_kernel,
        grid=grid,
        in_specs=[
            pl.BlockSpec((1, tm, D_MODEL), lambda i, j: (i, j, 0)),
            pl.BlockSpec((1, 3, D_MODEL), lambda i, j: (i, 0, 0)),
            pl.BlockSpec((1, D_MODEL), const2),
            pl.BlockSpec((D_MODEL, N_NAT), const2),
            pl.BlockSpec((N_TR, D_MODEL), const2),
            pl.BlockSpec((B_HEAD_DIM, tm), lambda i, j: (0, j)),
            pl.BlockSpec((B_HEAD_DIM, tm), lambda i, j: (0, j)),
            pl.BlockSpec((B_HEAD_DIM, tm), const2),
            pl.BlockSpec((B_HEAD_DIM, tm), const2),
        ],
        out_specs=[
            pl.BlockSpec((1, tm, A_WIDTH), lambda i, j: (i, j, 0)),
            pl.BlockSpec((1, tm, D_MODEL), lambda i, j: (i, j, 0)),
            pl.BlockSpec((1, A_WIDTH, tm), lambda i, j: (i, 0, j)),
            pl.BlockSpec((1, A_HEADS, A_V_DIM + ONES_ROWS, tm), lambda i, j: (i, 0, 0, j)),
            pl.BlockSpec((1, B_WIDTH, tm), lambda i, j: (i, 0, j)),
            pl.BlockSpec((1, tm, B_KV_HEADS * B_HEAD_DIM), lambda i, j: (i, j, 0)),
            pl.BlockSpec((1, B_KV_HEADS, B_HEAD_DIM + ONES_ROWS, tm), lambda i, j: (i, 0, 0, j)),
        ],
        out_shape=[
            jax.ShapeDtypeStruct((b, s, A_WIDTH), BF16),
            jax.ShapeDtypeStruct((b, s, D_MODEL), F32),
            jax.ShapeDtypeStruct((b, A_WIDTH, s), BF16),
            jax.ShapeDtypeStruct((b, A_HEADS, A_V_DIM + ONES_ROWS, s), BF16),
            jax.ShapeDtypeStruct((b, B_WIDTH, s), BF16),
            jax.ShapeDtypeStruct((b, s, B_KV_HEADS * B_HEAD_DIM), BF16),
            jax.ShapeDtypeStruct((b, B_KV_HEADS, B_HEAD_DIM + ONES_ROWS, s), BF16),
        ],
        compiler_params=pltpu.CompilerParams(
            dimension_semantics=("arbitrary", "arbitrary"), vmem_limit_bytes=VMEM_LIMIT_BYTES),
        name="inproj",
    )(x, mod, ng, wn, wt, cos_t, sin_t, qg_b, kg_b)


def _attn_a_kernel(scal_ref, qt_ref, k_ref, vt_ref, bias_ref, sg_ref, gate_ref, o_ref, *scratch,
                   tq, tk, n_chunks, ratio, unroll, units, out_scale):
    for unit in range(units):
        wq_ref, acc_ref, s0_ref, s1_ref = scratch[4 * unit:4 * unit + 4]
        _attn_a_unit(scal_ref, qt_ref, k_ref, vt_ref, bias_ref, sg_ref, gate_ref, o_ref, wq_ref, acc_ref, s0_ref, s1_ref,
                     unit, pl.program_id(2) * units + unit, tq=tq, tk=tk, n_chunks=n_chunks, ratio=ratio,
                     unroll=unroll, out_scale=out_scale)


def _attn_a_unit(scal_ref, qt_ref, k_ref, vt_ref, bias_ref, sg_ref, gate_ref, o_ref, wq_ref, acc_ref, s0_ref, s1_ref,
                 unit, qi, *, tq, tk, n_chunks, ratio, unroll, out_scale):
    s_bufs = (s0_ref, s1_ref)
    q = qt_ref[0, :, unit * tq:(unit + 1) * tq]
    zero = jnp.zeros((A_HEAD_DIM, tq), BF16)
    wq_ref[0:64, 0:tq] = q[0:64]
    wq_ref[0:64, tq:2 * tq] = zero
    wq_ref[64:128, 0:tq] = zero
    wq_ref[64:128, tq:2 * tq] = q[64:128]
    acc_ref[...] = jnp.zeros(acc_ref.shape, F32)

    lam = scal_ref[0]
    log_ratio = ratio.bit_length() - 1
    c_own = lax.shift_right_logical(qi, log_ratio)
    r = lax.bitwise_and(qi, ratio - 1)

    def qk(c, s_ref):
        start = pl.multiple_of(c * tk, tk)
        d = c - c_own
        tile = jnp.where(d < -1, 3 * ratio, jnp.where(d > 1, 3 * ratio + 1, (d + 1) * ratio + r))
        t = bias_ref[0, tile]
        s = jnp.dot(k_ref[0, pl.ds(start, tk), :], wq_ref[...], preferred_element_type=F32)
        s = s + jnp.concatenate([t, t], axis=1)
        s_ref[...] = s
        return jnp.max(s, axis=0, keepdims=True)

    def softmax_pv(s_ref, mc, m, c):
        mn = jnp.maximum(m, mc)
        alpha = jnp.exp2(m - mn)
        p = jnp.exp2(s_ref[...] - mn).astype(BF16)
        start = pl.multiple_of(c * tk, tk)
        pv = jnp.dot(vt_ref[0, 0, :, pl.ds(start, tk)], p, preferred_element_type=F32)
        acc_ref[...] = acc_ref[...] * alpha + pv
        return mn

    def body(i, carry):
        m, mc = carry
        c0 = unroll * i
        for u in range(unroll):
            mc_next = qk(jnp.minimum(c0 + u + 1, n_chunks - 1), s_bufs[(u + 1) % 2])
            m = softmax_pv(s_bufs[u % 2], mc, m, c0 + u)
            mc = mc_next
        return m, mc

    mc_first = qk(0, s_bufs[0])
    lax.fori_loop(0, n_chunks // unroll, body, (jnp.full((1, 2 * tq), M_INIT, F32), mc_first))

    acc = acc_ref[...]
    o = acc[0:A_V_DIM, :] * (1.0 / acc[A_V_DIM:A_V_DIM + 1, :])
    w = o[:, 0:tq] - lam * o[:, tq:2 * tq]
    y = _norm_t(w, sg_ref[...]) * out_scale
    rows = slice(unit * tq, (unit + 1) * tq)
    o_ref[0, rows, :] = (y.T * gate_ref[0, rows, :]).astype(BF16)


def _attn_a(scal, qat, ka, vat, bias_tiles, sg_b, gate, tq, tk, out_scale):
    b, s, _ = ka.shape
    n_chunks = s // tk
    ratio = tk // tq
    unroll = next(u for u in (16, 8, 4, 2) if n_chunks % u == 0)
    units = UNITS_PER_STEP if n_chunks == unroll and (s // tq) % UNITS_PER_STEP == 0 else 1
    assert n_chunks % unroll == 0 and ratio * tq == tk and ratio & (ratio - 1) == 0
    kern = functools.partial(_attn_a_kernel, tq=tq, tk=tk, n_chunks=n_chunks, ratio=ratio, unroll=unroll,
                             units=units, out_scale=out_scale)
    return pl.pallas_call(
        kern,
        grid=(b, A_HEADS, s // (tq * units)),
        in_specs=[
            pl.BlockSpec(memory_space=pltpu.SMEM),
            pl.BlockSpec((1, 2 * A_HEAD_DIM, tq * units), lambda i, hh, j: (i, hh, j)),
            pl.BlockSpec((1, s, 2 * A_HEAD_DIM), lambda i, hh, j: (i, 0, hh)),
            pl.BlockSpec((1, 1, A_V_DIM + ONES_ROWS, s), lambda i, hh, j: (i, hh, 0, 0)),
            pl.BlockSpec((1, 3 * ratio + 2, tk, tq), lambda i, hh, j: (hh, 0, 0, 0)),
            pl.BlockSpec((A_V_DIM, tq), lambda i, hh, j: (0, 0)),
            pl.BlockSpec((1, tq * units, A_V_DIM), lambda i, hh, j: (i, j, hh)),
        ],
        out_specs=pl.BlockSpec((1, tq * units, A_V_DIM), lambda i, hh, j: (i, j, hh)),
        out_shape=jax.ShapeDtypeStruct((b, s, A_WIDTH), BF16),
        scratch_shapes=[
            pltpu.VMEM((2 * A_HEAD_DIM, 2 * tq), BF16),
            pltpu.VMEM((A_V_DIM + ONES_ROWS, 2 * tq), F32),
            pltpu.VMEM((tk, 2 * tq), F32),
            pltpu.VMEM((tk, 2 * tq), F32),
        ] * units,
        compiler_params=pltpu.CompilerParams(
            dimension_semantics=("arbitrary", "arbitrary", "arbitrary"), vmem_limit_bytes=VMEM_LIMIT_BYTES),
        name="attn_a",
    )(scal, qat, ka, vat, bias_tiles, sg_b, gate)


def _attn_b_kernel(qt_ref, k_ref, vt_ref, gate_ref, o_ref, *scratch, tq, tk, n_chunks, unroll, units):
    for unit in range(units):
        _attn_b_unit(qt_ref, k_ref, vt_ref, gate_ref, o_ref, *scratch[5 * unit:5 * unit + 5], unit,
                     tq=tq, tk=tk, n_chunks=n_chunks, unroll=unroll)


def _attn_b_unit(qt_ref, k_ref, vt_ref, gate_ref, o_ref, wq_ref, acc0_ref, acc1_ref, s0_ref, s1_ref, unit,
                 *, tq, tk, n_chunks, unroll):
    half = B_GROUP * tq
    lanes = B_HEADS * tq
    wq_ref[...] = jnp.zeros(wq_ref.shape, BF16)
    for hh in range(B_HEADS):
        n = hh // B_GROUP
        wq_ref[n * 64:(n + 1) * 64, hh * tq:(hh + 1) * tq] = qt_ref[0, hh * 64:(hh + 1) * 64,
                                                                    unit * tq:(unit + 1) * tq]
    acc0_ref[...] = jnp.zeros(acc0_ref.shape, F32)
    acc1_ref[...] = jnp.zeros(acc1_ref.shape, F32)
    s_bufs = (s0_ref, s1_ref)

    def qk(c, s_ref):
        start = pl.multiple_of(c * tk, tk)
        s = jnp.dot(k_ref[0, pl.ds(start, tk), :], wq_ref[...], preferred_element_type=F32)
        s_ref[...] = s
        return jnp.max(s, axis=0, keepdims=True)

    def softmax_pv(s_ref, mc, m, c):
        mn = jnp.maximum(m, mc)
        alpha = jnp.exp2(m - mn)
        p = jnp.exp2(s_ref[...] - mn).astype(BF16)
        start = pl.multiple_of(c * tk, tk)
        v0 = vt_ref[0, 0, :, pl.ds(start, tk)]
        v1 = vt_ref[0, 1, :, pl.ds(start, tk)]
        acc0_ref[...] = acc0_ref[...] * alpha[:, 0:half] + jnp.dot(v0, p[:, 0:half], preferred_element_type=F32)
        acc1_ref[...] = acc1_ref[...] * alpha[:, half:] + jnp.dot(v1, p[:, half:], preferred_element_type=F32)
        return mn

    def body(i, carry):
        m, mc = carry
        c0 = unroll * i
        for u in range(unroll):
            mc_next = qk(jnp.minimum(c0 + u + 1, n_chunks - 1), s_bufs[(u + 1) % 2])
            m = softmax_pv(s_bufs[u % 2], mc, m, c0 + u)
            mc = mc_next
        return m, mc

    mc_first = qk(0, s0_ref)
    lax.fori_loop(0, n_chunks // unroll, body, (jnp.full((1, lanes), M_INIT, F32), mc_first))

    parts = []
    for acc_ref in (acc0_ref, acc1_ref):
        acc = acc_ref[...]
        o = acc[0:64, :] * (1.0 / acc[64:65, :])
        for g in range(B_GROUP):
            parts.append(o[:, g * tq:(g + 1) * tq])
    rows = slice(unit * tq, (unit + 1) * tq)
    out = jnp.concatenate(parts, axis=0).T
    o_ref[0, rows, :] = (out * gate_ref[0, rows, :]).astype(BF16)


def _attn_b(qbt, kb, vbt, gate, tq, tk):
    b, s, _ = kb.shape
    n_chunks = s // tk
    unroll = next(u for u in (16, 8, 4, 2) if n_chunks % u == 0)
    units = UNITS_PER_STEP if n_chunks == unroll and (s // tq) % UNITS_PER_STEP == 0 else 1
    kern = functools.partial(_attn_b_kernel, tq=tq, tk=tk, n_chunks=n_chunks, unroll=unroll, units=units)
    return pl.pallas_call(
        kern,
        grid=(b, s // (tq * units)),
        in_specs=[
            pl.BlockSpec((1, B_WIDTH, tq * units), lambda i, j: (i, 0, j)),
            pl.BlockSpec((1, s, B_KV_HEADS * B_HEAD_DIM), lambda i, j: (i, 0, 0)),
            pl.BlockSpec((1, B_KV_HEADS, B_HEAD_DIM + ONES_ROWS, s), lambda i, j: (i, 0, 0, 0)),
            pl.BlockSpec((1, tq * units, B_WIDTH), lambda i, j: (i, j, 1)),
        ],
        out_specs=pl.BlockSpec((1, tq * units, B_WIDTH), lambda i, j: (i, j, 0)),
        out_shape=jax.ShapeDtypeStruct((b, s, B_WIDTH), BF16),
        scratch_shapes=[
            pltpu.VMEM((B_KV_HEADS * B_HEAD_DIM, B_HEADS * tq), BF16),
            pltpu.VMEM((B_HEAD_DIM + ONES_ROWS, B_GROUP * tq), F32),
            pltpu.VMEM((B_HEAD_DIM + ONES_ROWS, B_GROUP * tq), F32),
            pltpu.VMEM((tk, B_HEADS * tq), F32),
            pltpu.VMEM((tk, B_HEADS * tq), F32),
        ] * units,
        compiler_params=pltpu.CompilerParams(
            dimension_semantics=("arbitrary", "arbitrary"), vmem_limit_bytes=VMEM_LIMIT_BYTES),
        name="attn_b",
    )(qbt, kb, vbt, gate)


def _outproj_kernel(ma_ref, mb_ref, x_ref, mod_ref, w_ref, fg_ref, o_ref, *, final):
    y = (jnp.dot(ma_ref[0], w_ref[0:A_WIDTH, :], preferred_element_type=F32)
         + jnp.dot(mb_ref[0], w_ref[A_WIDTH:D_MODEL, :], preferred_element_type=F32))
    xn = x_ref[0] + mod_ref[0, 2:3, :] * y
    if final:
        xn = _rms_rows(xn, fg_ref[...])
    o_ref[0] = xn


def _outproj(ma, mb, x, mod, w, fg, tm, final):
    b, s, _ = x.shape
    tok = lambda i, j: (i, j, 0)
    return pl.pallas_call(
        functools.partial(_outproj_kernel, final=final),
        grid=(b, s // tm),
        in_specs=[
            pl.BlockSpec((1, tm, A_WIDTH), tok),
            pl.BlockSpec((1, tm, B_WIDTH), tok),
            pl.BlockSpec((1, tm, D_MODEL), tok),
            pl.BlockSpec((1, 3, D_MODEL), lambda i, j: (i, 0, 0)),
            pl.BlockSpec((D_MODEL, D_MODEL), lambda i, j: (0, 0)),
            pl.BlockSpec((1, D_MODEL), lambda i, j: (0, 0)),
        ],
        out_specs=pl.BlockSpec((1, tm, D_MODEL), tok),
        out_shape=jax.ShapeDtypeStruct((b, s, D_MODEL), F32),
        compiler_params=pltpu.CompilerParams(
            dimension_semantics=("arbitrary", "arbitrary"), vmem_limit_bytes=VMEM_LIMIT_BYTES),
        name="outproj",
    )(ma, mb, x, mod, w, fg)


def _t5_bucket(rel):
    half = NUM_BUCKETS // 2
    max_exact = half // 2
    ret = jnp.where(rel > 0, half, 0)
    n = jnp.abs(rel)
    nf = jnp.maximum(n, 1).astype(jnp.float32)
    large = max_exact + (jnp.log(nf / max_exact) / math.log(MAX_DISTANCE / max_exact)
                         * (half - max_exact)).astype(jnp.int32)
    large = jnp.minimum(large, half - 1)
    return ret + jnp.where(n < max_exact, n, large)


def _bias_tables(rel_table, s, tq, tk):
    ratio = tk // tq
    assert tk >= MAX_DISTANCE and s - 1 >= MAX_DISTANCE
    offsets = jnp.arange(-MAX_DISTANCE, MAX_DISTANCE + 1, dtype=jnp.int32)
    core = rel_table[_t5_bucket(offsets)].T.astype(F32) * LOG2E
    reach = 2 * tk + tq
    boff_p = jnp.pad(core, ((0, 0), (reach - MAX_DISTANCE, reach - MAX_DISTANCE)), mode="edge")
    span = tk + tq - 1
    tiles = []
    for dc in (-1, 0, 1):
        for r in range(ratio):
            first = dc * tk - r * tq - (tq - 1) + reach
            u = jnp.flip(boff_p[:, first:first + span], axis=1)
            hankel = jnp.tile(u, (1, tk + 1))[:, :tk * (span + 1)].reshape(A_HEADS, tk, span + 1)[:, :, :tq]
            tiles.append(jnp.flip(hankel, axis=1))
    for col in (0, 2 * reach):
        tiles.append(jnp.broadcast_to(boff_p[:, col][:, None, None], (A_HEADS, tk, tq)))
    return jnp.stack(tiles, axis=1)


def _rope_tables_t(s):
    rows = s // GRID_W
    row = jnp.repeat(jnp.arange(rows), GRID_W).astype(F32)
    col = jnp.tile(jnp.arange(GRID_W), rows).astype(F32)
    axis_dim = B_HEAD_DIM // 2
    inv_freq = ROPE_THETA ** (-jnp.arange(0, axis_dim, 2, dtype=F32) / axis_dim)
    ang_r = row[:, None] * inv_freq[None, :]
    ang_c = col[:, None] * inv_freq[None, :]
    ang = jnp.concatenate([ang_r, ang_r, ang_c, ang_c], axis=-1)
    return jnp.cos(ang).T, jnp.sin(ang).T


def _tile_sizes(s):
    del s
    return 512, 1024, (256, 512), (128, 256)


def _trunk(x, mod, lam, rel_table, norm_g, wn, wt, subln_g, q_norm_g, k_norm_g, w_out, final_g):
    b, s, _ = x.shape
    tm, tm_out, (tq_a, tk_a), (tq_b, tk_b) = _tile_sizes(s)
    cos_t, sin_t = _rope_tables_t(s)
    bias_tiles = _bias_tables(rel_table, s, tq_a, tk_a)
    fg = final_g.reshape(1, D_MODEL)
    for l in range(DEPTH):
        lam_init = 0.8 - 0.6 * math.exp(-0.3 * l)
        qg_b = jnp.broadcast_to(q_norm_g[l][:, None], (B_HEAD_DIM, tm))
        kg_b = jnp.broadcast_to(k_norm_g[l][:, None], (B_HEAD_DIM, tm))
        sg_b = jnp.broadcast_to(subln_g[l][:, None], (A_V_DIM, tq_a))
        ka, gate, qat, vat, qbt, kb, vbt = _inproj(
            x, mod[l], norm_g[l].reshape(1, D_MODEL), wn[l], wt[l], cos_t, sin_t, qg_b, kg_b, tm)
        ma = _attn_a(lam[l:l + 1], qat, ka, vat, bias_tiles, sg_b, gate, tq_a, tk_a, 1.0 - lam_init)
        mb = _attn_b(qbt, kb, vbt, gate, tq_b, tk_b)
        x = _outproj(ma, mb, x, mod[l], w_out[l], fg, tm_out, final=(l == DEPTH - 1))
    return x


def kernel(x_prompt, x_sample, c_prompt, c_sample, rel_table, norm_g, w_ada, b_ada, w_in, lam_q1, lam_k1, lam_q2,
           lam_k2, subln_g, q_norm_g, k_norm_g, w_out, final_g):
    bp = x_prompt.shape[0]
    bs = x_sample.shape[0]
    rows = -(-(bp + bs) // 8) * 8
    c_all = jnp.concatenate([c_prompt, c_sample, jnp.zeros((rows - bp - bs, D_MODEL), F32)], axis=0)
    mod_all = _adaln_mod(c_all, w_ada.astype(BF16), b_ada)
    mod_p = mod_all[:, :bp].reshape(DEPTH, bp, 3, D_MODEL)
    mod_s = mod_all[:, bp:bp + bs].reshape(DEPTH, bs, 3, D_MODEL)
    lam = _lambdas(lam_q1, lam_k1, lam_q2, lam_k2)

    w_bf = w_in.astype(BF16)
    wn = jnp.concatenate([w_bf[:, :, _C_KA:_C_VA], w_bf[:, :, _C_GA:_C_QB], w_bf[:, :, _C_GB:D_IN]], axis=2)
    wt = jnp.concatenate([w_bf[:, :, _C_QA:_C_KA], w_bf[:, :, _C_VA:_C_GA], w_bf[:, :, _C_QB:_C_GB]], axis=2)
    wt = jnp.swapaxes(wt, 1, 2)
    w_out_bf = w_out.astype(BF16)

    args = (rel_table, norm_g, wn, wt, subln_g, q_norm_g, k_norm_g, w_out_bf, final_g)
    y_prompt = _trunk(x_prompt, mod_p, lam, *args)
    y_sample = _trunk(x_sample, mod_s, lam, *args)
    return (y_prompt, y_sample)
```

```python
import functools
import math

import jax
import jax.numpy as jnp
from jax import lax
from jax.experimental import pallas as pl
from jax.experimental.pallas import tpu as pltpu

F32 = jnp.float32
BF16 = jnp.bfloat16

D_MODEL = 1024
DEPTH = 4
GRID_W = 64
A_WIDTH = 512
A_V_DIM = 128
A_HEAD_DIM = 64
A_HEADS = 4
B_WIDTH = 512
B_HEAD_DIM = 64
B_HEADS = 8
B_KV_HEADS = 2
B_GROUP = 4
NUM_BUCKETS = 32
MAX_DISTANCE = 128
ROPE_THETA = 10000.0
EPS = 1e-6

LOG2E = 1.4426950408889634
Q_SCALE = (A_HEAD_DIM ** -0.5) * LOG2E
M_INIT = -1e30
ONES_ROWS = 16
VMEM_LIMIT_BYTES = 56 * 1024 * 1024
UNITS_PER_STEP = 4
LOOP_BODY_CHUNKS = 16

_C_QA, _C_KA, _C_VA, _C_GA, _C_QB, _C_KB, _C_VB, _C_GB = 0, 512, 1024, 1536, 2048, 2560, 2688, 2816
D_IN = 3328
N_NAT = 1536
N_TR = 1792


def _rms_rows(x, g):
    ms = jnp.mean(x * x, axis=-1, keepdims=True)
    return x * lax.rsqrt(ms + EPS) * g


def _mod_kernel(c_ref, w_ref, b_ref, o_ref):
    c = c_ref[...]
    c_act = (c * jax.nn.sigmoid(c)).astype(BF16)
    o_ref[0] = jnp.dot(c_act, w_ref[0], preferred_element_type=F32) + b_ref[0]


def _adaln_mod(c_all, w_ada_bf, b_ada):
    rows = c_all.shape[0]
    return pl.pallas_call(
        _mod_kernel,
        grid=(DEPTH, 3),
        in_specs=[
            pl.BlockSpec((rows, D_MODEL), lambda l, j: (0, 0)),
            pl.BlockSpec((1, D_MODEL, D_MODEL), lambda l, j: (l, 0, j)),
            pl.BlockSpec((1, 1, D_MODEL), lambda l, j: (l, 0, j)),
        ],
        out_specs=pl.BlockSpec((1, rows, D_MODEL), lambda l, j: (l, 0, j)),
        out_shape=jax.ShapeDtypeStruct((DEPTH, rows, 3 * D_MODEL), F32),
        name="adaln_mod",
    )(c_all, w_ada_bf, b_ada.reshape(DEPTH, 1, 3 * D_MODEL))


def _lam_kernel(q1_ref, k1_ref, q2_ref, k2_ref, init_ref, o_ref):
    s1 = jnp.sum(q1_ref[...] * k1_ref[...], axis=-1, keepdims=True)
    s2 = jnp.sum(q2_ref[...] * k2_ref[...], axis=-1, keepdims=True)
    lam = jnp.exp(s1) - jnp.exp(s2) + init_ref[...][:, 0:1]
    o_ref[...] = jnp.broadcast_to(lam, o_ref.shape)


def _lambdas(lam_q1, lam_k1, lam_q2, lam_k2):
    init = jnp.asarray([[0.8 - 0.6 * math.exp(-0.3 * l)] * 128 for l in range(DEPTH)], F32)
    out = pl.pallas_call(
        _lam_kernel,
        out_shape=jax.ShapeDtypeStruct((DEPTH, 128), F32),
        name="diff_lambda",
    )(lam_q1, lam_k1, lam_q2, lam_k2, init)
    return out[:, 0]


def _rope_t(x, cos, sin):
    rot = jnp.concatenate([-x[16:32], x[0:16], -x[48:64], x[32:48]], axis=0)
    return x * cos + rot * sin


def _norm_t(x, g):
    ms = jnp.mean(x * x, axis=0, keepdims=True)
    return x * lax.rsqrt(ms + EPS) * g


def _inproj_kernel(x_ref, mod_ref, ng_ref, wn_ref, wt_ref, cos_ref, sin_ref, qg_ref, kg_ref,
                   ka_ref, gate_ref, qat_ref, vat_ref, qbt_ref, kb_ref, vbt_ref):
    tm = x_ref.shape[1]
    x = x_ref[0]
    shift = mod_ref[0, 0:1, :]
    scale = mod_ref[0, 1:2, :]
    h = (_rms_rows(x, ng_ref[...]) * (1.0 + scale) + shift).astype(BF16)

    pn = jnp.dot(h, wn_ref[...], preferred_element_type=F32)
    ka_ref[0] = pn[:, 0:512].astype(BF16)
    g = pn[:, 512:N_NAT]
    gate_ref[0] = g * jax.nn.sigmoid(g)

    pt = lax.dot_general(wt_ref[...], h, (((1,), (1,)), ((), ())), preferred_element_type=F32)
    qat_ref[0] = (pt[0:512] * Q_SCALE).astype(BF16)
    ones = jnp.ones((ONES_ROWS, tm), BF16)
    for hh in range(A_HEADS):
        vat_ref[0, hh, 0:A_V_DIM, :] = pt[512 + hh * A_V_DIM:512 + (hh + 1) * A_V_DIM].astype(BF16)
        vat_ref[0, hh, A_V_DIM:A_V_DIM + ONES_ROWS, :] = ones

    cos = cos_ref[...]
    sin = sin_ref[...]
    qg = qg_ref[...]
    for hh in range(B_HEADS):
        xh = pt[1024 + hh * 64:1024 + (hh + 1) * 64]
        qbt_ref[0, hh * 64:(hh + 1) * 64, :] = (_rope_t(_norm_t(xh, qg), cos, sin) * Q_SCALE).astype(BF16)
    kg = kg_ref[...]
    kparts = []
    for n in range(B_KV_HEADS):
        xh = pt[1536 + n * 64:1536 + (n + 1) * 64]
        kparts.append(_rope_t(_norm_t(xh, kg), cos, sin))
    kb_ref[0] = jnp.concatenate(kparts, axis=0).T.astype(BF16)
    for n in range(B_KV_HEADS):
        vbt_ref[0, n, 0:64, :] = pt[1664 + n * 64:1664 + (n + 1) * 64].astype(BF16)
        vbt_ref[0, n, 64:64 + ONES_ROWS, :] = ones


def _inproj(x, mod, ng, wn, wt, cos_t, sin_t, qg_b, kg_b, tm):
    b, s, _ = x.shape
    grid = (b, s // tm)
    const2 = lambda i, j: (0, 0)
    return pl.pallas_call(
        _inproj_kernel,
        grid=grid,
        in_specs=[
            pl.BlockSpec((1, tm, D_MODEL), lambda i, j: (i, j, 0)),
            pl.BlockSpec((1, 3, D_MODEL), lambda i, j: (i, 0, 0)),
            pl.BlockSpec((1, D_MODEL), const2),
            pl.BlockSpec((D_MODEL, N_NAT), const2),
            pl.BlockSpec((N_TR, D_MODEL), const2),
            pl.BlockSpec((B_HEAD_DIM, tm), lambda i, j: (0, j)),
            pl.BlockSpec((B_HEAD_DIM, tm), lambda i, j: (0, j)),
            pl.BlockSpec((B_HEAD_DIM, tm), const2),
            pl.BlockSpec((B_HEAD_DIM, tm), const2),
        ],
        out_specs=[
            pl.BlockSpec((1, tm, A_WIDTH), lambda i, j: (i, j, 0)),
            pl.BlockSpec((1, tm, D_MODEL), lambda i, j: (i, j, 0)),
            pl.BlockSpec((1, A_WIDTH, tm), lambda i, j: (i, 0, j)),
            pl.BlockSpec((1, A_HEADS, A_V_DIM + ONES_ROWS, tm), lambda i, j: (i, 0, 0, j)),
            pl.BlockSpec((1, B_WIDTH, tm), lambda i, j: (i, 0, j)),
            pl.BlockSpec((1, tm, B_KV_HEADS * B_HEAD_DIM), lambda i, j: (i, j, 0)),
            pl.BlockSpec((1, B_KV_HEADS, B_HEAD_DIM + ONES_ROWS, tm), lambda i, j: (i, 0, 0, j)),
        ],
        out_shape=[
            jax.ShapeDtypeStruct((b, s, A_WIDTH), BF16),
            jax.ShapeDtypeStruct((b, s, D_MODEL), F32),
            jax.ShapeDtypeStruct((b, A_WIDTH, s), BF16),
            jax.ShapeDtypeStruct((b, A_HEADS, A_V_DIM + ONES_ROWS, s), BF16),
            jax.ShapeDtypeStruct((b, B_WIDTH, s), BF16),
            jax.ShapeDtypeStruct((b, s, B_KV_HEADS * B_HEAD_DIM), BF16),
            jax.ShapeDtypeStruct((b, B_KV_HEADS, B_HEAD_DIM + ONES_ROWS, s), BF16),
        ],
        compiler_params=pltpu.CompilerParams(
            dimension_semantics=("arbitrary", "arbitrary"), vmem_limit_bytes=VMEM_LIMIT_BYTES),
        name="inproj",
    )(x, mod, ng, wn, wt, cos_t, sin_t, qg_b, kg_b)


def _loop_shape(n_chunks, n_tiles, loop_units):
    if n_chunks <= LOOP_BODY_CHUNKS // 2:
        unroll, units = n_chunks, UNITS_PER_STEP
    else:
        unroll, units = LOOP_BODY_CHUNKS // loop_units, loop_units
    assert n_chunks % unroll == 0 and unroll % 2 == 0
    while n_tiles % units:
        units //= 2
    return unroll, units


def _attn_a_kernel(scal_ref, qt_ref, k_ref, vt_ref, bias_ref, sg_ref, gate_ref, o_ref, *scratch,
                   tq, tk, n_chunks, ratio, unroll, units, out_scale):
    lam = scal_ref[0]
    log_ratio = ratio.bit_length() - 1
    wq_refs = [scratch[4 * unit] for unit in range(units)]
    acc_refs = [scratch[4 * unit + 1] for unit in range(units)]
    s_bufs = [scratch[4 * unit + 2:4 * unit + 4] for unit in range(units)]
    tiles = [pl.program_id(2) * units + unit for unit in range(units)]

    def qk(unit, c, s_ref):
        start = pl.multiple_of(c * tk, tk)
        d = c - lax.shift_right_logical(tiles[unit], log_ratio)
        r = lax.bitwise_and(tiles[unit], ratio - 1)
        tile = jnp.where(d < -1, 3 * ratio, jnp.where(d > 1, 3 * ratio + 1, (d + 1) * ratio + r))
        t = bias_ref[0, tile]
        s = jnp.dot(k_ref[0, pl.ds(start, tk), :], wq_refs[unit][...], preferred_element_type=F32)
        s = s + jnp.concatenate([t, t], axis=1)
        s_ref[...] = s
        return jnp.max(s, axis=0, keepdims=True)

    def softmax_pv(unit, s_ref, mc, m, c):
        mn = jnp.maximum(m, mc)
        alpha = jnp.exp2(m - mn)
        p = jnp.exp2(s_ref[...] - mn).astype(BF16)
        start = pl.multiple_of(c * tk, tk)
        pv = jnp.dot(vt_ref[0, 0, :, pl.ds(start, tk)], p, preferred_element_type=F32)
        acc_refs[unit][...] = acc_refs[unit][...] * alpha + pv
        return mn

    carry = []
    for unit in range(units):
        q = qt_ref[0, :, unit * tq:(unit + 1) * tq]
        zero = jnp.zeros((A_HEAD_DIM, tq), BF16)
        wq_refs[unit][0:64, 0:tq] = q[0:64]
        wq_refs[unit][0:64, tq:2 * tq] = zero
        wq_refs[unit][64:128, 0:tq] = zero
        wq_refs[unit][64:128, tq:2 * tq] = q[64:128]
        acc_refs[unit][...] = jnp.zeros(acc_refs[unit].shape, F32)
        carry.append((jnp.full((1, 2 * tq), M_INIT, F32), qk(unit, 0, s_bufs[unit][0])))

    def body(i, carry):
        carry = list(carry)
        c0 = unroll * i
        for u in range(unroll):
            for unit in range(units):
                m, mc = carry[unit]
                mc_next = qk(unit, jnp.minimum(c0 + u + 1, n_chunks - 1), s_bufs[unit][(u + 1) % 2])
                m = softmax_pv(unit, s_bufs[unit][u % 2], mc, m, c0 + u)
                carry[unit] = (m, mc_next)
        return tuple(carry)

    lax.fori_loop(0, n_chunks // unroll, body, tuple(carry))

    for unit in range(units):
        acc = acc_refs[unit][...]
        o = acc[0:A_V_DIM, :] * (1.0 / acc[A_V_DIM:A_V_DIM + 1, :])
        w = o[:, 0:tq] - lam * o[:, tq:2 * tq]
        y = _norm_t(w, sg_ref[...]) * out_scale
        rows = slice(unit * tq, (unit + 1) * tq)
        o_ref[0, rows, :] = (y.T * gate_ref[0, rows, :]).astype(BF16)


def _attn_a(scal, qat, ka, vat, bias_tiles, sg_b, gate, tq, tk, out_scale):
    b, s, _ = ka.shape
    n_chunks = s // tk
    ratio = tk // tq
    unroll, units = _loop_shape(n_chunks, s // tq, loop_units=2)
    assert n_chunks % unroll == 0 and ratio * tq == tk and ratio & (ratio - 1) == 0
    kern = functools.partial(_attn_a_kernel, tq=tq, tk=tk, n_chunks=n_chunks, ratio=ratio, unroll=unroll,
                             units=units, out_scale=out_scale)
    return pl.pallas_call(
        kern,
        grid=(b, A_HEADS, s // (tq * units)),
        in_specs=[
            pl.BlockSpec(memory_space=pltpu.SMEM),
            pl.BlockSpec((1, 2 * A_HEAD_DIM, tq * units), lambda i, hh, j: (i, hh, j)),
            pl.BlockSpec((1, s, 2 * A_HEAD_DIM), lambda i, hh, j: (i, 0, hh)),
            pl.BlockSpec((1, 1, A_V_DIM + ONES_ROWS, s), lambda i, hh, j: (i, hh, 0, 0)),
            pl.BlockSpec((1, 3 * ratio + 2, tk, tq), lambda i, hh, j: (hh, 0, 0, 0)),
            pl.BlockSpec((A_V_DIM, tq), lambda i, hh, j: (0, 0)),
            pl.BlockSpec((1, tq * units, A_V_DIM), lambda i, hh, j: (i, j, hh)),
        ],
        out_specs=pl.BlockSpec((1, tq * units, A_V_DIM), lambda i, hh, j: (i, j, hh)),
        out_shape=jax.ShapeDtypeStruct((b, s, A_WIDTH), BF16),
        scratch_shapes=[
            pltpu.VMEM((2 * A_HEAD_DIM, 2 * tq), BF16),
            pltpu.VMEM((A_V_DIM + ONES_ROWS, 2 * tq), F32),
            pltpu.VMEM((tk, 2 * tq), F32),
            pltpu.VMEM((tk, 2 * tq), F32),
        ] * units,
        compiler_params=pltpu.CompilerParams(
            dimension_semantics=("arbitrary", "arbitrary", "arbitrary"), vmem_limit_bytes=VMEM_LIMIT_BYTES),
        name="attn_a",
    )(scal, qat, ka, vat, bias_tiles, sg_b, gate)


def _attn_b_kernel(qt_ref, k_ref, vt_ref, gate_ref, o_ref, *scratch, tq, tk, n_chunks, unroll, units):
    half = B_GROUP * tq
    lanes = B_HEADS * tq
    wq_refs = [scratch[5 * unit] for unit in range(units)]
    acc_refs = [scratch[5 * unit + 1:5 * unit + 3] for unit in range(units)]
    s_bufs = [scratch[5 * unit + 3:5 * unit + 5] for unit in range(units)]

    def qk(unit, c, s_ref):
        start = pl.multiple_of(c * tk, tk)
        s = jnp.dot(k_ref[0, pl.ds(start, tk), :], wq_refs[unit][...], preferred_element_type=F32)
        s_ref[...] = s
        return jnp.max(s, axis=0, keepdims=True)

    def softmax_pv(unit, s_ref, mc, m, c):
        acc0_ref, acc1_ref = acc_refs[unit]
        mn = jnp.maximum(m, mc)
        alpha = jnp.exp2(m - mn)
        p = jnp.exp2(s_ref[...] - mn).astype(BF16)
        start = pl.multiple_of(c * tk, tk)
        v0 = vt_ref[0, 0, :, pl.ds(start, tk)]
        v1 = vt_ref[0, 1, :, pl.ds(start, tk)]
        acc0_ref[...] = acc0_ref[...] * alpha[:, 0:half] + jnp.dot(v0, p[:, 0:half], preferred_element_type=F32)
        acc1_ref[...] = acc1_ref[...] * alpha[:, half:] + jnp.dot(v1, p[:, half:], preferred_element_type=F32)
        return mn

    carry = []
    for unit in range(units):
        wq_refs[unit][...] = jnp.zeros(wq_refs[unit].shape, BF16)
        for hh in range(B_HEADS):
            n = hh // B_GROUP
            wq_refs[unit][n * 64:(n + 1) * 64, hh * tq:(hh + 1) * tq] = qt_ref[0, hh * 64:(hh + 1) * 64,
                                                                               unit * tq:(unit + 1) * tq]
        for acc_ref in acc_refs[unit]:
            acc_ref[...] = jnp.zeros(acc_ref.shape, F32)
        carry.append((jnp.full((1, lanes), M_INIT, F32), qk(unit, 0, s_bufs[unit][0])))

    def body(i, carry):
        carry = list(carry)
        c0 = unroll * i
        for u in range(unroll):
            for unit in range(units):
                m, mc = carry[unit]
                mc_next = qk(unit, jnp.minimum(c0 + u + 1, n_chunks - 1), s_bufs[unit][(u + 1) % 2])
                m = softmax_pv(unit, s_bufs[unit][u % 2], mc, m, c0 + u)
                carry[unit] = (m, mc_next)
        return tuple(carry)

    lax.fori_loop(0, n_chunks // unroll, body, tuple(carry))

    for unit in range(units):
        parts = []
        for acc_ref in acc_refs[unit]:
            acc = acc_ref[...]
            o = acc[0:64, :] * (1.0 / acc[64:65, :])
            for g in range(B_GROUP):
                parts.append(o[:, g * tq:(g + 1) * tq])
        rows = slice(unit * tq, (unit + 1) * tq)
        out = jnp.concatenate(parts, axis=0).T
        o_ref[0, rows, :] = (out * gate_ref[0, rows, :]).astype(BF16)


def _attn_b(qbt, kb, vbt, gate, tq, tk):
    b, s, _ = kb.shape
    n_chunks = s // tk
    unroll, units = _loop_shape(n_chunks, s // tq, loop_units=1)
    kern = functools.partial(_attn_b_kernel, tq=tq, tk=tk, n_chunks=n_chunks, unroll=unroll, units=units)
    return pl.pallas_call(
        kern,
        grid=(b, s // (tq * units)),
        in_specs=[
            pl.BlockSpec((1, B_WIDTH, tq * units), lambda i, j: (i, 0, j)),
            pl.BlockSpec((1, s, B_KV_HEADS * B_HEAD_DIM), lambda i, j: (i, 0, 0)),
            pl.BlockSpec((1, B_KV_HEADS, B_HEAD_DIM + ONES_ROWS, s), lambda i, j: (i, 0, 0, 0)),
            pl.BlockSpec((1, tq * units, B_WIDTH), lambda i, j: (i, j, 1)),
        ],
        out_specs=pl.BlockSpec((1, tq * units, B_WIDTH), lambda i, j: (i, j, 0)),
        out_shape=jax.ShapeDtypeStruct((b, s, B_WIDTH), BF16),
        scratch_shapes=[
            pltpu.VMEM((B_KV_HEADS * B_HEAD_DIM, B_HEADS * tq), BF16),
            pltpu.VMEM((B_HEAD_DIM + ONES_ROWS, B_GROUP * tq), F32),
            pltpu.VMEM((B_HEAD_DIM + ONES_ROWS, B_GROUP * tq), F32),
            pltpu.VMEM((tk, B_HEADS * tq), F32),
            pltpu.VMEM((tk, B_HEADS * tq), F32),
        ] * units,
        compiler_params=pltpu.CompilerParams(
            dimension_semantics=("arbitrary", "arbitrary"), vmem_limit_bytes=VMEM_LIMIT_BYTES),
        name="attn_b",
    )(qbt, kb, vbt, gate)


def _outproj_kernel(ma_ref, mb_ref, x_ref, mod_ref, w_ref, fg_ref, o_ref, *, final):
    y = (jnp.dot(ma_ref[0], w_ref[0:A_WIDTH, :], preferred_element_type=F32)
         + jnp.dot(mb_ref[0], w_ref[A_WIDTH:D_MODEL, :], preferred_element_type=F32))
    xn = x_ref[0] + mod_ref[0, 2:3, :] * y
    if final:
        xn = _rms_rows(xn, fg_ref[...])
    o_ref[0] = xn


def _outproj(ma, mb, x, mod, w, fg, tm, final):
    b, s, _ = x.shape
    tok = lambda i, j: (i, j, 0)
    return pl.pallas_call(
        functools.partial(_outproj_kernel, final=final),
        grid=(b, s // tm),
        in_specs=[
            pl.BlockSpec((1, tm, A_WIDTH), tok),
            pl.BlockSpec((1, tm, B_WIDTH), tok),
            pl.BlockSpec((1, tm, D_MODEL), tok),
            pl.BlockSpec((1, 3, D_MODEL), lambda i, j: (i, 0, 0)),
            pl.BlockSpec((D_MODEL, D_MODEL), lambda i, j: (0, 0)),
            pl.BlockSpec((1, D_MODEL), lambda i, j: (0, 0)),
        ],
        out_specs=pl.BlockSpec((1, tm, D_MODEL), tok),
        out_shape=jax.ShapeDtypeStruct((b, s, D_MODEL), F32),
        compiler_params=pltpu.CompilerParams(
            dimension_semantics=("arbitrary", "arbitrary"), vmem_limit_bytes=VMEM_LIMIT_BYTES),
        name="outproj",
    )(ma, mb, x, mod, w, fg)


def _t5_bucket(rel):
    half = NUM_BUCKETS // 2
    max_exact = half // 2
    ret = jnp.where(rel > 0, half, 0)
    n = jnp.abs(rel)
    nf = jnp.maximum(n, 1).astype(jnp.float32)
    large = max_exact + (jnp.log(nf / max_exact) / math.log(MAX_DISTANCE / max_exact)
                         * (half - max_exact)).astype(jnp.int32)
    large = jnp.minimum(large, half - 1)
    return ret + jnp.where(n < max_exact, n, large)


def _bias_tables(rel_table, s, tq, tk):
    ratio = tk // tq
    assert tk >= MAX_DISTANCE and s - 1 >= MAX_DISTANCE
    offsets = jnp.arange(-MAX_DISTANCE, MAX_DISTANCE + 1, dtype=jnp.int32)
    core = rel_table[_t5_bucket(offsets)].T.astype(F32) * LOG2E
    reach = 2 * tk + tq
    boff_p = jnp.pad(core, ((0, 0), (reach - MAX_DISTANCE, reach - MAX_DISTANCE)), mode="edge")
    span = tk + tq - 1
    tiles = []
    for dc in (-1, 0, 1):
        for r in range(ratio):
            first = dc * tk - r * tq - (tq - 1) + reach
            u = jnp.flip(boff_p[:, first:first + span], axis=1)
            hankel = jnp.tile(u, (1, tk + 1))[:, :tk * (span + 1)].reshape(A_HEADS, tk, span + 1)[:, :, :tq]
            tiles.append(jnp.flip(hankel, axis=1))
    for col in (0, 2 * reach):
        tiles.append(jnp.broadcast_to(boff_p[:, col][:, None, None], (A_HEADS, tk, tq)))
    return jnp.stack(tiles, axis=1)


def _rope_tables_t(s):
    rows = s // GRID_W
    row = jnp.repeat(jnp.arange(rows), GRID_W).astype(F32)
    col = jnp.tile(jnp.arange(GRID_W), rows).astype(F32)
    axis_dim = B_HEAD_DIM // 2
    inv_freq = ROPE_THETA ** (-jnp.arange(0, axis_dim, 2, dtype=F32) / axis_dim)
    ang_r = row[:, None] * inv_freq[None, :]
    ang_c = col[:, None] * inv_freq[None, :]
    ang = jnp.concatenate([ang_r, ang_r, ang_c, ang_c], axis=-1)
    return jnp.cos(ang).T, jnp.sin(ang).T


def _tile_sizes(s):
    del s
    return 512, 1024, (256, 512), (128, 256)


def _trunk(x, mod, lam, rel_table, norm_g, wn, wt, subln_g, q_norm_g, k_norm_g, w_out, final_g):
    b, s, _ = x.shape
    tm, tm_out, (tq_a, tk_a), (tq_b, tk_b) = _tile_sizes(s)
    cos_t, sin_t = _rope_tables_t(s)
    bias_tiles = _bias_tables(rel_table, s, tq_a, tk_a)
    fg = final_g.reshape(1, D_MODEL)
    for l in range(DEPTH):
        lam_init = 0.8 - 0.6 * math.exp(-0.3 * l)
        qg_b = jnp.broadcast_to(q_norm_g[l][:, None], (B_HEAD_DIM, tm))
        kg_b = jnp.broadcast_to(k_norm_g[l][:, None], (B_HEAD_DIM, tm))
        sg_b = jnp.broadcast_to(subln_g[l][:, None], (A_V_DIM, tq_a))
        ka, gate, qat, vat, qbt, kb, vbt = _inproj(
            x, mod[l], norm_g[l].reshape(1, D_MODEL), wn[l], wt[l], cos_t, sin_t, qg_b, kg_b, tm)
        ma = _attn_a(lam[l:l + 1], qat, ka, vat, bias_tiles, sg_b, gate, tq_a, tk_a, 1.0 - lam_init)
        mb = _attn_b(qbt, kb, vbt, gate, tq_b, tk_b)
        x = _outproj(ma, mb, x, mod[l], w_out[l], fg, tm_out, final=(l == DEPTH - 1))
    return x


def kernel(x_prompt, x_sample, c_prompt, c_sample, rel_table, norm_g, w_ada, b_ada, w_in, lam_q1, lam_k1, lam_q2,
           lam_k2, subln_g, q_norm_g, k_norm_g, w_out, final_g):
    bp = x_prompt.shape[0]
    bs = x_sample.shape[0]
    rows = -(-(bp + bs) // 8) * 8
    c_all = jnp.concatenate([c_prompt, c_sample, jnp.zeros((rows - bp - bs, D_MODEL), F32)], axis=0)
    mod_all = _adaln_mod(c_all, w_ada.astype(BF16), b_ada)
    mod_p = mod_all[:, :bp].reshape(DEPTH, bp, 3, D_MODEL)
    mod_s = mod_all[:, bp:bp + bs].reshape(DEPTH, bs, 3, D_MODEL)
    lam = _lambdas(lam_q1, lam_k1, lam_q2, lam_k2)

    w_bf = w_in.astype(BF16)
    wn = jnp.concatenate([w_bf[:, :, _C_KA:_C_VA], w_bf[:, :, _C_GA:_C_QB], w_bf[:, :, _C_GB:D_IN]], axis=2)
    wt = jnp.concatenate([w_bf[:, :, _C_QA:_C_KA], w_bf[:, :, _C_VA:_C_GA], w_bf[:, :, _C_QB:_C_GB]], axis=2)
    wt = jnp.swapaxes(wt, 1, 2)
    w_out_bf = w_out.astype(BF16)

    args = (rel_table, norm_g, wn, wt, subln_g, q_norm_g, k_norm_g, w_out_bf, final_g)
    y_prompt = _trunk(x_prompt, mod_p, lam, *args)
    y_sample = _trunk(x_sample, mod_s, lam, *args)
    return (y_prompt, y_sample)
```

```python
import functools
import math

import jax
import jax.numpy as jnp
from jax import lax
from jax.experimental import pallas as pl
from jax.experimental.pallas import tpu as pltpu

F32 = jnp.float32
BF16 = jnp.bfloat16

D_MODEL = 1024
DEPTH = 4
GRID_W = 64
A_WIDTH = 512
A_V_DIM = 128
A_HEAD_DIM = 64
A_HEADS = 4
B_WIDTH = 512
B_HEAD_DIM = 64
B_HEADS = 8
B_KV_HEADS = 2
B_GROUP = 4
NUM_BUCKETS = 32
MAX_DISTANCE = 128
ROPE_THETA = 10000.0
EPS = 1e-6

LOG2E = 1.4426950408889634
Q_SCALE = (A_HEAD_DIM ** -0.5) * LOG2E
M_INIT = -1e30
ONES_ROWS = 16
VMEM_LIMIT_BYTES = 56 * 1024 * 1024
UNITS_PER_STEP = 4
LOOP_BODY_CHUNKS = 16

_C_QA, _C_KA, _C_VA, _C_GA, _C_QB, _C_KB, _C_VB, _C_GB = 0, 512, 1024, 1536, 2048, 2560, 2688, 2816
D_IN = 3328
N_NAT = 1536
N_TR = 1792


def _rms_rows(x, g):
    ms = jnp.mean(x * x, axis=-1, keepdims=True)
    return x * lax.rsqrt(ms + EPS) * g


def _mod_kernel(c_ref, w_ref, b_ref, o_ref):
    c = c_ref[...]
    c_act = (c * jax.nn.sigmoid(c)).astype(BF16)
    o_ref[0] = jnp.dot(c_act, w_ref[0], preferred_element_type=F32) + b_ref[0]


def _adaln_mod(c_all, w_ada_bf, b_ada):
    rows = c_all.shape[0]
    return pl.pallas_call(
        _mod_kernel,
        grid=(DEPTH, 3),
        in_specs=[
            pl.BlockSpec((rows, D_MODEL), lambda l, j: (0, 0)),
            pl.BlockSpec((1, D_MODEL, D_MODEL), lambda l, j: (l, 0, j)),
            pl.BlockSpec((1, 1, D_MODEL), lambda l, j: (l, 0, j)),
        ],
        out_specs=pl.BlockSpec((1, rows, D_MODEL), lambda l, j: (l, 0, j)),
        out_shape=jax.ShapeDtypeStruct((DEPTH, rows, 3 * D_MODEL), F32),
        name="adaln_mod",
    )(c_all, w_ada_bf, b_ada.reshape(DEPTH, 1, 3 * D_MODEL))


def _lam_kernel(q1_ref, k1_ref, q2_ref, k2_ref, init_ref, o_ref):
    s1 = jnp.sum(q1_ref[...] * k1_ref[...], axis=-1, keepdims=True)
    s2 = jnp.sum(q2_ref[...] * k2_ref[...], axis=-1, keepdims=True)
    lam = jnp.exp(s1) - jnp.exp(s2) + init_ref[...][:, 0:1]
    o_ref[...] = jnp.broadcast_to(lam, o_ref.shape)


def _lambdas(lam_q1, lam_k1, lam_q2, lam_k2):
    init = jnp.asarray([[0.8 - 0.6 * math.exp(-0.3 * l)] * 128 for l in range(DEPTH)], F32)
    out = pl.pallas_call(
        _lam_kernel,
        out_shape=jax.ShapeDtypeStruct((DEPTH, 128), F32),
        name="diff_lambda",
    )(lam_q1, lam_k1, lam_q2, lam_k2, init)
    return out[:, 0]


def _rope_t(x, cos, sin):
    rot = jnp.concatenate([-x[16:32], x[0:16], -x[48:64], x[32:48]], axis=0)
    return x * cos + rot * sin


def _norm_t(x, g):
    ms = jnp.mean(x * x, axis=0, keepdims=True)
    return x * lax.rsqrt(ms + EPS) * g


def _inproj_kernel(x_ref, mod_ref, ng_ref, wn_ref, wt_ref, cos_ref, sin_ref, qg_ref, kg_ref,
                   ka_ref, gate_ref, qat_ref, vat_ref, qbt_ref, kb_ref, vbt_ref):
    ts = qg_ref.shape[1]
    for sub in range(x_ref.shape[1] // ts):
        _inproj_subtile(slice(sub * ts, (sub + 1) * ts), x_ref, mod_ref, ng_ref, wn_ref, wt_ref, cos_ref, sin_ref,
                        qg_ref, kg_ref, ka_ref, gate_ref, qat_ref, vat_ref, qbt_ref, kb_ref, vbt_ref)


def _inproj_subtile(rows, x_ref, mod_ref, ng_ref, wn_ref, wt_ref, cos_ref, sin_ref, qg_ref, kg_ref,
                    ka_ref, gate_ref, qat_ref, vat_ref, qbt_ref, kb_ref, vbt_ref):
    ts = rows.stop - rows.start
    x = x_ref[0, rows, :]
    shift = mod_ref[0, 0:1, :]
    scale = mod_ref[0, 1:2, :]
    h = (_rms_rows(x, ng_ref[...]) * (1.0 + scale) + shift).astype(BF16)

    pn = jnp.dot(h, wn_ref[...], preferred_element_type=F32)
    ka_ref[0, rows, :] = pn[:, 0:512].astype(BF16)
    g = pn[:, 512:N_NAT]
    gate_ref[0, rows, :] = g * jax.nn.sigmoid(g)

    pt = lax.dot_general(wt_ref[...], h, (((1,), (1,)), ((), ())), preferred_element_type=F32)
    qat_ref[0, :, rows] = (pt[0:512] * Q_SCALE).astype(BF16)
    ones = jnp.ones((ONES_ROWS, ts), BF16)
    for hh in range(A_HEADS):
        vat_ref[0, hh, 0:A_V_DIM, rows] = pt[512 + hh * A_V_DIM:512 + (hh + 1) * A_V_DIM].astype(BF16)
        vat_ref[0, hh, A_V_DIM:A_V_DIM + ONES_ROWS, rows] = ones

    cos = cos_ref[:, rows]
    sin = sin_ref[:, rows]
    qg = qg_ref[...]
    for hh in range(B_HEADS):
        xh = pt[1024 + hh * 64:1024 + (hh + 1) * 64]
        qbt_ref[0, hh * 64:(hh + 1) * 64, rows] = (_rope_t(_norm_t(xh, qg), cos, sin) * Q_SCALE).astype(BF16)
    kg = kg_ref[...]
    kparts = []
    for n in range(B_KV_HEADS):
        xh = pt[1536 + n * 64:1536 + (n + 1) * 64]
        kparts.append(_rope_t(_norm_t(xh, kg), cos, sin))
    kb_ref[0, rows, :] = jnp.concatenate(kparts, axis=0).T.astype(BF16)
    for n in range(B_KV_HEADS):
        vbt_ref[0, n, 0:64, rows] = pt[1664 + n * 64:1664 + (n + 1) * 64].astype(BF16)
        vbt_ref[0, n, 64:64 + ONES_ROWS, rows] = ones


def _inproj(x, mod, ng, wn, wt, cos_t, sin_t, qg_b, kg_b, tm):
    b, s, _ = x.shape
    grid = (b, s // tm)
    ts = qg_b.shape[1]
    const2 = lambda i, j: (0, 0)
    return pl.pallas_call(
        _inproj_kernel,
        grid=grid,
        in_specs=[
            pl.BlockSpec((1, tm, D_MODEL), lambda i, j: (i, j, 0)),
            pl.BlockSpec((1, 3, D_MODEL), lambda i, j: (i, 0, 0)),
            pl.BlockSpec((1, D_MODEL), const2),
            pl.BlockSpec((D_MODEL, N_NAT), const2, pipeline_mode=pl.Buffered(1)),
            pl.BlockSpec((N_TR, D_MODEL), const2, pipeline_mode=pl.Buffered(1)),
            pl.BlockSpec((B_HEAD_DIM, tm), lambda i, j: (0, j)),
            pl.BlockSpec((B_HEAD_DIM, tm), lambda i, j: (0, j)),
            pl.BlockSpec((B_HEAD_DIM, ts), const2),
            pl.BlockSpec((B_HEAD_DIM, ts), const2),
        ],
        out_specs=[
            pl.BlockSpec((1, tm, A_WIDTH), lambda i, j: (i, j, 0)),
            pl.BlockSpec((1, tm, D_MODEL), lambda i, j: (i, j, 0)),
            pl.BlockSpec((1, A_WIDTH, tm), lambda i, j: (i, 0, j)),
            pl.BlockSpec((1, A_HEADS, A_V_DIM + ONES_ROWS, tm), lambda i, j: (i, 0, 0, j)),
            pl.BlockSpec((1, B_WIDTH, tm), lambda i, j: (i, 0, j)),
            pl.BlockSpec((1, tm, B_KV_HEADS * B_HEAD_DIM), lambda i, j: (i, j, 0)),
            pl.BlockSpec((1, B_KV_HEADS, B_HEAD_DIM + ONES_ROWS, tm), lambda i, j: (i, 0, 0, j)),
        ],
        out_shape=[
            jax.ShapeDtypeStruct((b, s, A_WIDTH), BF16),
            jax.ShapeDtypeStruct((b, s, D_MODEL), F32),
            jax.ShapeDtypeStruct((b, A_WIDTH, s), BF16),
            jax.ShapeDtypeStruct((b, A_HEADS, A_V_DIM + ONES_ROWS, s), BF16),
            jax.ShapeDtypeStruct((b, B_WIDTH, s), BF16),
            jax.ShapeDtypeStruct((b, s, B_KV_HEADS * B_HEAD_DIM), BF16),
            jax.ShapeDtypeStruct((b, B_KV_HEADS, B_HEAD_DIM + ONES_ROWS, s), BF16),
        ],
        compiler_params=pltpu.CompilerParams(
            dimension_semantics=("arbitrary", "arbitrary"), vmem_limit_bytes=VMEM_LIMIT_BYTES),
        name="inproj",
    )(x, mod, ng, wn, wt, cos_t, sin_t, qg_b, kg_b)


def _loop_shape(n_chunks, n_tiles, loop_units):
    if n_chunks <= LOOP_BODY_CHUNKS // 2:
        unroll, units = n_chunks, UNITS_PER_STEP
    else:
        unroll, units = LOOP_BODY_CHUNKS // loop_units, loop_units
    assert n_chunks % unroll == 0 and unroll % 2 == 0
    while n_tiles % units:
        units //= 2
    return unroll, units


def _sub_steps(unroll, units, n_chunks):
    if n_chunks == unroll:
        return [(u, unit) for unit in range(units) for u in range(unroll)]
    return [(u, unit) for u in range(unroll) for unit in range(units)]


def _attn_a_kernel(scal_ref, qt_ref, k_ref, vt_ref, bias_ref, sg_ref, gate_ref, o_ref, *scratch,
                   tq, tk, n_chunks, ratio, unroll, units, out_scale):
    lam = scal_ref[0]
    log_ratio = ratio.bit_length() - 1
    wq_refs = [scratch[4 * unit] for unit in range(units)]
    acc_refs = [scratch[4 * unit + 1] for unit in range(units)]
    s_bufs = [scratch[4 * unit + 2:4 * unit + 4] for unit in range(units)]
    tiles = [pl.program_id(2) * units + unit for unit in range(units)]

    def qk(unit, c, s_ref):
        start = pl.multiple_of(c * tk, tk)
        d = c - lax.shift_right_logical(tiles[unit], log_ratio)
        r = lax.bitwise_and(tiles[unit], ratio - 1)
        tile = jnp.where(d < -1, 3 * ratio, jnp.where(d > 1, 3 * ratio + 1, (d + 1) * ratio + r))
        t = bias_ref[0, tile]
        s = jnp.dot(k_ref[0, pl.ds(start, tk), :], wq_refs[unit][...], preferred_element_type=F32)
        s = s + jnp.concatenate([t, t], axis=1)
        s_ref[...] = s
        return jnp.max(s, axis=0, keepdims=True)

    def softmax_pv(unit, s_ref, mc, m, c):
        mn = jnp.maximum(m, mc)
        alpha = jnp.exp2(m - mn)
        p = jnp.exp2(s_ref[...] - mn).astype(BF16)
        start = pl.multiple_of(c * tk, tk)
        pv = jnp.dot(vt_ref[0, 0, :, pl.ds(start, tk)], p, preferred_element_type=F32)
        acc_refs[unit][...] = acc_refs[unit][...] * alpha + pv
        return mn

    carry = []
    for unit in range(units):
        q = qt_ref[0, :, unit * tq:(unit + 1) * tq]
        zero = jnp.zeros((A_HEAD_DIM, tq), BF16)
        wq_refs[unit][0:64, 0:tq] = q[0:64]
        wq_refs[unit][0:64, tq:2 * tq] = zero
        wq_refs[unit][64:128, 0:tq] = zero
        wq_refs[unit][64:128, tq:2 * tq] = q[64:128]
        acc_refs[unit][...] = jnp.zeros(acc_refs[unit].shape, F32)
        carry.append((jnp.full((1, 2 * tq), M_INIT, F32), qk(unit, 0, s_bufs[unit][0])))

    def body(i, carry):
        carry = list(carry)
        c0 = unroll * i
        for u, unit in _sub_steps(unroll, units, n_chunks):
            m, mc = carry[unit]
            mc_next = qk(unit, jnp.minimum(c0 + u + 1, n_chunks - 1), s_bufs[unit][(u + 1) % 2])
            m = softmax_pv(unit, s_bufs[unit][u % 2], mc, m, c0 + u)
            carry[unit] = (m, mc_next)
        return tuple(carry)

    lax.fori_loop(0, n_chunks // unroll, body, tuple(carry))

    for unit in range(units):
        acc = acc_refs[unit][...]
        o = acc[0:A_V_DIM, :] * (1.0 / acc[A_V_DIM:A_V_DIM + 1, :])
        w = o[:, 0:tq] - lam * o[:, tq:2 * tq]
        y = _norm_t(w, sg_ref[...]) * out_scale
        rows = slice(unit * tq, (unit + 1) * tq)
        o_ref[0, rows, :] = (y.T * gate_ref[0, rows, :]).astype(BF16)


def _attn_a(scal, qat, ka, vat, bias_tiles, sg_b, gate, tq, tk, out_scale):
    b, s, _ = ka.shape
    n_chunks = s // tk
    ratio = tk // tq
    unroll, units = _loop_shape(n_chunks, s // tq, loop_units=2)
    assert n_chunks % unroll == 0 and ratio * tq == tk and ratio & (ratio - 1) == 0
    kern = functools.partial(_attn_a_kernel, tq=tq, tk=tk, n_chunks=n_chunks, ratio=ratio, unroll=unroll,
                             units=units, out_scale=out_scale)
    return pl.pallas_call(
        kern,
        grid=(b, A_HEADS, s // (tq * units)),
        in_specs=[
            pl.BlockSpec(memory_space=pltpu.SMEM),
            pl.BlockSpec((1, 2 * A_HEAD_DIM, tq * units), lambda i, hh, j: (i, hh, j)),
            pl.BlockSpec((1, s, 2 * A_HEAD_DIM), lambda i, hh, j: (i, 0, hh)),
            pl.BlockSpec((1, 1, A_V_DIM + ONES_ROWS, s), lambda i, hh, j: (i, hh, 0, 0)),
            pl.BlockSpec((1, 3 * ratio + 2, tk, tq), lambda i, hh, j: (hh, 0, 0, 0)),
            pl.BlockSpec((A_V_DIM, tq), lambda i, hh, j: (0, 0)),
            pl.BlockSpec((1, tq * units, A_V_DIM), lambda i, hh, j: (i, j, hh)),
        ],
        out_specs=pl.BlockSpec((1, tq * units, A_V_DIM), lambda i, hh, j: (i, j, hh)),
        out_shape=jax.ShapeDtypeStruct((b, s, A_WIDTH), BF16),
        scratch_shapes=[
            pltpu.VMEM((2 * A_HEAD_DIM, 2 * tq), BF16),
            pltpu.VMEM((A_V_DIM + ONES_ROWS, 2 * tq), F32),
            pltpu.VMEM((tk, 2 * tq), F32),
            pltpu.VMEM((tk, 2 * tq), F32),
        ] * units,
        compiler_params=pltpu.CompilerParams(
            dimension_semantics=("arbitrary", "arbitrary", "arbitrary"), vmem_limit_bytes=VMEM_LIMIT_BYTES),
        name="attn_a",
    )(scal, qat, ka, vat, bias_tiles, sg_b, gate)


def _attn_b_kernel(qt_ref, k_ref, vt_ref, gate_ref, o_ref, *scratch, tq, tk, n_chunks, unroll, units):
    half = B_GROUP * tq
    lanes = B_HEADS * tq
    wq_refs = [scratch[5 * unit] for unit in range(units)]
    acc_refs = [scratch[5 * unit + 1:5 * unit + 3] for unit in range(units)]
    s_bufs = [scratch[5 * unit + 3:5 * unit + 5] for unit in range(units)]

    def qk(unit, c, s_ref):
        start = pl.multiple_of(c * tk, tk)
        s = jnp.dot(k_ref[0, pl.ds(start, tk), :], wq_refs[unit][...], preferred_element_type=F32)
        s_ref[...] = s
        return jnp.max(s, axis=0, keepdims=True)

    def softmax_pv(unit, s_ref, mc, m, c):
        acc0_ref, acc1_ref = acc_refs[unit]
        mn = jnp.maximum(m, mc)
        alpha = jnp.exp2(m - mn)
        p = jnp.exp2(s_ref[...] - mn).astype(BF16)
        start = pl.multiple_of(c * tk, tk)
        v0 = vt_ref[0, 0, :, pl.ds(start, tk)]
        v1 = vt_ref[0, 1, :, pl.ds(start, tk)]
        acc0_ref[...] = acc0_ref[...] * alpha[:, 0:half] + jnp.dot(v0, p[:, 0:half], preferred_element_type=F32)
        acc1_ref[...] = acc1_ref[...] * alpha[:, half:] + jnp.dot(v1, p[:, half:], preferred_element_type=F32)
        return mn

    carry = []
    for unit in range(units):
        wq_refs[unit][...] = jnp.zeros(wq_refs[unit].shape, BF16)
        for hh in range(B_HEADS):
            n = hh // B_GROUP
            wq_refs[unit][n * 64:(n + 1) * 64, hh * tq:(hh + 1) * tq] = qt_ref[0, hh * 64:(hh + 1) * 64,
                                                                               unit * tq:(unit + 1) * tq]
        for acc_ref in acc_refs[unit]:
            acc_ref[...] = jnp.zeros(acc_ref.shape, F32)
        carry.append((jnp.full((1, lanes), M_INIT, F32), qk(unit, 0, s_bufs[unit][0])))

    def body(i, carry):
        carry = list(carry)
        c0 = unroll * i
        for u, unit in _sub_steps(unroll, units, n_chunks):
            m, mc = carry[unit]
            mc_next = qk(unit, jnp.minimum(c0 + u + 1, n_chunks - 1), s_bufs[unit][(u + 1) % 2])
            m = softmax_pv(unit, s_bufs[unit][u % 2], mc, m, c0 + u)
            carry[unit] = (m, mc_next)
        return tuple(carry)

    lax.fori_loop(0, n_chunks // unroll, body, tuple(carry))

    for unit in range(units):
        parts = []
        for acc_ref in acc_refs[unit]:
            acc = acc_ref[...]
            o = acc[0:64, :] * (1.0 / acc[64:65, :])
            for g in range(B_GROUP):
                parts.append(o[:, g * tq:(g + 1) * tq])
        rows = slice(unit * tq, (unit + 1) * tq)
        out = jnp.concatenate(parts, axis=0).T
        o_ref[0, rows, :] = (out * gate_ref[0, rows, :]).astype(BF16)


def _attn_b(qbt, kb, vbt, gate, tq, tk):
    b, s, _ = kb.shape
    n_chunks = s // tk
    unroll, units = _loop_shape(n_chunks, s // tq, loop_units=1)
    kern = functools.partial(_attn_b_kernel, tq=tq, tk=tk, n_chunks=n_chunks, unroll=unroll, units=units)
    return pl.pallas_call(
        kern,
        grid=(b, s // (tq * units)),
        in_specs=[
            pl.BlockSpec((1, B_WIDTH, tq * units), lambda i, j: (i, 0, j)),
            pl.BlockSpec((1, s, B_KV_HEADS * B_HEAD_DIM), lambda i, j: (i, 0, 0)),
            pl.BlockSpec((1, B_KV_HEADS, B_HEAD_DIM + ONES_ROWS, s), lambda i, j: (i, 0, 0, 0)),
            pl.BlockSpec((1, tq * units, B_WIDTH), lambda i, j: (i, j, 1)),
        ],
        out_specs=pl.BlockSpec((1, tq * units, B_WIDTH), lambda i, j: (i, j, 0)),
        out_shape=jax.ShapeDtypeStruct((b, s, B_WIDTH), BF16),
        scratch_shapes=[
            pltpu.VMEM((B_KV_HEADS * B_HEAD_DIM, B_HEADS * tq), BF16),
            pltpu.VMEM((B_HEAD_DIM + ONES_ROWS, B_GROUP * tq), F32),
            pltpu.VMEM((B_HEAD_DIM + ONES_ROWS, B_GROUP * tq), F32),
            pltpu.VMEM((tk, B_HEADS * tq), F32),
            pltpu.VMEM((tk, B_HEADS * tq), F32),
        ] * units,
        compiler_params=pltpu.CompilerParams(
            dimension_semantics=("arbitrary", "arbitrary"), vmem_limit_bytes=VMEM_LIMIT_BYTES),
        name="attn_b",
    )(qbt, kb, vbt, gate)


def _outproj_kernel(ma_ref, mb_ref, x_ref, mod_ref, w_ref, fg_ref, o_ref, *, final):
    y = (jnp.dot(ma_ref[0], w_ref[0:A_WIDTH, :], preferred_element_type=F32)
         + jnp.dot(mb_ref[0], w_ref[A_WIDTH:D_MODEL, :], preferred_element_type=F32))
    xn = x_ref[0] + mod_ref[0, 2:3, :] * y
    if final:
        xn = _rms_rows(xn, fg_ref[...])
    o_ref[0] = xn


def _outproj(ma, mb, x, mod, w, fg, tm, final):
    b, s, _ = x.shape
    tok = lambda i, j: (i, j, 0)
    return pl.pallas_call(
        functools.partial(_outproj_kernel, final=final),
        grid=(b, s // tm),
        in_specs=[
            pl.BlockSpec((1, tm, A_WIDTH), tok),
            pl.BlockSpec((1, tm, B_WIDTH), tok),
            pl.BlockSpec((1, tm, D_MODEL), tok),
            pl.BlockSpec((1, 3, D_MODEL), lambda i, j: (i, 0, 0)),
            pl.BlockSpec((D_MODEL, D_MODEL), lambda i, j: (0, 0)),
            pl.BlockSpec((1, D_MODEL), lambda i, j: (0, 0)),
        ],
        out_specs=pl.BlockSpec((1, tm, D_MODEL), tok),
        out_shape=jax.ShapeDtypeStruct((b, s, D_MODEL), F32),
        compiler_params=pltpu.CompilerParams(
            dimension_semantics=("arbitrary", "arbitrary"), vmem_limit_bytes=VMEM_LIMIT_BYTES),
        name="outproj",
    )(ma, mb, x, mod, w, fg)


def _t5_bucket(rel):
    half = NUM_BUCKETS // 2
    max_exact = half // 2
    ret = jnp.where(rel > 0, half, 0)
    n = jnp.abs(rel)
    nf = jnp.maximum(n, 1).astype(jnp.float32)
    large = max_exact + (jnp.log(nf / max_exact) / math.log(MAX_DISTANCE / max_exact)
                         * (half - max_exact)).astype(jnp.int32)
    large = jnp.minimum(large, half - 1)
    return ret + jnp.where(n < max_exact, n, large)


def _bias_tables(rel_table, s, tq, tk):
    ratio = tk // tq
    assert tk >= MAX_DISTANCE and s - 1 >= MAX_DISTANCE
    offsets = jnp.arange(-MAX_DISTANCE, MAX_DISTANCE + 1, dtype=jnp.int32)
    core = rel_table[_t5_bucket(offsets)].T.astype(F32) * LOG2E
    reach = 2 * tk + tq
    boff_p = jnp.pad(core, ((0, 0), (reach - MAX_DISTANCE, reach - MAX_DISTANCE)), mode="edge")
    span = tk + tq - 1
    tiles = []
    for dc in (-1, 0, 1):
        for r in range(ratio):
            first = dc * tk - r * tq - (tq - 1) + reach
            u = jnp.flip(boff_p[:, first:first + span], axis=1)
            hankel = jnp.tile(u, (1, tk + 1))[:, :tk * (span + 1)].reshape(A_HEADS, tk, span + 1)[:, :, :tq]
            tiles.append(jnp.flip(hankel, axis=1))
    for col in (0, 2 * reach):
        tiles.append(jnp.broadcast_to(boff_p[:, col][:, None, None], (A_HEADS, tk, tq)))
    return jnp.stack(tiles, axis=1)


def _rope_tables_t(s):
    rows = s // GRID_W
    row = jnp.repeat(jnp.arange(rows), GRID_W).astype(F32)
    col = jnp.tile(jnp.arange(GRID_W), rows).astype(F32)
    axis_dim = B_HEAD_DIM // 2
    inv_freq = ROPE_THETA ** (-jnp.arange(0, axis_dim, 2, dtype=F32) / axis_dim)
    ang_r = row[:, None] * inv_freq[None, :]
    ang_c = col[:, None] * inv_freq[None, :]
    ang = jnp.concatenate([ang_r, ang_r, ang_c, ang_c], axis=-1)
    return jnp.cos(ang).T, jnp.sin(ang).T


def _tile_sizes(s):
    del s
    return (512, 1024), 1024, (256, 512), (128, 256)


def _trunk(x, mod, lam, rel_table, norm_g, wn, wt, subln_g, q_norm_g, k_norm_g, w_out, final_g):
    b, s, _ = x.shape
    (ts_in, tm), tm_out, (tq_a, tk_a), (tq_b, tk_b) = _tile_sizes(s)
    cos_t, sin_t = _rope_tables_t(s)
    bias_tiles = _bias_tables(rel_table, s, tq_a, tk_a)
    fg = final_g.reshape(1, D_MODEL)
    for l in range(DEPTH):
        lam_init = 0.8 - 0.6 * math.exp(-0.3 * l)
        qg_b = jnp.broadcast_to(q_norm_g[l][:, None], (B_HEAD_DIM, ts_in))
        kg_b = jnp.broadcast_to(k_norm_g[l][:, None], (B_HEAD_DIM, ts_in))
        sg_b = jnp.broadcast_to(subln_g[l][:, None], (A_V_DIM, tq_a))
        ka, gate, qat, vat, qbt, kb, vbt = _inproj(
            x, mod[l], norm_g[l].reshape(1, D_MODEL), wn[l], wt[l], cos_t, sin_t, qg_b, kg_b, tm)
        ma = _attn_a(lam[l:l + 1], qat, ka, vat, bias_tiles, sg_b, gate, tq_a, tk_a, 1.0 - lam_init)
        mb = _attn_b(qbt, kb, vbt, gate, tq_b, tk_b)
        x = _outproj(ma, mb, x, mod[l], w_out[l], fg, tm_out, final=(l == DEPTH - 1))
    return x


def kernel(x_prompt, x_sample, c_prompt, c_sample, rel_table, norm_g, w_ada, b_ada, w_in, lam_q1, lam_k1, lam_q2,
           lam_k2, subln_g, q_norm_g, k_norm_g, w_out, final_g):
    bp = x_prompt.shape[0]
    bs = x_sample.shape[0]
    rows = -(-(bp + bs) // 8) * 8
    c_all = jnp.concatenate([c_prompt, c_sample, jnp.zeros((rows - bp - bs, D_MODEL), F32)], axis=0)
    mod_all = _adaln_mod(c_all, w_ada.astype(BF16), b_ada)
    mod_p = mod_all[:, :bp].reshape(DEPTH, bp, 3, D_MODEL)
    mod_s = mod_all[:, bp:bp + bs].reshape(DEPTH, bs, 3, D_MODEL)
    lam = _lambdas(lam_q1, lam_k1, lam_q2, lam_k2)

    w_bf = w_in.astype(BF16)
    wn = jnp.concatenate([w_bf[:, :, _C_KA:_C_VA], w_bf[:, :, _C_GA:_C_QB], w_bf[:, :, _C_GB:D_IN]], axis=2)
    wt = jnp.concatenate([w_bf[:, :, _C_QA:_C_KA], w_bf[:, :, _C_VA:_C_GA], w_bf[:, :, _C_QB:_C_GB]], axis=2)
    wt = jnp.swapaxes(wt, 1, 2)
    w_out_bf = w_out.astype(BF16)

    args = (rel_table, norm_g, wn, wt, subln_g, q_norm_g, k_norm_g, w_out_bf, final_g)
    y_prompt = _trunk(x_prompt, mod_p, lam, *args)
    y_sample = _trunk(x_sample, mod_s, lam, *args)
    return (y_prompt, y_sample)
```

```python
import functools
import math

import jax
import jax.numpy as jnp
from jax import lax
from jax.experimental import pallas as pl
from jax.experimental.pallas import tpu as pltpu

F32 = jnp.float32
BF16 = jnp.bfloat16

D_MODEL = 1024
DEPTH = 4
GRID_W = 64
A_WIDTH = 512
A_V_DIM = 128
A_HEAD_DIM = 64
A_HEADS = 4
B_WIDTH = 512
B_HEAD_DIM = 64
B_HEADS = 8
B_KV_HEADS = 2
B_GROUP = 4
NUM_BUCKETS = 32
MAX_DISTANCE = 128
ROPE_THETA = 10000.0
EPS = 1e-6

LOG2E = 1.4426950408889634
Q_SCALE = (A_HEAD_DIM ** -0.5) * LOG2E
M_INIT = -1e30
ONES_ROWS = 16
VMEM_LIMIT_BYTES = 56 * 1024 * 1024
UNITS_PER_STEP = 8
LOOP_BODY_CHUNKS = 16

_C_QA, _C_KA, _C_VA, _C_GA, _C_QB, _C_KB, _C_VB, _C_GB = 0, 512, 1024, 1536, 2048, 2560, 2688, 2816
D_IN = 3328
N_NAT = 1536
N_TR = 1792


def _rms_rows(x, g):
    ms = jnp.mean(x * x, axis=-1, keepdims=True)
    return x * lax.rsqrt(ms + EPS) * g


def _mod_kernel(c_ref, w_ref, b_ref, o_ref):
    c = c_ref[...]
    c_act = (c * jax.nn.sigmoid(c)).astype(BF16)
    o_ref[0] = jnp.dot(c_act, w_ref[0], preferred_element_type=F32) + b_ref[0]


def _adaln_mod(c_all, w_ada_bf, b_ada):
    rows = c_all.shape[0]
    return pl.pallas_call(
        _mod_kernel,
        grid=(DEPTH, 3),
        in_specs=[
            pl.BlockSpec((rows, D_MODEL), lambda l, j: (0, 0)),
            pl.BlockSpec((1, D_MODEL, D_MODEL), lambda l, j: (l, 0, j)),
            pl.BlockSpec((1, 1, D_MODEL), lambda l, j: (l, 0, j)),
        ],
        out_specs=pl.BlockSpec((1, rows, D_MODEL), lambda l, j: (l, 0, j)),
        out_shape=jax.ShapeDtypeStruct((DEPTH, rows, 3 * D_MODEL), F32),
        name="adaln_mod",
    )(c_all, w_ada_bf, b_ada.reshape(DEPTH, 1, 3 * D_MODEL))


def _lam_kernel(q1_ref, k1_ref, q2_ref, k2_ref, init_ref, o_ref):
    s1 = jnp.sum(q1_ref[...] * k1_ref[...], axis=-1, keepdims=True)
    s2 = jnp.sum(q2_ref[...] * k2_ref[...], axis=-1, keepdims=True)
    lam = jnp.exp(s1) - jnp.exp(s2) + init_ref[...][:, 0:1]
    o_ref[...] = jnp.broadcast_to(lam, o_ref.shape)


def _lambdas(lam_q1, lam_k1, lam_q2, lam_k2):
    init = jnp.asarray([[0.8 - 0.6 * math.exp(-0.3 * l)] * 128 for l in range(DEPTH)], F32)
    out = pl.pallas_call(
        _lam_kernel,
        out_shape=jax.ShapeDtypeStruct((DEPTH, 128), F32),
        name="diff_lambda",
    )(lam_q1, lam_k1, lam_q2, lam_k2, init)
    return out[:, 0]


def _rope_t(x, cos, sin):
    rot = jnp.concatenate([-x[16:32], x[0:16], -x[48:64], x[32:48]], axis=0)
    return x * cos + rot * sin


def _norm_t(x, g):
    ms = jnp.mean(x * x, axis=0, keepdims=True)
    return x * lax.rsqrt(ms + EPS) * g


def _inproj_kernel(x_ref, mod_ref, ng_ref, wn_ref, wt_ref, cos_ref, sin_ref, qg_ref, kg_ref,
                   ka_ref, gate_ref, qat_ref, vat_ref, qbt_ref, kb_ref, vbt_ref):
    ts = qg_ref.shape[1]
    for sub in range(x_ref.shape[1] // ts):
        _inproj_subtile(slice(sub * ts, (sub + 1) * ts), x_ref, mod_ref, ng_ref, wn_ref, wt_ref, cos_ref, sin_ref,
                        qg_ref, kg_ref, ka_ref, gate_ref, qat_ref, vat_ref, qbt_ref, kb_ref, vbt_ref)


def _inproj_subtile(rows, x_ref, mod_ref, ng_ref, wn_ref, wt_ref, cos_ref, sin_ref, qg_ref, kg_ref,
                    ka_ref, gate_ref, qat_ref, vat_ref, qbt_ref, kb_ref, vbt_ref):
    ts = rows.stop - rows.start
    x = x_ref[0, rows, :]
    shift = mod_ref[0, 0:1, :]
    scale = mod_ref[0, 1:2, :]
    h = (_rms_rows(x, ng_ref[...]) * (1.0 + scale) + shift).astype(BF16)

    pn = jnp.dot(h, wn_ref[...], preferred_element_type=F32)
    ka_ref[0, rows, :] = pn[:, 0:512].astype(BF16)
    g = pn[:, 512:N_NAT]
    gate_ref[0, rows, :] = g * jax.nn.sigmoid(g)

    pt = lax.dot_general(wt_ref[...], h, (((1,), (1,)), ((), ())), preferred_element_type=F32)
    qat_ref[0, :, rows] = (pt[0:512] * Q_SCALE).astype(BF16)
    ones = jnp.ones((ONES_ROWS, ts), BF16)
    for hh in range(A_HEADS):
        vat_ref[0, hh, 0:A_V_DIM, rows] = pt[512 + hh * A_V_DIM:512 + (hh + 1) * A_V_DIM].astype(BF16)
        vat_ref[0, hh, A_V_DIM:A_V_DIM + ONES_ROWS, rows] = ones

    cos = cos_ref[:, rows]
    sin = sin_ref[:, rows]
    qg = qg_ref[...]
    for hh in range(B_HEADS):
        xh = pt[1024 + hh * 64:1024 + (hh + 1) * 64]
        qbt_ref[0, hh * 64:(hh + 1) * 64, rows] = (_rope_t(_norm_t(xh, qg), cos, sin) * Q_SCALE).astype(BF16)
    kg = kg_ref[...]
    kparts = []
    for n in range(B_KV_HEADS):
        xh = pt[1536 + n * 64:1536 + (n + 1) * 64]
        kparts.append(_rope_t(_norm_t(xh, kg), cos, sin))
    kb_ref[0, rows, :] = jnp.concatenate(kparts, axis=0).T.astype(BF16)
    for n in range(B_KV_HEADS):
        vbt_ref[0, n, 0:64, rows] = pt[1664 + n * 64:1664 + (n + 1) * 64].astype(BF16)
        vbt_ref[0, n, 64:64 + ONES_ROWS, rows] = ones


def _inproj(x, mod, ng, wn, wt, cos_t, sin_t, qg_b, kg_b, tm):
    b, s, _ = x.shape
    grid = (b, s // tm)
    ts = qg_b.shape[1]
    const2 = lambda i, j: (0, 0)
    return pl.pallas_call(
        _inproj_kernel,
        grid=grid,
        in_specs=[
            pl.BlockSpec((1, tm, D_MODEL), lambda i, j: (i, j, 0)),
            pl.BlockSpec((1, 3, D_MODEL), lambda i, j: (i, 0, 0)),
            pl.BlockSpec((1, D_MODEL), const2),
            pl.BlockSpec((D_MODEL, N_NAT), const2, pipeline_mode=pl.Buffered(1)),
            pl.BlockSpec((N_TR, D_MODEL), const2, pipeline_mode=pl.Buffered(1)),
            pl.BlockSpec((B_HEAD_DIM, tm), lambda i, j: (0, j)),
            pl.BlockSpec((B_HEAD_DIM, tm), lambda i, j: (0, j)),
            pl.BlockSpec((B_HEAD_DIM, ts), const2),
            pl.BlockSpec((B_HEAD_DIM, ts), const2),
        ],
        out_specs=[
            pl.BlockSpec((1, tm, A_WIDTH), lambda i, j: (i, j, 0)),
            pl.BlockSpec((1, tm, D_MODEL), lambda i, j: (i, j, 0)),
            pl.BlockSpec((1, A_WIDTH, tm), lambda i, j: (i, 0, j)),
            pl.BlockSpec((1, A_HEADS, A_V_DIM + ONES_ROWS, tm), lambda i, j: (i, 0, 0, j)),
            pl.BlockSpec((1, B_WIDTH, tm), lambda i, j: (i, 0, j)),
            pl.BlockSpec((1, tm, B_KV_HEADS * B_HEAD_DIM), lambda i, j: (i, j, 0)),
            pl.BlockSpec((1, B_KV_HEADS, B_HEAD_DIM + ONES_ROWS, tm), lambda i, j: (i, 0, 0, j)),
        ],
        out_shape=[
            jax.ShapeDtypeStruct((b, s, A_WIDTH), BF16),
            jax.ShapeDtypeStruct((b, s, D_MODEL), F32),
            jax.ShapeDtypeStruct((b, A_WIDTH, s), BF16),
            jax.ShapeDtypeStruct((b, A_HEADS, A_V_DIM + ONES_ROWS, s), BF16),
            jax.ShapeDtypeStruct((b, B_WIDTH, s), BF16),
            jax.ShapeDtypeStruct((b, s, B_KV_HEADS * B_HEAD_DIM), BF16),
            jax.ShapeDtypeStruct((b, B_KV_HEADS, B_HEAD_DIM + ONES_ROWS, s), BF16),
        ],
        compiler_params=pltpu.CompilerParams(
            dimension_semantics=("arbitrary", "arbitrary"), vmem_limit_bytes=VMEM_LIMIT_BYTES),
        name="inproj",
    )(x, mod, ng, wn, wt, cos_t, sin_t, qg_b, kg_b)


def _loop_shape(n_chunks, n_tiles, loop_units):
    if n_chunks <= LOOP_BODY_CHUNKS // 2:
        unroll, units = n_chunks, UNITS_PER_STEP
    else:
        unroll, units = LOOP_BODY_CHUNKS // loop_units, loop_units
    assert n_chunks % unroll == 0 and unroll % 2 == 0
    while n_tiles % units:
        units //= 2
    return unroll, units


def _run_tiles(units, unroll, n_chunks, setup, sub_step, finalize):
    if n_chunks == unroll:
        for unit in range(units):
            carry = setup(unit)
            for u in range(unroll):
                carry = sub_step(unit, u, 0, carry)
            finalize(unit)
        return
    carries = tuple(setup(unit) for unit in range(units))

    def body(i, carries):
        carries = list(carries)
        for u in range(unroll):
            for unit in range(units):
                carries[unit] = sub_step(unit, u, unroll * i, carries[unit])
        return tuple(carries)

    lax.fori_loop(0, n_chunks // unroll, body, carries)
    for unit in range(units):
        finalize(unit)


def _attn_a_kernel(scal_ref, qt_ref, k_ref, vt_ref, bias_ref, sg_ref, gate_ref, o_ref, *scratch,
                   tq, tk, n_chunks, ratio, unroll, units, out_scale):
    lam = scal_ref[0]
    log_ratio = ratio.bit_length() - 1
    wq_refs = [scratch[4 * unit] for unit in range(units)]
    acc_refs = [scratch[4 * unit + 1] for unit in range(units)]
    s_bufs = [scratch[4 * unit + 2:4 * unit + 4] for unit in range(units)]
    tiles = [pl.program_id(2) * units + unit for unit in range(units)]

    def qk(unit, c, s_ref):
        start = pl.multiple_of(c * tk, tk)
        d = c - lax.shift_right_logical(tiles[unit], log_ratio)
        r = lax.bitwise_and(tiles[unit], ratio - 1)
        tile = jnp.where(d < -1, 3 * ratio, jnp.where(d > 1, 3 * ratio + 1, (d + 1) * ratio + r))
        t = bias_ref[0, tile]
        s = jnp.dot(k_ref[0, pl.ds(start, tk), :], wq_refs[unit][...], preferred_element_type=F32)
        s = s + jnp.concatenate([t, t], axis=1)
        s_ref[...] = s
        return jnp.max(s, axis=0, keepdims=True)

    def softmax_pv(unit, s_ref, mc, m, c):
        mn = jnp.maximum(m, mc)
        alpha = jnp.exp2(m - mn)
        p = jnp.exp2(s_ref[...] - mn).astype(BF16)
        start = pl.multiple_of(c * tk, tk)
        pv = jnp.dot(vt_ref[0, 0, :, pl.ds(start, tk)], p, preferred_element_type=F32)
        acc_refs[unit][...] = acc_refs[unit][...] * alpha + pv
        return mn

    def setup(unit):
        q = qt_ref[0, :, unit * tq:(unit + 1) * tq]
        zero = jnp.zeros((A_HEAD_DIM, tq), BF16)
        wq_refs[unit][0:64, 0:tq] = q[0:64]
        wq_refs[unit][0:64, tq:2 * tq] = zero
        wq_refs[unit][64:128, 0:tq] = zero
        wq_refs[unit][64:128, tq:2 * tq] = q[64:128]
        acc_refs[unit][...] = jnp.zeros(acc_refs[unit].shape, F32)
        return jnp.full((1, 2 * tq), M_INIT, F32), qk(unit, 0, s_bufs[unit][0])

    def sub_step(unit, u, c0, carry):
        m, mc = carry
        mc_next = qk(unit, jnp.minimum(c0 + u + 1, n_chunks - 1), s_bufs[unit][(u + 1) % 2])
        return softmax_pv(unit, s_bufs[unit][u % 2], mc, m, c0 + u), mc_next

    def finalize(unit):
        acc = acc_refs[unit][...]
        o = acc[0:A_V_DIM, :] * (1.0 / acc[A_V_DIM:A_V_DIM + 1, :])
        w = o[:, 0:tq] - lam * o[:, tq:2 * tq]
        y = _norm_t(w, sg_ref[...]) * out_scale
        rows = slice(unit * tq, (unit + 1) * tq)
        o_ref[0, rows, :] = (y.T * gate_ref[0, rows, :]).astype(BF16)

    _run_tiles(units, unroll, n_chunks, setup, sub_step, finalize)


def _attn_a(scal, qat, ka, vat, bias_tiles, sg_b, gate, tq, tk, out_scale):
    b, s, _ = ka.shape
    n_chunks = s // tk
    ratio = tk // tq
    unroll, units = _loop_shape(n_chunks, s // tq, loop_units=2)
    assert n_chunks % unroll == 0 and ratio * tq == tk and ratio & (ratio - 1) == 0
    kern = functools.partial(_attn_a_kernel, tq=tq, tk=tk, n_chunks=n_chunks, ratio=ratio, unroll=unroll,
                             units=units, out_scale=out_scale)
    return pl.pallas_call(
        kern,
        grid=(b, A_HEADS, s // (tq * units)),
        in_specs=[
            pl.BlockSpec(memory_space=pltpu.SMEM),
            pl.BlockSpec((1, 2 * A_HEAD_DIM, tq * units), lambda i, hh, j: (i, hh, j)),
            pl.BlockSpec((1, s, 2 * A_HEAD_DIM), lambda i, hh, j: (i, 0, hh)),
            pl.BlockSpec((1, 1, A_V_DIM + ONES_ROWS, s), lambda i, hh, j: (i, hh, 0, 0)),
            pl.BlockSpec((1, 3 * ratio + 2, tk, tq), lambda i, hh, j: (hh, 0, 0, 0)),
            pl.BlockSpec((A_V_DIM, tq), lambda i, hh, j: (0, 0)),
            pl.BlockSpec((1, tq * units, A_V_DIM), lambda i, hh, j: (i, j, hh)),
        ],
        out_specs=pl.BlockSpec((1, tq * units, A_V_DIM), lambda i, hh, j: (i, j, hh)),
        out_shape=jax.ShapeDtypeStruct((b, s, A_WIDTH), BF16),
        scratch_shapes=[
            pltpu.VMEM((2 * A_HEAD_DIM, 2 * tq), BF16),
            pltpu.VMEM((A_V_DIM + ONES_ROWS, 2 * tq), F32),
            pltpu.VMEM((tk, 2 * tq), F32),
            pltpu.VMEM((tk, 2 * tq), F32),
        ] * units,
        compiler_params=pltpu.CompilerParams(
            dimension_semantics=("arbitrary", "arbitrary", "arbitrary"), vmem_limit_bytes=VMEM_LIMIT_BYTES),
        name="attn_a",
    )(scal, qat, ka, vat, bias_tiles, sg_b, gate)


def _attn_b_kernel(qt_ref, k_ref, vt_ref, gate_ref, o_ref, *scratch, tq, tk, n_chunks, unroll, units):
    half = B_GROUP * tq
    lanes = B_HEADS * tq
    wq_refs = [scratch[5 * unit] for unit in range(units)]
    acc_refs = [scratch[5 * unit + 1:5 * unit + 3] for unit in range(units)]
    s_bufs = [scratch[5 * unit + 3:5 * unit + 5] for unit in range(units)]

    def qk(unit, c, s_ref):
        start = pl.multiple_of(c * tk, tk)
        s = jnp.dot(k_ref[0, pl.ds(start, tk), :], wq_refs[unit][...], preferred_element_type=F32)
        s_ref[...] = s
        return jnp.max(s, axis=0, keepdims=True)

    def softmax_pv(unit, s_ref, mc, m, c):
        acc0_ref, acc1_ref = acc_refs[unit]
        mn = jnp.maximum(m, mc)
        alpha = jnp.exp2(m - mn)
        p = jnp.exp2(s_ref[...] - mn).astype(BF16)
        start = pl.multiple_of(c * tk, tk)
        v0 = vt_ref[0, 0, :, pl.ds(start, tk)]
        v1 = vt_ref[0, 1, :, pl.ds(start, tk)]
        acc0_ref[...] = acc0_ref[...] * alpha[:, 0:half] + jnp.dot(v0, p[:, 0:half], preferred_element_type=F32)
        acc1_ref[...] = acc1_ref[...] * alpha[:, half:] + jnp.dot(v1, p[:, half:], preferred_element_type=F32)
        return mn

    def setup(unit):
        wq_refs[unit][...] = jnp.zeros(wq_refs[unit].shape, BF16)
        for hh in range(B_HEADS):
            n = hh // B_GROUP
            wq_refs[unit][n * 64:(n + 1) * 64, hh * tq:(hh + 1) * tq] = qt_ref[0, hh * 64:(hh + 1) * 64,
                                                                               unit * tq:(unit + 1) * tq]
        for acc_ref in acc_refs[unit]:
            acc_ref[...] = jnp.zeros(acc_ref.shape, F32)
        return jnp.full((1, lanes), M_INIT, F32), qk(unit, 0, s_bufs[unit][0])

    def sub_step(unit, u, c0, carry):
        m, mc = carry
        mc_next = qk(unit, jnp.minimum(c0 + u + 1, n_chunks - 1), s_bufs[unit][(u + 1) % 2])
        return softmax_pv(unit, s_bufs[unit][u % 2], mc, m, c0 + u), mc_next

    def finalize(unit):
        parts = []
        for acc_ref in acc_refs[unit]:
            acc = acc_ref[...]
            o = acc[0:64, :] * (1.0 / acc[64:65, :])
            for g in range(B_GROUP):
                parts.append(o[:, g * tq:(g + 1) * tq])
        rows = slice(unit * tq, (unit + 1) * tq)
        out = jnp.concatenate(parts, axis=0).T
        o_ref[0, rows, :] = (out * gate_ref[0, rows, :]).astype(BF16)

    _run_tiles(units, unroll, n_chunks, setup, sub_step, finalize)


def _attn_b(qbt, kb, vbt, gate, tq, tk):
    b, s, _ = kb.shape
    n_chunks = s // tk
    unroll, units = _loop_shape(n_chunks, s // tq, loop_units=1)
    kern = functools.partial(_attn_b_kernel, tq=tq, tk=tk, n_chunks=n_chunks, unroll=unroll, units=units)
    return pl.pallas_call(
        kern,
        grid=(b, s // (tq * units)),
        in_specs=[
            pl.BlockSpec((1, B_WIDTH, tq * units), lambda i, j: (i, 0, j)),
            pl.BlockSpec((1, s, B_KV_HEADS * B_HEAD_DIM), lambda i, j: (i, 0, 0)),
            pl.BlockSpec((1, B_KV_HEADS, B_HEAD_DIM + ONES_ROWS, s), lambda i, j: (i, 0, 0, 0)),
            pl.BlockSpec((1, tq * units, B_WIDTH), lambda i, j: (i, j, 1)),
        ],
        out_specs=pl.BlockSpec((1, tq * units, B_WIDTH), lambda i, j: (i, j, 0)),
        out_shape=jax.ShapeDtypeStruct((b, s, B_WIDTH), BF16),
        scratch_shapes=[
            pltpu.VMEM((B_KV_HEADS * B_HEAD_DIM, B_HEADS * tq), BF16),
            pltpu.VMEM((B_HEAD_DIM + ONES_ROWS, B_GROUP * tq), F32),
            pltpu.VMEM((B_HEAD_DIM + ONES_ROWS, B_GROUP * tq), F32),
            pltpu.VMEM((tk, B_HEADS * tq), F32),
            pltpu.VMEM((tk, B_HEADS * tq), F32),
        ] * units,
        compiler_params=pltpu.CompilerParams(
            dimension_semantics=("arbitrary", "arbitrary"), vmem_limit_bytes=VMEM_LIMIT_BYTES),
        name="attn_b",
    )(qbt, kb, vbt, gate)


def _outproj_kernel(ma_ref, mb_ref, x_ref, mod_ref, w_ref, fg_ref, o_ref, *, final):
    y = (jnp.dot(ma_ref[0], w_ref[0:A_WIDTH, :], preferred_element_type=F32)
         + jnp.dot(mb_ref[0], w_ref[A_WIDTH:D_MODEL, :], preferred_element_type=F32))
    xn = x_ref[0] + mod_ref[0, 2:3, :] * y
    if final:
        xn = _rms_rows(xn, fg_ref[...])
    o_ref[0] = xn


def _outproj(ma, mb, x, mod, w, fg, tm, final):
    b, s, _ = x.shape
    tok = lambda i, j: (i, j, 0)
    return pl.pallas_call(
        functools.partial(_outproj_kernel, final=final),
        grid=(b, s // tm),
        in_specs=[
            pl.BlockSpec((1, tm, A_WIDTH), tok),
            pl.BlockSpec((1, tm, B_WIDTH), tok),
            pl.BlockSpec((1, tm, D_MODEL), tok),
            pl.BlockSpec((1, 3, D_MODEL), lambda i, j: (i, 0, 0)),
            pl.BlockSpec((D_MODEL, D_MODEL), lambda i, j: (0, 0)),
            pl.BlockSpec((1, D_MODEL), lambda i, j: (0, 0)),
        ],
        out_specs=pl.BlockSpec((1, tm, D_MODEL), tok),
        out_shape=jax.ShapeDtypeStruct((b, s, D_MODEL), F32),
        compiler_params=pltpu.CompilerParams(
            dimension_semantics=("arbitrary", "arbitrary"), vmem_limit_bytes=VMEM_LIMIT_BYTES),
        name="outproj",
    )(ma, mb, x, mod, w, fg)


def _t5_bucket(rel):
    half = NUM_BUCKETS // 2
    max_exact = half // 2
    ret = jnp.where(rel > 0, half, 0)
    n = jnp.abs(rel)
    nf = jnp.maximum(n, 1).astype(jnp.float32)
    large = max_exact + (jnp.log(nf / max_exact) / math.log(MAX_DISTANCE / max_exact)
                         * (half - max_exact)).astype(jnp.int32)
    large = jnp.minimum(large, half - 1)
    return ret + jnp.where(n < max_exact, n, large)


def _bias_tables(rel_table, s, tq, tk):
    ratio = tk // tq
    assert tk >= MAX_DISTANCE and s - 1 >= MAX_DISTANCE
    offsets = jnp.arange(-MAX_DISTANCE, MAX_DISTANCE + 1, dtype=jnp.int32)
    core = rel_table[_t5_bucket(offsets)].T.astype(F32) * LOG2E
    reach = 2 * tk + tq
    boff_p = jnp.pad(core, ((0, 0), (reach - MAX_DISTANCE, reach - MAX_DISTANCE)), mode="edge")
    span = tk + tq - 1
    tiles = []
    for dc in (-1, 0, 1):
        for r in range(ratio):
            first = dc * tk - r * tq - (tq - 1) + reach
            u = jnp.flip(boff_p[:, first:first + span], axis=1)
            hankel = jnp.tile(u, (1, tk + 1))[:, :tk * (span + 1)].reshape(A_HEADS, tk, span + 1)[:, :, :tq]
            tiles.append(jnp.flip(hankel, axis=1))
    for col in (0, 2 * reach):
        tiles.append(jnp.broadcast_to(boff_p[:, col][:, None, None], (A_HEADS, tk, tq)))
    return jnp.stack(tiles, axis=1)


def _rope_tables_t(s):
    rows = s // GRID_W
    row = jnp.repeat(jnp.arange(rows), GRID_W).astype(F32)
    col = jnp.tile(jnp.arange(GRID_W), rows).astype(F32)
    axis_dim = B_HEAD_DIM // 2
    inv_freq = ROPE_THETA ** (-jnp.arange(0, axis_dim, 2, dtype=F32) / axis_dim)
    ang_r = row[:, None] * inv_freq[None, :]
    ang_c = col[:, None] * inv_freq[None, :]
    ang = jnp.concatenate([ang_r, ang_r, ang_c, ang_c], axis=-1)
    return jnp.cos(ang).T, jnp.sin(ang).T


def _tile_sizes(s):
    del s
    return (512, 1024), 1024, (256, 512), (128, 256)


def _trunk(x, mod, lam, rel_table, norm_g, wn, wt, subln_g, q_norm_g, k_norm_g, w_out, final_g):
    b, s, _ = x.shape
    (ts_in, tm), tm_out, (tq_a, tk_a), (tq_b, tk_b) = _tile_sizes(s)
    cos_t, sin_t = _rope_tables_t(s)
    bias_tiles = _bias_tables(rel_table, s, tq_a, tk_a)
    fg = final_g.reshape(1, D_MODEL)
    for l in range(DEPTH):
        lam_init = 0.8 - 0.6 * math.exp(-0.3 * l)
        qg_b = jnp.broadcast_to(q_norm_g[l][:, None], (B_HEAD_DIM, ts_in))
        kg_b = jnp.broadcast_to(k_norm_g[l][:, None], (B_HEAD_DIM, ts_in))
        sg_b = jnp.broadcast_to(subln_g[l][:, None], (A_V_DIM, tq_a))
        ka, gate, qat, vat, qbt, kb, vbt = _inproj(
            x, mod[l], norm_g[l].reshape(1, D_MODEL), wn[l], wt[l], cos_t, sin_t, qg_b, kg_b, tm)
        ma = _attn_a(lam[l:l + 1], qat, ka, vat, bias_tiles, sg_b, gate, tq_a, tk_a, 1.0 - lam_init)
        mb = _attn_b(qbt, kb, vbt, gate, tq_b, tk_b)
        x = _outproj(ma, mb, x, mod[l], w_out[l], fg, tm_out, final=(l == DEPTH - 1))
    return x


def kernel(x_prompt, x_sample, c_prompt, c_sample, rel_table, norm_g, w_ada, b_ada, w_in, lam_q1, lam_k1, lam_q2,
           lam_k2, subln_g, q_norm_g, k_norm_g, w_out, final_g):
    bp = x_prompt.shape[0]
    bs = x_sample.shape[0]
    rows = -(-(bp + bs) // 8) * 8
    c_all = jnp.concatenate([c_prompt, c_sample, jnp.zeros((rows - bp - bs, D_MODEL), F32)], axis=0)
    mod_all = _adaln_mod(c_all, w_ada.astype(BF16), b_ada)
    mod_p = mod_all[:, :bp].reshape(DEPTH, bp, 3, D_MODEL)
    mod_s = mod_all[:, bp:bp + bs].reshape(DEPTH, bs, 3, D_MODEL)
    lam = _lambdas(lam_q1, lam_k1, lam_q2, lam_k2)

    w_bf = w_in.astype(BF16)
    wn = jnp.concatenate([w_bf[:, :, _C_KA:_C_VA], w_bf[:, :, _C_GA:_C_QB], w_bf[:, :, _C_GB:D_IN]], axis=2)
    wt = jnp.concatenate([w_bf[:, :, _C_QA:_C_KA], w_bf[:, :, _C_VA:_C_GA], w_bf[:, :, _C_QB:_C_GB]], axis=2)
    wt = jnp.swapaxes(wt, 1, 2)
    w_out_bf = w_out.astype(BF16)

    args = (rel_table, norm_g, wn, wt, subln_g, q_norm_g, k_norm_g, w_out_bf, final_g)
    y_prompt = _trunk(x_prompt, mod_p, lam, *args)
    y_sample = _trunk(x_sample, mod_s, lam, *args)
    return (y_prompt, y_sample)
```

```python
import functools
import math

import jax
import jax.numpy as jnp
from jax import lax
from jax.experimental import pallas as pl
from jax.experimental.pallas import tpu as pltpu

F32 = jnp.float32
BF16 = jnp.bfloat16

D_MODEL = 1024
DEPTH = 4
GRID_W = 64
A_WIDTH = 512
A_V_DIM = 128
A_HEAD_DIM = 64
A_HEADS = 4
B_WIDTH = 512
B_HEAD_DIM = 64
B_HEADS = 8
B_KV_HEADS = 2
B_GROUP = 4
NUM_BUCKETS = 32
MAX_DISTANCE = 128
ROPE_THETA = 10000.0
EPS = 1e-6

LOG2E = 1.4426950408889634
Q_SCALE = (A_HEAD_DIM ** -0.5) * LOG2E
M_INIT = -1e30
ONES_ROWS = 16
VMEM_LIMIT_BYTES = 56 * 1024 * 1024
UNITS_PER_STEP = 8
LOOP_BODY_CHUNKS = 16

_C_QA, _C_KA, _C_VA, _C_GA, _C_QB, _C_KB, _C_VB, _C_GB = 0, 512, 1024, 1536, 2048, 2560, 2688, 2816
D_IN = 3328
N_NAT = 1536
N_TR = 1792


def _rms_rows(x, g):
    ms = jnp.mean(x * x, axis=-1, keepdims=True)
    return x * lax.rsqrt(ms + EPS) * g


def _mod_kernel(c_ref, w_ref, b_ref, o_ref):
    c = c_ref[...]
    c_act = (c * jax.nn.sigmoid(c)).astype(BF16)
    o_ref[0] = jnp.dot(c_act, w_ref[0], preferred_element_type=F32) + b_ref[0]


def _adaln_mod(c_all, w_ada_bf, b_ada):
    rows = c_all.shape[0]
    return pl.pallas_call(
        _mod_kernel,
        grid=(DEPTH, 3),
        in_specs=[
            pl.BlockSpec((rows, D_MODEL), lambda l, j: (0, 0)),
            pl.BlockSpec((1, D_MODEL, D_MODEL), lambda l, j: (l, 0, j)),
            pl.BlockSpec((1, 1, D_MODEL), lambda l, j: (l, 0, j)),
        ],
        out_specs=pl.BlockSpec((1, rows, D_MODEL), lambda l, j: (l, 0, j)),
        out_shape=jax.ShapeDtypeStruct((DEPTH, rows, 3 * D_MODEL), F32),
        name="adaln_mod",
    )(c_all, w_ada_bf, b_ada.reshape(DEPTH, 1, 3 * D_MODEL))


def _lam_kernel(q1_ref, k1_ref, q2_ref, k2_ref, init_ref, o_ref):
    s1 = jnp.sum(q1_ref[...] * k1_ref[...], axis=-1, keepdims=True)
    s2 = jnp.sum(q2_ref[...] * k2_ref[...], axis=-1, keepdims=True)
    lam = jnp.exp(s1) - jnp.exp(s2) + init_ref[...][:, 0:1]
    o_ref[...] = jnp.broadcast_to(lam, o_ref.shape)


def _lambdas(lam_q1, lam_k1, lam_q2, lam_k2):
    init = jnp.asarray([[0.8 - 0.6 * math.exp(-0.3 * l)] * 128 for l in range(DEPTH)], F32)
    out = pl.pallas_call(
        _lam_kernel,
        out_shape=jax.ShapeDtypeStruct((DEPTH, 128), F32),
        name="diff_lambda",
    )(lam_q1, lam_k1, lam_q2, lam_k2, init)
    return out[:, 0]


def _rope_t(x, cos, sin):
    rot = jnp.concatenate([-x[16:32], x[0:16], -x[48:64], x[32:48]], axis=0)
    return x * cos + rot * sin


def _norm_t(x, g):
    ms = jnp.mean(x * x, axis=0, keepdims=True)
    return x * lax.rsqrt(ms + EPS) * g


def _inproj_kernel(x_ref, mod_ref, ng_ref, wn_ref, wt_ref, cos_ref, sin_ref, qg_ref, kg_ref,
                   ka_ref, gate_ref, qat_ref, vat_ref, qbt_ref, kb_ref, vbt_ref):
    ts = qg_ref.shape[1]
    for sub in range(x_ref.shape[1] // ts):
        rows = slice(sub * ts, (sub + 1) * ts)
        _inproj_subtile(rows, x_ref[0, rows, :], mod_ref, ng_ref, wn_ref, wt_ref, cos_ref, sin_ref,
                        qg_ref, kg_ref, ka_ref, gate_ref, qat_ref, vat_ref, qbt_ref, kb_ref, vbt_ref)


def _mid_kernel(ma_ref, mb_ref, x_ref, modp_ref, wo_ref, mod_ref, ng_ref, wn_ref, wt_ref, cos_ref, sin_ref,
                qg_ref, kg_ref, xn_ref, ka_ref, gate_ref, qat_ref, vat_ref, qbt_ref, kb_ref, vbt_ref):
    ts = qg_ref.shape[1]
    for sub in range(x_ref.shape[1] // ts):
        rows = slice(sub * ts, (sub + 1) * ts)
        y = (jnp.dot(ma_ref[0, rows, :], wo_ref[0:A_WIDTH, :], preferred_element_type=F32)
             + jnp.dot(mb_ref[0, rows, :], wo_ref[A_WIDTH:D_MODEL, :], preferred_element_type=F32))
        xn = x_ref[0, rows, :] + modp_ref[0, 2:3, :] * y
        xn_ref[0, rows, :] = xn
        _inproj_subtile(rows, xn, mod_ref, ng_ref, wn_ref, wt_ref, cos_ref, sin_ref,
                        qg_ref, kg_ref, ka_ref, gate_ref, qat_ref, vat_ref, qbt_ref, kb_ref, vbt_ref)


def _inproj_subtile(rows, x, mod_ref, ng_ref, wn_ref, wt_ref, cos_ref, sin_ref, qg_ref, kg_ref,
                    ka_ref, gate_ref, qat_ref, vat_ref, qbt_ref, kb_ref, vbt_ref):
    ts = rows.stop - rows.start
    shift = mod_ref[0, 0:1, :]
    scale = mod_ref[0, 1:2, :]
    h = (_rms_rows(x, ng_ref[...]) * (1.0 + scale) + shift).astype(BF16)

    pn = jnp.dot(h, wn_ref[...], preferred_element_type=F32)
    ka_ref[0, rows, :] = pn[:, 0:512].astype(BF16)
    g = pn[:, 512:N_NAT]
    gate_ref[0, rows, :] = g * jax.nn.sigmoid(g)

    pt = lax.dot_general(wt_ref[...], h, (((1,), (1,)), ((), ())), preferred_element_type=F32)
    qat_ref[0, :, rows] = (pt[0:512] * Q_SCALE).astype(BF16)
    ones = jnp.ones((ONES_ROWS, ts), BF16)
    for hh in range(A_HEADS):
        vat_ref[0, hh, 0:A_V_DIM, rows] = pt[512 + hh * A_V_DIM:512 + (hh + 1) * A_V_DIM].astype(BF16)
        vat_ref[0, hh, A_V_DIM:A_V_DIM + ONES_ROWS, rows] = ones

    cos = cos_ref[:, rows]
    sin = sin_ref[:, rows]
    qg = qg_ref[...]
    for hh in range(B_HEADS):
        xh = pt[1024 + hh * 64:1024 + (hh + 1) * 64]
        qbt_ref[0, hh * 64:(hh + 1) * 64, rows] = (_rope_t(_norm_t(xh, qg), cos, sin) * Q_SCALE).astype(BF16)
    kg = kg_ref[...]
    kparts = []
    for n in range(B_KV_HEADS):
        xh = pt[1536 + n * 64:1536 + (n + 1) * 64]
        kparts.append(_rope_t(_norm_t(xh, kg), cos, sin))
    kb_ref[0, rows, :] = jnp.concatenate(kparts, axis=0).T.astype(BF16)
    for n in range(B_KV_HEADS):
        vbt_ref[0, n, 0:64, rows] = pt[1664 + n * 64:1664 + (n + 1) * 64].astype(BF16)
        vbt_ref[0, n, 64:64 + ONES_ROWS, rows] = ones


def _inproj(x, mod, ng, wn, wt, cos_t, sin_t, qg_b, kg_b, tm, prev=None):
    b, s, _ = x.shape
    grid = (b, s // tm)
    ts = qg_b.shape[1]
    const2 = lambda i, j: (0, 0)
    tok = lambda i, j: (i, j, 0)
    per_batch = lambda i, j: (i, 0, 0)
    fused_in, fused_args, fused_out_specs, fused_out_shape = [], [], [], []
    if prev is not None:
        ma, mb, mod_prev, w_prev = prev
        fused_in = [pl.BlockSpec((1, tm, A_WIDTH), tok), pl.BlockSpec((1, tm, B_WIDTH), tok)]
        fused_args = [ma, mb]
        fused_out_specs = [pl.BlockSpec((1, tm, D_MODEL), tok)]
        fused_out_shape = [jax.ShapeDtypeStruct((b, s, D_MODEL), F32)]
    return pl.pallas_call(
        _inproj_kernel if prev is None else _mid_kernel,
        grid=grid,
        in_specs=fused_in + [
            pl.BlockSpec((1, tm, D_MODEL), tok),
        ] + ([] if prev is None else [
            pl.BlockSpec((1, 3, D_MODEL), per_batch),
            pl.BlockSpec((D_MODEL, D_MODEL), const2, pipeline_mode=pl.Buffered(1)),
        ]) + [
            pl.BlockSpec((1, 3, D_MODEL), per_batch),
            pl.BlockSpec((1, D_MODEL), const2),
            pl.BlockSpec((D_MODEL, N_NAT), const2, pipeline_mode=pl.Buffered(1)),
            pl.BlockSpec((N_TR, D_MODEL), const2, pipeline_mode=pl.Buffered(1)),
            pl.BlockSpec((B_HEAD_DIM, tm), lambda i, j: (0, j)),
            pl.BlockSpec((B_HEAD_DIM, tm), lambda i, j: (0, j)),
            pl.BlockSpec((B_HEAD_DIM, ts), const2),
            pl.BlockSpec((B_HEAD_DIM, ts), const2),
        ],
        out_specs=fused_out_specs + [
            pl.BlockSpec((1, tm, A_WIDTH), lambda i, j: (i, j, 0)),
            pl.BlockSpec((1, tm, D_MODEL), lambda i, j: (i, j, 0)),
            pl.BlockSpec((1, A_WIDTH, tm), lambda i, j: (i, 0, j)),
            pl.BlockSpec((1, A_HEADS, A_V_DIM + ONES_ROWS, tm), lambda i, j: (i, 0, 0, j)),
            pl.BlockSpec((1, B_WIDTH, tm), lambda i, j: (i, 0, j)),
            pl.BlockSpec((1, tm, B_KV_HEADS * B_HEAD_DIM), lambda i, j: (i, j, 0)),
            pl.BlockSpec((1, B_KV_HEADS, B_HEAD_DIM + ONES_ROWS, tm), lambda i, j: (i, 0, 0, j)),
        ],
        out_shape=fused_out_shape + [
            jax.ShapeDtypeStruct((b, s, A_WIDTH), BF16),
            jax.ShapeDtypeStruct((b, s, D_MODEL), F32),
            jax.ShapeDtypeStruct((b, A_WIDTH, s), BF16),
            jax.ShapeDtypeStruct((b, A_HEADS, A_V_DIM + ONES_ROWS, s), BF16),
            jax.ShapeDtypeStruct((b, B_WIDTH, s), BF16),
            jax.ShapeDtypeStruct((b, s, B_KV_HEADS * B_HEAD_DIM), BF16),
            jax.ShapeDtypeStruct((b, B_KV_HEADS, B_HEAD_DIM + ONES_ROWS, s), BF16),
        ],
        compiler_params=pltpu.CompilerParams(
            dimension_semantics=("arbitrary", "arbitrary"), vmem_limit_bytes=VMEM_LIMIT_BYTES),
        name="inproj" if prev is None else "outproj_inproj",
    )(*fused_args, x, *(() if prev is None else (mod_prev, w_prev)), mod, ng, wn, wt, cos_t, sin_t, qg_b, kg_b)


def _loop_shape(n_chunks, n_tiles, loop_units):
    if n_chunks <= LOOP_BODY_CHUNKS // 2:
        unroll, units = n_chunks, UNITS_PER_STEP
    else:
        unroll, units = LOOP_BODY_CHUNKS // loop_units, loop_units
    assert n_chunks % unroll == 0 and unroll % 2 == 0
    while n_tiles % units:
        units //= 2
    return unroll, units


def _run_tiles(units, unroll, n_chunks, setup, sub_step, finalize):
    if n_chunks == unroll:
        for unit in range(units):
            carry = setup(unit)
            for u in range(unroll):
                carry = sub_step(unit, u, 0, carry)
            finalize(unit)
        return
    carries = tuple(setup(unit) for unit in range(units))

    def body(i, carries):
        carries = list(carries)
        for u in range(unroll):
            for unit in range(units):
                carries[unit] = sub_step(unit, u, unroll * i, carries[unit])
        return tuple(carries)

    lax.fori_loop(0, n_chunks // unroll, body, carries)
    for unit in range(units):
        finalize(unit)


def _attn_a_kernel(scal_ref, qt_ref, k_ref, vt_ref, bias_ref, sg_ref, gate_ref, o_ref, *scratch,
                   tq, tk, n_chunks, ratio, unroll, units, out_scale):
    lam = scal_ref[0]
    log_ratio = ratio.bit_length() - 1
    wq_refs = [scratch[4 * unit] for unit in range(units)]
    acc_refs = [scratch[4 * unit + 1] for unit in range(units)]
    s_bufs = [scratch[4 * unit + 2:4 * unit + 4] for unit in range(units)]
    tiles = [pl.program_id(2) * units + unit for unit in range(units)]

    def qk(unit, c, s_ref):
        start = pl.multiple_of(c * tk, tk)
        d = c - lax.shift_right_logical(tiles[unit], log_ratio)
        r = lax.bitwise_and(tiles[unit], ratio - 1)
        tile = jnp.where(d < -1, 3 * ratio, jnp.where(d > 1, 3 * ratio + 1, (d + 1) * ratio + r))
        t = bias_ref[0, tile]
        s = jnp.dot(k_ref[0, pl.ds(start, tk), :], wq_refs[unit][...], preferred_element_type=F32)
        s = s + jnp.concatenate([t, t], axis=1)
        s_ref[...] = s
        return jnp.max(s, axis=0, keepdims=True)

    def softmax_pv(unit, s_ref, mc, m, c):
        mn = jnp.maximum(m, mc)
        alpha = jnp.exp2(m - mn)
        p = jnp.exp2(s_ref[...] - mn).astype(BF16)
        start = pl.multiple_of(c * tk, tk)
        pv = jnp.dot(vt_ref[0, 0, :, pl.ds(start, tk)], p, preferred_element_type=F32)
        acc_refs[unit][...] = acc_refs[unit][...] * alpha + pv
        return mn

    def setup(unit):
        q = qt_ref[0, :, unit * tq:(unit + 1) * tq]
        zero = jnp.zeros((A_HEAD_DIM, tq), BF16)
        wq_refs[unit][0:64, 0:tq] = q[0:64]
        wq_refs[unit][0:64, tq:2 * tq] = zero
        wq_refs[unit][64:128, 0:tq] = zero
        wq_refs[unit][64:128, tq:2 * tq] = q[64:128]
        acc_refs[unit][...] = jnp.zeros(acc_refs[unit].shape, F32)
        return jnp.full((1, 2 * tq), M_INIT, F32), qk(unit, 0, s_bufs[unit][0])

    def sub_step(unit, u, c0, carry):
        m, mc = carry
        mc_next = qk(unit, jnp.minimum(c0 + u + 1, n_chunks - 1), s_bufs[unit][(u + 1) % 2])
        return softmax_pv(unit, s_bufs[unit][u % 2], mc, m, c0 + u), mc_next

    def finalize(unit):
        acc = acc_refs[unit][...]
        o = acc[0:A_V_DIM, :] * (1.0 / acc[A_V_DIM:A_V_DIM + 1, :])
        w = o[:, 0:tq] - lam * o[:, tq:2 * tq]
        y = _norm_t(w, sg_ref[...]) * out_scale
        rows = slice(unit * tq, (unit + 1) * tq)
        o_ref[0, rows, :] = (y.T * gate_ref[0, rows, :]).astype(BF16)

    _run_tiles(units, unroll, n_chunks, setup, sub_step, finalize)


def _attn_a(scal, qat, ka, vat, bias_tiles, sg_b, gate, tq, tk, out_scale):
    b, s, _ = ka.shape
    n_chunks = s // tk
    ratio = tk // tq
    unroll, units = _loop_shape(n_chunks, s // tq, loop_units=2)
    assert n_chunks % unroll == 0 and ratio * tq == tk and ratio & (ratio - 1) == 0
    kern = functools.partial(_attn_a_kernel, tq=tq, tk=tk, n_chunks=n_chunks, ratio=ratio, unroll=unroll,
                             units=units, out_scale=out_scale)
    return pl.pallas_call(
        kern,
        grid=(b, A_HEADS, s // (tq * units)),
        in_specs=[
            pl.BlockSpec(memory_space=pltpu.SMEM),
            pl.BlockSpec((1, 2 * A_HEAD_DIM, tq * units), lambda i, hh, j: (i, hh, j)),
            pl.BlockSpec((1, s, 2 * A_HEAD_DIM), lambda i, hh, j: (i, 0, hh)),
            pl.BlockSpec((1, 1, A_V_DIM + ONES_ROWS, s), lambda i, hh, j: (i, hh, 0, 0)),
            pl.BlockSpec((1, 3 * ratio + 2, tk, tq), lambda i, hh, j: (hh, 0, 0, 0)),
            pl.BlockSpec((A_V_DIM, tq), lambda i, hh, j: (0, 0)),
            pl.BlockSpec((1, tq * units, A_V_DIM), lambda i, hh, j: (i, j, hh)),
        ],
        out_specs=pl.BlockSpec((1, tq * units, A_V_DIM), lambda i, hh, j: (i, j, hh)),
        out_shape=jax.ShapeDtypeStruct((b, s, A_WIDTH), BF16),
        scratch_shapes=[
            pltpu.VMEM((2 * A_HEAD_DIM, 2 * tq), BF16),
            pltpu.VMEM((A_V_DIM + ONES_ROWS, 2 * tq), F32),
            pltpu.VMEM((tk, 2 * tq), F32),
            pltpu.VMEM((tk, 2 * tq), F32),
        ] * units,
        compiler_params=pltpu.CompilerParams(
            dimension_semantics=("arbitrary", "arbitrary", "arbitrary"), vmem_limit_bytes=VMEM_LIMIT_BYTES),
        name="attn_a",
    )(scal, qat, ka, vat, bias_tiles, sg_b, gate)


def _attn_b_kernel(qt_ref, k_ref, vt_ref, gate_ref, o_ref, *scratch, tq, tk, n_chunks, unroll, units):
    half = B_GROUP * tq
    lanes = B_HEADS * tq
    wq_refs = [scratch[5 * unit] for unit in range(units)]
    acc_refs = [scratch[5 * unit + 1:5 * unit + 3] for unit in range(units)]
    s_bufs = [scratch[5 * unit + 3:5 * unit + 5] for unit in range(units)]

    def qk(unit, c, s_ref):
        start = pl.multiple_of(c * tk, tk)
        s = jnp.dot(k_ref[0, pl.ds(start, tk), :], wq_refs[unit][...], preferred_element_type=F32)
        s_ref[...] = s
        return jnp.max(s, axis=0, keepdims=True)

    def softmax_pv(unit, s_ref, mc, m, c):
        acc0_ref, acc1_ref = acc_refs[unit]
        mn = jnp.maximum(m, mc)
        alpha = jnp.exp2(m - mn)
        p = jnp.exp2(s_ref[...] - mn).astype(BF16)
        start = pl.multiple_of(c * tk, tk)
        v0 = vt_ref[0, 0, :, pl.ds(start, tk)]
        v1 = vt_ref[0, 1, :, pl.ds(start, tk)]
        acc0_ref[...] = acc0_ref[...] * alpha[:, 0:half] + jnp.dot(v0, p[:, 0:half], preferred_element_type=F32)
        acc1_ref[...] = acc1_ref[...] * alpha[:, half:] + jnp.dot(v1, p[:, half:], preferred_element_type=F32)
        return mn

    def setup(unit):
        wq_refs[unit][...] = jnp.zeros(wq_refs[unit].shape, BF16)
        for hh in range(B_HEADS):
            n = hh // B_GROUP
            wq_refs[unit][n * 64:(n + 1) * 64, hh * tq:(hh + 1) * tq] = qt_ref[0, hh * 64:(hh + 1) * 64,
                                                                               unit * tq:(unit + 1) * tq]
        for acc_ref in acc_refs[unit]:
            acc_ref[...] = jnp.zeros(acc_ref.shape, F32)
        return jnp.full((1, lanes), M_INIT, F32), qk(unit, 0, s_bufs[unit][0])

    def sub_step(unit, u, c0, carry):
        m, mc = carry
        mc_next = qk(unit, jnp.minimum(c0 + u + 1, n_chunks - 1), s_bufs[unit][(u + 1) % 2])
        return softmax_pv(unit, s_bufs[unit][u % 2], mc, m, c0 + u), mc_next

    def finalize(unit):
        parts = []
        for acc_ref in acc_refs[unit]:
            acc = acc_ref[...]
            o = acc[0:64, :] * (1.0 / acc[64:65, :])
            for g in range(B_GROUP):
                parts.append(o[:, g * tq:(g + 1) * tq])
        rows = slice(unit * tq, (unit + 1) * tq)
        out = jnp.concatenate(parts, axis=0).T
        o_ref[0, rows, :] = (out * gate_ref[0, rows, :]).astype(BF16)

    _run_tiles(units, unroll, n_chunks, setup, sub_step, finalize)


def _attn_b(qbt, kb, vbt, gate, tq, tk):
    b, s, _ = kb.shape
    n_chunks = s // tk
    unroll, units = _loop_shape(n_chunks, s // tq, loop_units=1)
    kern = functools.partial(_attn_b_kernel, tq=tq, tk=tk, n_chunks=n_chunks, unroll=unroll, units=units)
    return pl.pallas_call(
        kern,
        grid=(b, s // (tq * units)),
        in_specs=[
            pl.BlockSpec((1, B_WIDTH, tq * units), lambda i, j: (i, 0, j)),
            pl.BlockSpec((1, s, B_KV_HEADS * B_HEAD_DIM), lambda i, j: (i, 0, 0)),
            pl.BlockSpec((1, B_KV_HEADS, B_HEAD_DIM + ONES_ROWS, s), lambda i, j: (i, 0, 0, 0)),
            pl.BlockSpec((1, tq * units, B_WIDTH), lambda i, j: (i, j, 1)),
        ],
        out_specs=pl.BlockSpec((1, tq * units, B_WIDTH), lambda i, j: (i, j, 0)),
        out_shape=jax.ShapeDtypeStruct((b, s, B_WIDTH), BF16),
        scratch_shapes=[
            pltpu.VMEM((B_KV_HEADS * B_HEAD_DIM, B_HEADS * tq), BF16),
            pltpu.VMEM((B_HEAD_DIM + ONES_ROWS, B_GROUP * tq), F32),
            pltpu.VMEM((B_HEAD_DIM + ONES_ROWS, B_GROUP * tq), F32),
            pltpu.VMEM((tk, B_HEADS * tq), F32),
            pltpu.VMEM((tk, B_HEADS * tq), F32),
        ] * units,
        compiler_params=pltpu.CompilerParams(
            dimension_semantics=("arbitrary", "arbitrary"), vmem_limit_bytes=VMEM_LIMIT_BYTES),
        name="attn_b",
    )(qbt, kb, vbt, gate)


def _outproj_kernel(ma_ref, mb_ref, x_ref, mod_ref, w_ref, fg_ref, o_ref, *, final):
    y = (jnp.dot(ma_ref[0], w_ref[0:A_WIDTH, :], preferred_element_type=F32)
         + jnp.dot(mb_ref[0], w_ref[A_WIDTH:D_MODEL, :], preferred_element_type=F32))
    xn = x_ref[0] + mod_ref[0, 2:3, :] * y
    if final:
        xn = _rms_rows(xn, fg_ref[...])
    o_ref[0] = xn


def _outproj(ma, mb, x, mod, w, fg, tm, final):
    b, s, _ = x.shape
    tok = lambda i, j: (i, j, 0)
    return pl.pallas_call(
        functools.partial(_outproj_kernel, final=final),
        grid=(b, s // tm),
        in_specs=[
            pl.BlockSpec((1, tm, A_WIDTH), tok),
            pl.BlockSpec((1, tm, B_WIDTH), tok),
            pl.BlockSpec((1, tm, D_MODEL), tok),
            pl.BlockSpec((1, 3, D_MODEL), lambda i, j: (i, 0, 0)),
            pl.BlockSpec((D_MODEL, D_MODEL), lambda i, j: (0, 0)),
            pl.BlockSpec((1, D_MODEL), lambda i, j: (0, 0)),
        ],
        out_specs=pl.BlockSpec((1, tm, D_MODEL), tok),
        out_shape=jax.ShapeDtypeStruct((b, s, D_MODEL), F32),
        compiler_params=pltpu.CompilerParams(
            dimension_semantics=("arbitrary", "arbitrary"), vmem_limit_bytes=VMEM_LIMIT_BYTES),
        name="outproj",
    )(ma, mb, x, mod, w, fg)


def _t5_bucket(rel):
    half = NUM_BUCKETS // 2
    max_exact = half // 2
    ret = jnp.where(rel > 0, half, 0)
    n = jnp.abs(rel)
    nf = jnp.maximum(n, 1).astype(jnp.float32)
    large = max_exact + (jnp.log(nf / max_exact) / math.log(MAX_DISTANCE / max_exact)
                         * (half - max_exact)).astype(jnp.int32)
    large = jnp.minimum(large, half - 1)
    return ret + jnp.where(n < max_exact, n, large)


def _bias_tables(rel_table, s, tq, tk):
    ratio = tk // tq
    assert tk >= MAX_DISTANCE and s - 1 >= MAX_DISTANCE
    offsets = jnp.arange(-MAX_DISTANCE, MAX_DISTANCE + 1, dtype=jnp.int32)
    core = rel_table[_t5_bucket(offsets)].T.astype(F32) * LOG2E
    reach = 2 * tk + tq
    boff_p = jnp.pad(core, ((0, 0), (reach - MAX_DISTANCE, reach - MAX_DISTANCE)), mode="edge")
    span = tk + tq - 1
    tiles = []
    for dc in (-1, 0, 1):
        for r in range(ratio):
            first = dc * tk - r * tq - (tq - 1) + reach
            u = jnp.flip(boff_p[:, first:first + span], axis=1)
            hankel = jnp.tile(u, (1, tk + 1))[:, :tk * (span + 1)].reshape(A_HEADS, tk, span + 1)[:, :, :tq]
            tiles.append(jnp.flip(hankel, axis=1))
    for col in (0, 2 * reach):
        tiles.append(jnp.broadcast_to(boff_p[:, col][:, None, None], (A_HEADS, tk, tq)))
    return jnp.stack(tiles, axis=1)


def _rope_tables_t(s):
    rows = s // GRID_W
    row = jnp.repeat(jnp.arange(rows), GRID_W).astype(F32)
    col = jnp.tile(jnp.arange(GRID_W), rows).astype(F32)
    axis_dim = B_HEAD_DIM // 2
    inv_freq = ROPE_THETA ** (-jnp.arange(0, axis_dim, 2, dtype=F32) / axis_dim)
    ang_r = row[:, None] * inv_freq[None, :]
    ang_c = col[:, None] * inv_freq[None, :]
    ang = jnp.concatenate([ang_r, ang_r, ang_c, ang_c], axis=-1)
    return jnp.cos(ang).T, jnp.sin(ang).T


def _tile_sizes(s):
    del s
    return (512, 1024), 1024, (256, 512), (128, 256)


def _trunk(x, mod, lam, rel_table, norm_g, wn, wt, subln_g, q_norm_g, k_norm_g, w_out, final_g):
    b, s, _ = x.shape
    (ts_in, tm), tm_out, (tq_a, tk_a), (tq_b, tk_b) = _tile_sizes(s)
    cos_t, sin_t = _rope_tables_t(s)
    bias_tiles = _bias_tables(rel_table, s, tq_a, tk_a)
    fg = final_g.reshape(1, D_MODEL)
    prev = None
    for l in range(DEPTH):
        lam_init = 0.8 - 0.6 * math.exp(-0.3 * l)
        qg_b =jnp.broadcast_to(q_norm_g[l][:, None], (B_HEAD_DIM, ts_in))
        kg_b = jnp.broadcast_to(k_norm_g[l][:, None], (B_HEAD_DIM, ts_in))
        sg_b = jnp.broadcast_to(subln_g[l][:, None], (A_V_DIM, tq_a))
        outs = _inproj(x, mod[l], norm_g[l].reshape(1, D_MODEL), wn[l], wt[l], cos_t, sin_t, qg_b, kg_b, tm, prev=prev)
        if prev is not None:
            x, outs = outs[0], outs[1:]
        ka, gate, qat, vat, qbt, kb, vbt = outs
        ma = _attn_a(lam[l:l + 1], qat, ka, vat, bias_tiles, sg_b, gate, tq_a, tk_a, 1.0 - lam_init)
        mb = _attn_b(qbt, kb, vbt, gate, tq_b, tk_b)
        prev = (ma, mb, mod[l], w_out[l])
    return _outproj(ma, mb, x, mod[DEPTH - 1], w_out[DEPTH - 1], fg, tm_out, final=True)


def kernel(x_prompt, x_sample, c_prompt, c_sample, rel_table, norm_g, w_ada, b_ada, w_in, lam_q1, lam_k1, lam_q2,
           lam_k2, subln_g, q_norm_g, k_norm_g, w_out, final_g):
    bp = x_prompt.shape[0]
    bs = x_sample.shape[0]
    rows = -(-(bp + bs) // 8) * 8
    c_all = jnp.concatenate([c_prompt, c_sample, jnp.zeros((rows - bp - bs, D_MODEL), F32)], axis=0)
    mod_all = _adaln_mod(c_all, w_ada.astype(BF16), b_ada)
    mod_p = mod_all[:, :bp].reshape(DEPTH, bp, 3, D_MODEL)
    mod_s = mod_all[:, bp:bp + bs].reshape(DEPTH, bs, 3, D_MODEL)
    lam = _lambdas(lam_q1, lam_k1, lam_q2, lam_k2)

    w_bf = w_in.astype(BF16)
    wn = jnp.concatenate([w_bf[:, :, _C_KA:_C_VA], w_bf[:, :, _C_GA:_C_QB], w_bf[:, :, _C_GB:D_IN]], axis=2)
    wt = jnp.concatenate([w_bf[:, :, _C_QA:_C_KA], w_bf[:, :, _C_VA:_C_GA], w_bf[:, :, _C_QB:_C_GB]], axis=2)
    wt = jnp.swapaxes(wt, 1, 2)
    w_out_bf = w_out.astype(BF16)

    args = (rel_table, norm_g, wn, wt, subln_g, q_norm_g, k_norm_g, w_out_bf, final_g)
    y_prompt = _trunk(x_prompt, mod_p, lam, *args)
    y_sample = _trunk(x_sample, mod_s, lam, *args)
    return (y_prompt, y_sample)
```

```python
import functools
import math

import jax
import jax.numpy as jnp
from jax import lax
from jax.experimental import pallas as pl
from jax.experimental.pallas import tpu as pltpu

F32 = jnp.float32
BF16 = jnp.bfloat16

D_MODEL = 1024
DEPTH = 4
GRID_W = 64
A_WIDTH = 512
A_V_DIM = 128
A_HEAD_DIM = 64
A_HEADS = 4
B_WIDTH = 512
B_HEAD_DIM = 64
B_HEADS = 8
B_KV_HEADS = 2
B_GROUP = 4
NUM_BUCKETS = 32
MAX_DISTANCE = 128
ROPE_THETA = 10000.0
EPS = 1e-6

LOG2E = 1.4426950408889634
Q_SCALE = (A_HEAD_DIM ** -0.5) * LOG2E
M_INIT = -1e30
ONES_ROWS = 16
VMEM_LIMIT_BYTES = 56 * 1024 * 1024
UNITS_PER_STEP = 8
LOOP_BODY_CHUNKS = 32

_C_QA, _C_KA, _C_VA, _C_GA, _C_QB, _C_KB, _C_VB, _C_GB = 0, 512, 1024, 1536, 2048, 2560, 2688, 2816
D_IN = 3328
N_NAT = 1536
N_TR = 1792


def _rms_rows(x, g):
    ms = jnp.mean(x * x, axis=-1, keepdims=True)
    return x * lax.rsqrt(ms + EPS) * g


def _mod_kernel(c_ref, w_ref, b_ref, o_ref):
    c = c_ref[...]
    c_act = (c * jax.nn.sigmoid(c)).astype(BF16)
    o_ref[0] = jnp.dot(c_act, w_ref[0], preferred_element_type=F32) + b_ref[0]


def _adaln_mod(c_all, w_ada_bf, b_ada):
    rows = c_all.shape[0]
    return pl.pallas_call(
        _mod_kernel,
        grid=(DEPTH, 3),
        in_specs=[
            pl.BlockSpec((rows, D_MODEL), lambda l, j: (0, 0)),
            pl.BlockSpec((1, D_MODEL, D_MODEL), lambda l, j: (l, 0, j)),
            pl.BlockSpec((1, 1, D_MODEL), lambda l, j: (l, 0, j)),
        ],
        out_specs=pl.BlockSpec((1, rows, D_MODEL), lambda l, j: (l, 0, j)),
        out_shape=jax.ShapeDtypeStruct((DEPTH, rows, 3 * D_MODEL), F32),
        name="adaln_mod",
    )(c_all, w_ada_bf, b_ada.reshape(DEPTH, 1, 3 * D_MODEL))


def _lam_kernel(q1_ref, k1_ref, q2_ref, k2_ref, init_ref, o_ref):
    s1 = jnp.sum(q1_ref[...] * k1_ref[...], axis=-1, keepdims=True)
    s2 = jnp.sum(q2_ref[...] * k2_ref[...], axis=-1, keepdims=True)
    lam = jnp.exp(s1) - jnp.exp(s2) + init_ref[...][:, 0:1]
    o_ref[...] = jnp.broadcast_to(lam, o_ref.shape)


def _lambdas(lam_q1, lam_k1, lam_q2, lam_k2):
    init = jnp.asarray([[0.8 - 0.6 * math.exp(-0.3 * l)] * 128 for l in range(DEPTH)], F32)
    out = pl.pallas_call(
        _lam_kernel,
        out_shape=jax.ShapeDtypeStruct((DEPTH, 128), F32),
        name="diff_lambda",
    )(lam_q1, lam_k1, lam_q2, lam_k2, init)
    return out[:, 0]


def _rope_t(x, cos, sin):
    rot = jnp.concatenate([-x[16:32], x[0:16], -x[48:64], x[32:48]], axis=0)
    return x * cos + rot * sin


def _norm_t(x, g):
    ms = jnp.mean(x * x, axis=0, keepdims=True)
    return x * lax.rsqrt(ms + EPS) * g


def _inproj_kernel(x_ref, mod_ref, ng_ref, wn_ref, wt_ref, cos_ref, sin_ref, qg_ref, kg_ref,
                   ka_ref, gate_ref, qat_ref, vat_ref, qbt_ref, kb_ref, vbt_ref):
    ts = qg_ref.shape[1]
    for sub in range(x_ref.shape[1] // ts):
        rows = slice(sub * ts, (sub + 1) * ts)
        _inproj_subtile(rows, x_ref[0, rows, :], mod_ref, ng_ref, wn_ref, wt_ref, cos_ref, sin_ref,
                        qg_ref, kg_ref, ka_ref, gate_ref, qat_ref, vat_ref, qbt_ref, kb_ref, vbt_ref)


def _mid_kernel(ma_ref, mb_ref, x_ref, modp_ref, wo_ref, mod_ref, ng_ref, wn_ref, wt_ref, cos_ref, sin_ref,
                qg_ref, kg_ref, xn_ref, ka_ref, gate_ref, qat_ref, vat_ref, qbt_ref, kb_ref, vbt_ref):
    ts = qg_ref.shape[1]
    for sub in range(x_ref.shape[1] // ts):
        rows = slice(sub * ts, (sub + 1) * ts)
        y = (jnp.dot(ma_ref[0, rows, :], wo_ref[0:A_WIDTH, :], preferred_element_type=F32)
             + jnp.dot(mb_ref[0, rows, :], wo_ref[A_WIDTH:D_MODEL, :], preferred_element_type=F32))
        xn = x_ref[0, rows, :] + modp_ref[0, 2:3, :] * y
        xn_ref[0, rows, :] = xn
        _inproj_subtile(rows, xn, mod_ref, ng_ref, wn_ref, wt_ref, cos_ref, sin_ref,
                        qg_ref, kg_ref, ka_ref, gate_ref, qat_ref, vat_ref, qbt_ref, kb_ref, vbt_ref)


def _inproj_subtile(rows, x, mod_ref, ng_ref, wn_ref, wt_ref, cos_ref, sin_ref, qg_ref, kg_ref,
                    ka_ref, gate_ref, qat_ref, vat_ref, qbt_ref, kb_ref, vbt_ref):
    ts = rows.stop - rows.start
    shift = mod_ref[0, 0:1, :]
    scale = mod_ref[0, 1:2, :]
    h = (_rms_rows(x, ng_ref[...]) * (1.0 + scale) + shift).astype(BF16)

    pn = jnp.dot(h, wn_ref[...], preferred_element_type=F32)
    ka_ref[0, rows, :] = pn[:, 0:512].astype(BF16)
    g = pn[:, 512:N_NAT]
    gate_ref[0, rows, :] = g * jax.nn.sigmoid(g)

    pt = lax.dot_general(wt_ref[...], h, (((1,), (1,)), ((), ())), preferred_element_type=F32)
    qat_ref[0, :, rows] = (pt[0:512] * Q_SCALE).astype(BF16)
    ones = jnp.ones((ONES_ROWS, ts), BF16)
    for hh in range(A_HEADS):
        vat_ref[0, hh, 0:A_V_DIM, rows] = pt[512 + hh * A_V_DIM:512 + (hh + 1) * A_V_DIM].astype(BF16)
        vat_ref[0, hh, A_V_DIM:A_V_DIM + ONES_ROWS, rows] = ones

    cos = cos_ref[:, rows]
    sin = sin_ref[:, rows]
    qg = qg_ref[...]
    for hh in range(B_HEADS):
        xh = pt[1024 + hh * 64:1024 + (hh + 1) * 64]
        qbt_ref[0, hh * 64:(hh + 1) * 64, rows] = (_rope_t(_norm_t(xh, qg), cos, sin) * Q_SCALE).astype(BF16)
    kg = kg_ref[...]
    kparts = []
    for n in range(B_KV_HEADS):
        xh = pt[1536 + n * 64:1536 + (n + 1) * 64]
        kparts.append(_rope_t(_norm_t(xh, kg), cos, sin))
    kb_ref[0, rows, :] = jnp.concatenate(kparts, axis=0).T.astype(BF16)
    for n in range(B_KV_HEADS):
        vbt_ref[0, n, 0:64, rows] = pt[1664 + n * 64:1664 + (n + 1) * 64].astype(BF16)
        vbt_ref[0, n, 64:64 + ONES_ROWS, rows] = ones


def _inproj(x, mod, ng, wn, wt, cos_t, sin_t, qg_b, kg_b, tm, prev=None):
    b, s, _ = x.shape
    grid = (b, s // tm)
    ts = qg_b.shape[1]
    const2 = lambda i, j: (0, 0)
    tok = lambda i, j: (i, j, 0)
    per_batch = lambda i, j: (i, 0, 0)
    fused_in, fused_args, fused_out_specs, fused_out_shape = [], [], [], []
    if prev is not None:
        ma, mb, mod_prev, w_prev = prev
        fused_in = [pl.BlockSpec((1, tm, A_WIDTH), tok), pl.BlockSpec((1, tm, B_WIDTH), tok)]
        fused_args = [ma, mb]
        fused_out_specs = [pl.BlockSpec((1, tm, D_MODEL), tok)]
        fused_out_shape = [jax.ShapeDtypeStruct((b, s, D_MODEL), F32)]
    return pl.pallas_call(
        _inproj_kernel if prev is None else _mid_kernel,
        grid=grid,
        in_specs=fused_in + [
            pl.BlockSpec((1, tm, D_MODEL), tok),
        ] + ([] if prev is None else [
            pl.BlockSpec((1, 3, D_MODEL), per_batch),
            pl.BlockSpec((D_MODEL, D_MODEL), const2, pipeline_mode=pl.Buffered(1)),
        ]) + [
            pl.BlockSpec((1, 3, D_MODEL), per_batch),
            pl.BlockSpec((1, D_MODEL), const2),
            pl.BlockSpec((D_MODEL, N_NAT), const2, pipeline_mode=pl.Buffered(1)),
            pl.BlockSpec((N_TR, D_MODEL), const2, pipeline_mode=pl.Buffered(1)),
            pl.BlockSpec((B_HEAD_DIM, tm), lambda i, j: (0, j)),
            pl.BlockSpec((B_HEAD_DIM, tm), lambda i, j: (0, j)),
            pl.BlockSpec((B_HEAD_DIM, ts), const2),
            pl.BlockSpec((B_HEAD_DIM, ts), const2),
        ],
        out_specs=fused_out_specs + [
            pl.BlockSpec((1, tm, A_WIDTH), lambda i, j: (i, j, 0)),
            pl.BlockSpec((1, tm, D_MODEL), lambda i, j: (i, j, 0)),
            pl.BlockSpec((1, A_WIDTH, tm), lambda i, j: (i, 0, j)),
            pl.BlockSpec((1, A_HEADS, A_V_DIM + ONES_ROWS, tm), lambda i, j: (i, 0, 0, j)),
            pl.BlockSpec((1, B_WIDTH, tm), lambda i, j: (i, 0, j)),
            pl.BlockSpec((1, tm, B_KV_HEADS * B_HEAD_DIM), lambda i, j: (i, j, 0)),
            pl.BlockSpec((1, B_KV_HEADS, B_HEAD_DIM + ONES_ROWS, tm), lambda i, j: (i, 0, 0, j)),
        ],
        out_shape=fused_out_shape + [
            jax.ShapeDtypeStruct((b, s, A_WIDTH), BF16),
            jax.ShapeDtypeStruct((b, s, D_MODEL), F32),
            jax.ShapeDtypeStruct((b, A_WIDTH, s), BF16),
            jax.ShapeDtypeStruct((b, A_HEADS, A_V_DIM + ONES_ROWS, s), BF16),
            jax.ShapeDtypeStruct((b, B_WIDTH, s), BF16),
            jax.ShapeDtypeStruct((b, s, B_KV_HEADS * B_HEAD_DIM), BF16),
            jax.ShapeDtypeStruct((b, B_KV_HEADS, B_HEAD_DIM + ONES_ROWS, s), BF16),
        ],
        compiler_params=pltpu.CompilerParams(
            dimension_semantics=("arbitrary", "arbitrary"), vmem_limit_bytes=VMEM_LIMIT_BYTES),
        name="inproj" if prev is None else "outproj_inproj",
    )(*fused_args, x, *(() if prev is None else (mod_prev, w_prev)), mod, ng, wn, wt, cos_t, sin_t, qg_b, kg_b)


def _loop_shape(n_chunks, n_tiles, loop_units):
    if n_chunks <= LOOP_BODY_CHUNKS // 2:
        unroll, units = n_chunks, UNITS_PER_STEP
    else:
        unroll, units = LOOP_BODY_CHUNKS // loop_units, loop_units
    assert n_chunks % unroll == 0 and unroll % 2 == 0
    while n_tiles % units:
        units //= 2
    return unroll, units


def _run_tiles(units, unroll, n_chunks, setup, sub_step, finalize):
    if n_chunks == unroll:
        for unit in range(units):
            carry = setup(unit)
            for u in range(unroll):
                carry = sub_step(unit, u, 0, carry)
            finalize(unit)
        return
    carries = tuple(setup(unit) for unit in range(units))

    def body(i, carries):
        carries = list(carries)
        for u in range(unroll):
            for unit in range(units):
                carries[unit] = sub_step(unit, u, unroll * i, carries[unit])
        return tuple(carries)

    lax.fori_loop(0, n_chunks // unroll, body, carries)
    for unit in range(units):
        finalize(unit)


def _attn_a_kernel(scal_ref, qt_ref, k_ref, vt_ref, bias_ref, sg_ref, gate_ref, o_ref, *scratch,
                   tq, tk, n_chunks, ratio, unroll, units, out_scale):
    lam = scal_ref[0]
    log_ratio = ratio.bit_length() - 1
    wq_refs = [scratch[4 * unit] for unit in range(units)]
    acc_refs = [scratch[4 * unit + 1] for unit in range(units)]
    s_bufs = [scratch[4 * unit + 2:4 * unit + 4] for unit in range(units)]
    tiles = [pl.program_id(2) * units + unit for unit in range(units)]

    def qk(unit, c, s_ref):
        start = pl.multiple_of(c * tk, tk)
        d = c - lax.shift_right_logical(tiles[unit], log_ratio)
        r = lax.bitwise_and(tiles[unit], ratio - 1)
        tile = jnp.where(d < -1, 3 * ratio, jnp.where(d > 1, 3 * ratio + 1, (d + 1) * ratio + r))
        t = bias_ref[0, tile]
        s = jnp.dot(k_ref[0, pl.ds(start, tk), :], wq_refs[unit][...], preferred_element_type=F32)
        s = s + jnp.concatenate([t, t], axis=1)
        s_ref[...] = s
        return jnp.max(s, axis=0, keepdims=True)

    def softmax_pv(unit, s_ref, mc, m, c):
        mn = jnp.maximum(m, mc)
        alpha = jnp.exp2(m - mn)
        p = jnp.exp2(s_ref[...] - mn).astype(BF16)
        start = pl.multiple_of(c * tk, tk)
        pv = jnp.dot(vt_ref[0, 0, :, pl.ds(start, tk)], p, preferred_element_type=F32)
        acc_refs[unit][...] = acc_refs[unit][...] * alpha + pv
        return mn

    def setup(unit):
        q = qt_ref[0, :, unit * tq:(unit + 1) * tq]
        zero = jnp.zeros((A_HEAD_DIM, tq), BF16)
        wq_refs[unit][0:64, 0:tq] = q[0:64]
        wq_refs[unit][0:64, tq:2 * tq] = zero
        wq_refs[unit][64:128, 0:tq] = zero
        wq_refs[unit][64:128, tq:2 * tq] = q[64:128]
        acc_refs[unit][...] = jnp.zeros(acc_refs[unit].shape, F32)
        return jnp.full((1, 2 * tq), M_INIT, F32), qk(unit, 0, s_bufs[unit][0])

    def sub_step(unit, u, c0, carry):
        m, mc = carry
        mc_next = qk(unit, jnp.minimum(c0 + u + 1, n_chunks - 1), s_bufs[unit][(u + 1) % 2])
        return softmax_pv(unit, s_bufs[unit][u % 2], mc, m, c0 + u), mc_next

    def finalize(unit):
        acc = acc_refs[unit][...]
        o = acc[0:A_V_DIM, :] * (1.0 / acc[A_V_DIM:A_V_DIM + 1, :])
        w = o[:, 0:tq] - lam * o[:, tq:2 * tq]
        y = _norm_t(w, sg_ref[...]) * out_scale
        rows = slice(unit * tq, (unit + 1) * tq)
        o_ref[0, rows, :] = (y.T * gate_ref[0, rows, :]).astype(BF16)

    _run_tiles(units, unroll, n_chunks, setup, sub_step, finalize)


def _attn_a(scal, qat, ka, vat, bias_tiles, sg_b, gate, tq, tk, out_scale):
    b, s, _ = ka.shape
    n_chunks = s // tk
    ratio = tk // tq
    unroll, units = _loop_shape(n_chunks, s // tq, loop_units=2)
    assert n_chunks % unroll == 0 and ratio * tq == tk and ratio & (ratio - 1) == 0
    kern = functools.partial(_attn_a_kernel, tq=tq, tk=tk, n_chunks=n_chunks, ratio=ratio, unroll=unroll,
                             units=units, out_scale=out_scale)
    return pl.pallas_call(
        kern,
        grid=(b, A_HEADS, s // (tq * units)),
        in_specs=[
            pl.BlockSpec(memory_space=pltpu.SMEM),
            pl.BlockSpec((1, 2 * A_HEAD_DIM, tq * units), lambda i, hh, j: (i, hh, j)),
            pl.BlockSpec((1, s, 2 * A_HEAD_DIM), lambda i, hh, j: (i, 0, hh)),
            pl.BlockSpec((1, 1, A_V_DIM + ONES_ROWS, s), lambda i, hh, j: (i, hh, 0, 0)),
            pl.BlockSpec((1, 3 * ratio + 2, tk, tq), lambda i, hh, j: (hh, 0, 0, 0)),
            pl.BlockSpec((A_V_DIM, tq), lambda i, hh, j: (0, 0)),
            pl.BlockSpec((1, tq * units, A_V_DIM), lambda i, hh, j: (i, j, hh)),
        ],
        out_specs=pl.BlockSpec((1, tq * units, A_V_DIM), lambda i, hh, j: (i, j, hh)),
        out_shape=jax.ShapeDtypeStruct((b, s, A_WIDTH), BF16),
        scratch_shapes=[
            pltpu.VMEM((2 * A_HEAD_DIM, 2 * tq), BF16),
            pltpu.VMEM((A_V_DIM + ONES_ROWS, 2 * tq), F32),
            pltpu.VMEM((tk, 2 * tq), F32),
            pltpu.VMEM((tk, 2 * tq), F32),
        ] * units,
        compiler_params=pltpu.CompilerParams(
            dimension_semantics=("arbitrary", "arbitrary", "arbitrary"), vmem_limit_bytes=VMEM_LIMIT_BYTES),
        name="attn_a",
    )(scal, qat, ka, vat, bias_tiles, sg_b, gate)


def _attn_b_kernel(qt_ref, k_ref, vt_ref, gate_ref, o_ref, *scratch, tq, tk, n_chunks, unroll, units):
    half = B_GROUP * tq
    lanes = B_HEADS * tq
    wq_refs = [scratch[5 * unit] for unit in range(units)]
    acc_refs = [scratch[5 * unit + 1:5 * unit + 3] for unit in range(units)]
    s_bufs = [scratch[5 * unit + 3:5 * unit + 5] for unit in range(units)]

    def qk(unit, c, s_ref):
        start = pl.multiple_of(c * tk, tk)
        s = jnp.dot(k_ref[0, pl.ds(start, tk), :], wq_refs[unit][...], preferred_element_type=F32)
        s_ref[...] = s
        return jnp.max(s, axis=0, keepdims=True)

    def softmax_pv(unit, s_ref, mc, m, c):
        acc0_ref, acc1_ref = acc_refs[unit]
        mn = jnp.maximum(m, mc)
        alpha = jnp.exp2(m - mn)
        p = jnp.exp2(s_ref[...] - mn).astype(BF16)
        start = pl.multiple_of(c * tk, tk)
        v0 = vt_ref[0, 0, :, pl.ds(start, tk)]
        v1 = vt_ref[0, 1, :, pl.ds(start, tk)]
        acc0_ref[...] = acc0_ref[...] * alpha[:, 0:half] + jnp.dot(v0, p[:, 0:half], preferred_element_type=F32)
        acc1_ref[...] = acc1_ref[...] * alpha[:, half:] + jnp.dot(v1, p[:, half:], preferred_element_type=F32)
        return mn

    def setup(unit):
        wq_refs[unit][...] = jnp.zeros(wq_refs[unit].shape, BF16)
        for hh in range(B_HEADS):
            n = hh // B_GROUP
            wq_refs[unit][n * 64:(n + 1) * 64, hh * tq:(hh + 1) * tq] = qt_ref[0, hh * 64:(hh + 1) * 64,
                                                                               unit * tq:(unit + 1) * tq]
        for acc_ref in acc_refs[unit]:
            acc_ref[...] = jnp.zeros(acc_ref.shape, F32)
        return jnp.full((1, lanes), M_INIT, F32), qk(unit, 0, s_bufs[unit][0])

    def sub_step(unit, u, c0, carry):
        m, mc = carry
        mc_next = qk(unit, jnp.minimum(c0 + u + 1, n_chunks - 1), s_bufs[unit][(u + 1) % 2])
        return softmax_pv(unit, s_bufs[unit][u % 2], mc, m, c0 + u), mc_next

    def finalize(unit):
        parts = []
        for acc_ref in acc_refs[unit]:
            acc = acc_ref[...]
            o = acc[0:64, :] * (1.0 / acc[64:65, :])
            for g in range(B_GROUP):
                parts.append(o[:, g * tq:(g + 1) * tq])
        rows = slice(unit * tq, (unit + 1) * tq)
        out = jnp.concatenate(parts, axis=0).T
        o_ref[0, rows, :] = (out * gate_ref[0, rows, :]).astype(BF16)

    _run_tiles(units, unroll, n_chunks, setup, sub_step, finalize)


def _attn_b(qbt, kb, vbt, gate, tq, tk):
    b, s, _ = kb.shape
    n_chunks = s // tk
    unroll, units = _loop_shape(n_chunks, s // tq, loop_units=1)
    kern = functools.partial(_attn_b_kernel, tq=tq, tk=tk, n_chunks=n_chunks, unroll=unroll, units=units)
    return pl.pallas_call(
        kern,
        grid=(b, s // (tq * units)),
        in_specs=[
            pl.BlockSpec((1, B_WIDTH, tq * units), lambda i, j: (i, 0, j)),
            pl.BlockSpec((1, s, B_KV_HEADS * B_HEAD_DIM), lambda i, j: (i, 0, 0)),
            pl.BlockSpec((1, B_KV_HEADS, B_HEAD_DIM + ONES_ROWS, s), lambda i, j: (i, 0, 0, 0)),
            pl.BlockSpec((1, tq * units, B_WIDTH), lambda i, j: (i, j, 1)),
        ],
        out_specs=pl.BlockSpec((1, tq * units, B_WIDTH), lambda i, j: (i, j, 0)),
        out_shape=jax.ShapeDtypeStruct((b, s, B_WIDTH), BF16),
        scratch_shapes=[
            pltpu.VMEM((B_KV_HEADS * B_HEAD_DIM, B_HEADS * tq), BF16),
            pltpu.VMEM((B_HEAD_DIM + ONES_ROWS, B_GROUP * tq), F32),
            pltpu.VMEM((B_HEAD_DIM + ONES_ROWS, B_GROUP * tq), F32),
            pltpu.VMEM((tk, B_HEADS * tq), F32),
            pltpu.VMEM((tk, B_HEADS * tq), F32),
        ] * units,
        compiler_params=pltpu.CompilerParams(
            dimension_semantics=("arbitrary", "arbitrary"), vmem_limit_bytes=VMEM_LIMIT_BYTES),
        name="attn_b",
    )(qbt, kb, vbt, gate)


def _outproj_kernel(ma_ref, mb_ref, x_ref, mod_ref, w_ref, fg_ref, o_ref, *, final):
    y = (jnp.dot(ma_ref[0], w_ref[0:A_WIDTH, :], preferred_element_type=F32)
         + jnp.dot(mb_ref[0], w_ref[A_WIDTH:D_MODEL, :], preferred_element_type=F32))
    xn = x_ref[0] + mod_ref[0, 2:3, :] * y
    if final:
        xn = _rms_rows(xn, fg_ref[...])
    o_ref[0] = xn


def _outproj(ma, mb, x, mod, w, fg, tm, final):
    b, s, _ = x.shape
    tok = lambda i, j: (i, j, 0)
    return pl.pallas_call(
        functools.partial(_outproj_kernel, final=final),
        grid=(b, s // tm),
        in_specs=[
            pl.BlockSpec((1, tm, A_WIDTH), tok),
            pl.BlockSpec((1, tm, B_WIDTH), tok),
            pl.BlockSpec((1, tm, D_MODEL), tok),
            pl.BlockSpec((1, 3, D_MODEL), lambda i, j: (i, 0, 0)),
            pl.BlockSpec((D_MODEL, D_MODEL), lambda i, j: (0, 0)),
            pl.BlockSpec((1, D_MODEL), lambda i, j: (0, 0)),
        ],
        out_specs=pl.BlockSpec((1, tm, D_MODEL), tok),
        out_shape=jax.ShapeDtypeStruct((b, s, D_MODEL), F32),
        compiler_params=pltpu.CompilerParams(
            dimension_semantics=("arbitrary", "arbitrary"), vmem_limit_bytes=VMEM_LIMIT_BYTES),
        name="outproj",
    )(ma, mb, x, mod, w, fg)


def _t5_bucket(rel):
    half = NUM_BUCKETS // 2
    max_exact = half // 2
    ret = jnp.where(rel > 0, half, 0)
    n = jnp.abs(rel)
    nf = jnp.maximum(n, 1).astype(jnp.float32)
    large = max_exact + (jnp.log(nf / max_exact) / math.log(MAX_DISTANCE / max_exact)
                         * (half - max_exact)).astype(jnp.int32)
    large = jnp.minimum(large, half - 1)
    return ret + jnp.where(n < max_exact, n, large)


def _bias_tables(rel_table, s, tq, tk):
    ratio = tk // tq
    assert tk >= MAX_DISTANCE and s - 1 >= MAX_DISTANCE
    offsets = jnp.arange(-MAX_DISTANCE, MAX_DISTANCE + 1, dtype=jnp.int32)
    core = rel_table[_t5_bucket(offsets)].T.astype(F32) * LOG2E
    reach = 2 * tk + tq
    boff_p = jnp.pad(core, ((0, 0), (reach - MAX_DISTANCE, reach - MAX_DISTANCE)), mode="edge")
    span = tk + tq - 1
    tiles = []
    for dc in (-1, 0, 1):
        for r in range(ratio):
            first = dc * tk - r * tq - (tq - 1) + reach
            u = jnp.flip(boff_p[:, first:first + span], axis=1)
            hankel = jnp.tile(u, (1, tk + 1))[:, :tk * (span + 1)].reshape(A_HEADS, tk, span + 1)[:, :, :tq]
            tiles.append(jnp.flip(hankel, axis=1))
    for col in (0, 2 * reach):
        tiles.append(jnp.broadcast_to(boff_p[:, col][:, None, None], (A_HEADS, tk, tq)))
    return jnp.stack(tiles, axis=1)


def _rope_tables_t(s):
    rows = s // GRID_W
    row = jnp.repeat(jnp.arange(rows), GRID_W).astype(F32)
    col = jnp.tile(jnp.arange(GRID_W), rows).astype(F32)
    axis_dim = B_HEAD_DIM // 2
    inv_freq = ROPE_THETA ** (-jnp.arange(0, axis_dim, 2, dtype=F32) / axis_dim)
    ang_r = row[:, None] * inv_freq[None, :]
    ang_c = col[:, None] * inv_freq[None, :]
    ang = jnp.concatenate([ang_r, ang_r, ang_c, ang_c], axis=-1)
    return jnp.cos(ang).T, jnp.sin(ang).T


def _tile_sizes(s):
    del s
    return (512, 1024), 1024, (256, 512), (128, 256)


def _trunk(x, mod, lam, rel_table, norm_g, wn, wt, subln_g, q_norm_g, k_norm_g, w_out, final_g):
    b, s, _ = x.shape
    (ts_in, tm), tm_out, (tq_a, tk_a), (tq_b, tk_b) = _tile_sizes(s)
    cos_t, sin_t = _rope_tables_t(s)
    bias_tiles = _bias_tables(rel_table, s, tq_a, tk_a)
    fg = final_g.reshape(1, D_MODEL)
    prev = None
    for l in range(DEPTH):
        lam_init = 0.8 - 0.6 * math.exp(-0.3 * l)
        qg_b =jnp.broadcast_to(q_norm_g[l][:, None], (B_HEAD_DIM, ts_in))
        kg_b = jnp.broadcast_to(k_norm_g[l][:, None], (B_HEAD_DIM, ts_in))
        sg_b = jnp.broadcast_to(subln_g[l][:, None], (A_V_DIM, tq_a))
        outs = _inproj(x, mod[l], norm_g[l].reshape(1, D_MODEL), wn[l], wt[l], cos_t, sin_t, qg_b, kg_b, tm, prev=prev)
        if prev is not None:
            x, outs = outs[0], outs[1:]
        ka, gate, qat, vat, qbt, kb, vbt = outs
        ma = _attn_a(lam[l:l + 1], qat, ka, vat, bias_tiles, sg_b, gate, tq_a, tk_a, 1.0 - lam_init)
        mb = _attn_b(qbt, kb, vbt, gate, tq_b, tk_b)
        prev = (ma, mb, mod[l], w_out[l])
    return _outproj(ma, mb, x, mod[DEPTH - 1], w_out[DEPTH - 1], fg, tm_out, final=True)


def kernel(x_prompt, x_sample, c_prompt, c_sample, rel_table, norm_g, w_ada, b_ada, w_in, lam_q1, lam_k1, lam_q2,
           lam_k2, subln_g, q_norm_g, k_norm_g, w_out, final_g):
    bp = x_prompt.shape[0]
    bs = x_sample.shape[0]
    rows = -(-(bp + bs) // 8) * 8
    c_all = jnp.concatenate([c_prompt, c_sample, jnp.zeros((rows - bp - bs, D_MODEL), F32)], axis=0)
    mod_all = _adaln_mod(c_all, w_ada.astype(BF16), b_ada)
    mod_p = mod_all[:, :bp].reshape(DEPTH, bp, 3, D_MODEL)
    mod_s = mod_all[:, bp:bp + bs].reshape(DEPTH, bs, 3, D_MODEL)
    lam = _lambdas(lam_q1, lam_k1, lam_q2, lam_k2)

    w_bf = w_in.astype(BF16)
    wn = jnp.concatenate([w_bf[:, :, _C_KA:_C_VA], w_bf[:, :, _C_GA:_C_QB], w_bf[:, :, _C_GB:D_IN]], axis=2)
    wt = jnp.concatenate([w_bf[:, :, _C_QA:_C_KA], w_bf[:, :, _C_VA:_C_GA], w_bf[:, :, _C_QB:_C_GB]], axis=2)
    wt = jnp.swapaxes(wt, 1, 2)
    w_out_bf = w_out.astype(BF16)

    args = (rel_table, norm_g, wn, wt, subln_g, q_norm_g, k_norm_g, w_out_bf, final_g)
    y_prompt = _trunk(x_prompt, mod_p, lam, *args)
    y_sample = _trunk(x_sample, mod_s, lam, *args)
    return (y_prompt, y_sample)
```

```python
import functools
import math

import jax
import jax.numpy as jnp
from jax import lax
from jax.experimental import pallas as pl
from jax.experimental.pallas import tpu as pltpu

F32 = jnp.float32
BF16 = jnp.bfloat16

D_MODEL = 1024
DEPTH = 4
GRID_W = 64
A_WIDTH = 512
A_V_DIM = 128
A_HEAD_DIM = 64
A_HEADS = 4
B_WIDTH = 512
B_HEAD_DIM = 64
B_HEADS = 8
B_KV_HEADS = 2
B_GROUP = 4
NUM_BUCKETS = 32
MAX_DISTANCE = 128
ROPE_THETA = 10000.0
EPS = 1e-6

LOG2E = 1.4426950408889634
Q_SCALE = (A_HEAD_DIM ** -0.5) * LOG2E
M_INIT = -1e30
ONES_ROWS = 16
VMEM_LIMIT_BYTES = 56 * 1024 * 1024
UNITS_PER_STEP = 8
LOOP_BODY_CHUNKS = 32

_C_QA, _C_KA, _C_VA, _C_GA, _C_QB, _C_KB, _C_VB, _C_GB = 0, 512, 1024, 1536, 2048, 2560, 2688, 2816
D_IN = 3328
N_NAT = 1536
N_TR = 1792


def _rms_rows(x, g):
    ms = jnp.mean(x * x, axis=-1, keepdims=True)
    return x * lax.rsqrt(ms + EPS) * g


def _mod_kernel(c_ref, w_ref, b_ref, o_ref):
    c = c_ref[...]
    c_act = (c * jax.nn.sigmoid(c)).astype(BF16)
    o_ref[0] = jnp.dot(c_act, w_ref[0], preferred_element_type=F32) + b_ref[0]


def _adaln_mod(c_all, w_ada_bf, b_ada):
    rows = c_all.shape[0]
    return pl.pallas_call(
        _mod_kernel,
        grid=(DEPTH, 3),
        in_specs=[
            pl.BlockSpec((rows, D_MODEL), lambda l, j: (0, 0)),
            pl.BlockSpec((1, D_MODEL, D_MODEL), lambda l, j: (l, 0, j)),
            pl.BlockSpec((1, 1, D_MODEL), lambda l, j: (l, 0, j)),
        ],
        out_specs=pl.BlockSpec((1, rows, D_MODEL), lambda l, j: (l, 0, j)),
        out_shape=jax.ShapeDtypeStruct((DEPTH, rows, 3 * D_MODEL), F32),
        name="adaln_mod",
    )(c_all, w_ada_bf, b_ada.reshape(DEPTH, 1, 3 * D_MODEL))


def _lam_kernel(q1_ref, k1_ref, q2_ref, k2_ref, init_ref, o_ref):
    s1 = jnp.sum(q1_ref[...] * k1_ref[...], axis=-1, keepdims=True)
    s2 = jnp.sum(q2_ref[...] * k2_ref[...], axis=-1, keepdims=True)
    lam = jnp.exp(s1) - jnp.exp(s2) + init_ref[...][:, 0:1]
    o_ref[...] = jnp.broadcast_to(lam, o_ref.shape)


def _lambdas(lam_q1, lam_k1, lam_q2, lam_k2):
    init = jnp.asarray([[0.8 - 0.6 * math.exp(-0.3 * l)] * 128 for l in range(DEPTH)], F32)
    out = pl.pallas_call(
        _lam_kernel,
        out_shape=jax.ShapeDtypeStruct((DEPTH, 128), F32),
        name="diff_lambda",
    )(lam_q1, lam_k1, lam_q2, lam_k2, init)
    return out[:, 0]


def _rope_t(x, cos, sin):
    rot = jnp.concatenate([-x[16:32], x[0:16], -x[48:64], x[32:48]], axis=0)
    return x * cos + rot * sin


def _norm_t(x, g):
    ms = jnp.mean(x * x, axis=0, keepdims=True)
    return x * lax.rsqrt(ms + EPS) * g


def _inproj_kernel(x_ref, mod_ref, ng_ref, wn_ref, wt_ref, cos_ref, sin_ref, qg_ref, kg_ref,
                   ka_ref, gate_ref, qat_ref, vat_ref, qbt_ref, kb_ref, vbt_ref):
    ts = qg_ref.shape[1]
    for sub in range(x_ref.shape[1] // ts):
        rows = slice(sub * ts, (sub + 1) * ts)
        _inproj_subtile(rows, x_ref[0, rows, :], mod_ref, ng_ref, wn_ref, wt_ref, cos_ref, sin_ref,
                        qg_ref, kg_ref, ka_ref, gate_ref, qat_ref, vat_ref, qbt_ref, kb_ref, vbt_ref)


def _mid_kernel(ma_ref, mb_ref, x_ref, modp_ref, wo_ref, mod_ref, ng_ref, wn_ref, wt_ref, cos_ref, sin_ref,
                qg_ref, kg_ref, xn_ref, ka_ref, gate_ref, qat_ref, vat_ref, qbt_ref, kb_ref, vbt_ref):
    ts = qg_ref.shape[1]
    for sub in range(x_ref.shape[1] // ts):
        rows = slice(sub * ts, (sub + 1) * ts)
        y = (jnp.dot(ma_ref[0, rows, :], wo_ref[0:A_WIDTH, :], preferred_element_type=F32)
             + jnp.dot(mb_ref[0, rows, :], wo_ref[A_WIDTH:D_MODEL, :], preferred_element_type=F32))
        xn = x_ref[0, rows, :] + modp_ref[0, 2:3, :] * y
        xn_ref[0, rows, :] = xn
        _inproj_subtile(rows, xn, mod_ref, ng_ref, wn_ref, wt_ref, cos_ref, sin_ref,
                        qg_ref, kg_ref, ka_ref, gate_ref, qat_ref, vat_ref, qbt_ref, kb_ref, vbt_ref)


def _inproj_subtile(rows, x, mod_ref, ng_ref, wn_ref, wt_ref, cos_ref, sin_ref, qg_ref, kg_ref,
                    ka_ref, gate_ref, qat_ref, vat_ref, qbt_ref, kb_ref, vbt_ref):
    ts = rows.stop - rows.start
    shift = mod_ref[0, 0:1, :]
    scale = mod_ref[0, 1:2, :]
    h = (_rms_rows(x, ng_ref[...]) * (1.0 + scale) + shift).astype(BF16)

    pn = jnp.dot(h, wn_ref[...], preferred_element_type=F32)
    ka_ref[0, rows, :] = pn[:, 0:512].astype(BF16)
    g = pn[:, 512:N_NAT]
    gate_ref[0, rows, :] = g * jax.nn.sigmoid(g)

    pt = lax.dot_general(wt_ref[...], h, (((1,), (1,)), ((), ())), preferred_element_type=F32)
    qat_ref[0, :, rows] = (pt[0:512] * Q_SCALE).astype(BF16)
    ones = jnp.ones((ONES_ROWS, ts), BF16)
    for hh in range(A_HEADS):
        vat_ref[0, hh, 0:A_V_DIM, rows] = pt[512 + hh * A_V_DIM:512 + (hh + 1) * A_V_DIM].astype(BF16)
        vat_ref[0, hh, A_V_DIM:A_V_DIM + ONES_ROWS, rows] = ones

    cos = cos_ref[:, rows]
    sin = sin_ref[:, rows]
    qg = qg_ref[...]
    for hh in range(B_HEADS):
        xh = pt[1024 + hh * 64:1024 + (hh + 1) * 64]
        qbt_ref[0, hh * 64:(hh + 1) * 64, rows] = (_rope_t(_norm_t(xh, qg), cos, sin) * Q_SCALE).astype(BF16)
    kg = kg_ref[...]
    kparts = []
    for n in range(B_KV_HEADS):
        xh = pt[1536 + n * 64:1536 + (n + 1) * 64]
        kparts.append(_rope_t(_norm_t(xh, kg), cos, sin))
    kb_ref[0, rows, :] = jnp.concatenate(kparts, axis=0).T.astype(BF16)
    for n in range(B_KV_HEADS):
        vbt_ref[0, n, 0:64, rows] = pt[1664 + n * 64:1664 + (n + 1) * 64].astype(BF16)
        vbt_ref[0, n, 64:64 + ONES_ROWS, rows] = ones


def _inproj(x, mod, ng, wn, wt, cos_t, sin_t, qg_b, kg_b, tm, prev=None):
    b, s, _ = x.shape
    grid = (b, s // tm)
    ts = qg_b.shape[1]
    const2 = lambda i, j: (0, 0)
    tok = lambda i, j: (i, j, 0)
    per_batch = lambda i, j: (i, 0, 0)
    fused_in, fused_args, fused_out_specs, fused_out_shape = [], [], [], []
    if prev is not None:
        ma, mb, mod_prev, w_prev = prev
        fused_in = [pl.BlockSpec((1, tm, A_WIDTH), tok), pl.BlockSpec((1, tm, B_WIDTH), tok)]
        fused_args = [ma, mb]
        fused_out_specs = [pl.BlockSpec((1, tm, D_MODEL), tok)]
        fused_out_shape = [jax.ShapeDtypeStruct((b, s, D_MODEL), F32)]
    return pl.pallas_call(
        _inproj_kernel if prev is None else _mid_kernel,
        grid=grid,
        in_specs=fused_in + [
            pl.BlockSpec((1, tm, D_MODEL), tok),
        ] + ([] if prev is None else [
            pl.BlockSpec((1, 3, D_MODEL), per_batch),
            pl.BlockSpec((D_MODEL, D_MODEL), const2, pipeline_mode=pl.Buffered(1)),
        ]) + [
            pl.BlockSpec((1, 3, D_MODEL), per_batch),
            pl.BlockSpec((1, D_MODEL), const2),
            pl.BlockSpec((D_MODEL, N_NAT), const2, pipeline_mode=pl.Buffered(1)),
            pl.BlockSpec((N_TR, D_MODEL), const2, pipeline_mode=pl.Buffered(1)),
            pl.BlockSpec((B_HEAD_DIM, tm), lambda i, j: (0, j)),
            pl.BlockSpec((B_HEAD_DIM, tm), lambda i, j: (0, j)),
            pl.BlockSpec((B_HEAD_DIM, ts), const2),
            pl.BlockSpec((B_HEAD_DIM, ts), const2),
        ],
        out_specs=fused_out_specs + [
            pl.BlockSpec((1, tm, A_WIDTH), lambda i, j: (i, j, 0)),
            pl.BlockSpec((1, tm, D_MODEL), lambda i, j: (i, j, 0)),
            pl.BlockSpec((1, A_WIDTH, tm), lambda i, j: (i, 0, j)),
            pl.BlockSpec((1, A_HEADS, A_V_DIM + ONES_ROWS, tm), lambda i, j: (i, 0, 0, j)),
            pl.BlockSpec((1, B_WIDTH, tm), lambda i, j: (i, 0, j)),
            pl.BlockSpec((1, tm, B_KV_HEADS * B_HEAD_DIM), lambda i, j: (i, j, 0)),
            pl.BlockSpec((1, B_KV_HEADS, B_HEAD_DIM + ONES_ROWS, tm), lambda i, j: (i, 0, 0, j)),
        ],
        out_shape=fused_out_shape + [
            jax.ShapeDtypeStruct((b, s, A_WIDTH), BF16),
            jax.ShapeDtypeStruct((b, s, D_MODEL), F32),
            jax.ShapeDtypeStruct((b, A_WIDTH, s), BF16),
            jax.ShapeDtypeStruct((b, A_HEADS, A_V_DIM + ONES_ROWS, s), BF16),
            jax.ShapeDtypeStruct((b, B_WIDTH, s), BF16),
            jax.ShapeDtypeStruct((b, s, B_KV_HEADS * B_HEAD_DIM), BF16),
            jax.ShapeDtypeStruct((b, B_KV_HEADS, B_HEAD_DIM + ONES_ROWS, s), BF16),
        ],
        compiler_params=pltpu.CompilerParams(
            dimension_semantics=("arbitrary", "arbitrary"), vmem_limit_bytes=VMEM_LIMIT_BYTES),
        name="inproj" if prev is None else "outproj_inproj",
    )(*fused_args, x, *(() if prev is None else (mod_prev, w_prev)), mod, ng, wn, wt, cos_t, sin_t, qg_b, kg_b)


def _loop_shape(n_chunks, n_tiles, loop_units):
    if n_chunks <= LOOP_BODY_CHUNKS // 2:
        unroll, units = n_chunks, UNITS_PER_STEP
    else:
        unroll, units = LOOP_BODY_CHUNKS // loop_units, loop_units
    assert n_chunks % unroll == 0 and unroll % 2 == 0
    while n_tiles % units:
        units //= 2
    return unroll, units


def _run_tiles(units, unroll, n_chunks, setup, sub_step, finalize):
    if n_chunks == unroll:
        for unit in range(units):
            carry = setup(unit)
            for u in range(unroll):
                carry = sub_step(unit, u, 0, carry)
            finalize(unit)
        return
    carries = tuple(setup(unit) for unit in range(units))

    def body(i, carries):
        carries = list(carries)
        for u in range(unroll):
            for unit in range(units):
                carries[unit] = sub_step(unit, u, unroll * i, carries[unit])
        return tuple(carries)

    lax.fori_loop(0, n_chunks // unroll, body, carries)
    for unit in range(units):
        finalize(unit)


def _attn_a_kernel(scal_ref, qt_ref, k_ref, vt_ref, bias_ref, sg_ref, gate_ref, o_ref, *scratch,
                   tq, tk, n_chunks, ratio, unroll, units, out_scale):
    lam = scal_ref[0]
    log_ratio = ratio.bit_length() - 1
    wq_refs = [scratch[4 * unit] for unit in range(units)]
    acc_refs = [scratch[4 * unit + 1] for unit in range(units)]
    s_bufs = [scratch[4 * unit + 2:4 * unit + 4] for unit in range(units)]
    tiles = [pl.program_id(2) * units + unit for unit in range(units)]

    def qk(unit, c, s_ref):
        start = pl.multiple_of(c * tk, tk)
        d = c - lax.shift_right_logical(tiles[unit], log_ratio)
        r = lax.bitwise_and(tiles[unit], ratio - 1)
        tile = jnp.where(d < -1, 3 * ratio, jnp.where(d > 1, 3 * ratio + 1, (d + 1) * ratio + r))
        t = bias_ref[0, tile]
        s = jnp.dot(k_ref[0, pl.ds(start, tk), :], wq_refs[unit][...], preferred_element_type=F32)
        s = s + jnp.concatenate([t, t], axis=1)
        s_ref[...] = s
        return jnp.max(s, axis=0, keepdims=True)

    def softmax_pv(unit, s_ref, mc, m, c):
        mn = jnp.maximum(m, mc)
        alpha = jnp.exp2(m - mn)
        p = jnp.exp2(s_ref[...] - mn).astype(BF16)
        start = pl.multiple_of(c * tk, tk)
        pv = jnp.dot(vt_ref[0, 0, :, pl.ds(start, tk)], p, preferred_element_type=F32)
        acc_refs[unit][...] = acc_refs[unit][...] * alpha + pv
        return mn

    def setup(unit, c_first=0):
        q = qt_ref[0, :, unit * tq:(unit + 1) * tq]
        zero = jnp.zeros((A_HEAD_DIM, tq), BF16)
        wq_refs[unit][0:64, 0:tq] = q[0:64]
        wq_refs[unit][0:64, tq:2 * tq] = zero
        wq_refs[unit][64:128, 0:tq] = zero
        wq_refs[unit][64:128, tq:2 * tq] = q[64:128]
        acc_refs[unit][...] = jnp.zeros(acc_refs[unit].shape, F32)
        return jnp.full((1, 2 * tq), M_INIT, F32), qk(unit, c_first, s_bufs[unit][0])

    def sub_step(unit, u, c0, carry):
        m, mc = carry
        mc_next = qk(unit, jnp.minimum(c0 + u + 1, n_chunks - 1), s_bufs[unit][(u + 1) % 2])
        return softmax_pv(unit, s_bufs[unit][u % 2], mc, m, c0 + u), mc_next

    def finalize(unit):
        acc = acc_refs[unit][...]
        o = acc[0:A_V_DIM, :] * (1.0 / acc[A_V_DIM:A_V_DIM + 1, :])
        w = o[:, 0:tq] - lam * o[:, tq:2 * tq]
        y = _norm_t(w, sg_ref[...]) * out_scale
        rows = slice(unit * tq, (unit + 1) * tq)
        o_ref[0, rows, :] = (y.T * gate_ref[0, rows, :]).astype(BF16)

    if n_chunks != 2 * unroll:
        _run_tiles(units, unroll, n_chunks, setup, sub_step, finalize)
        return

    h = pl.program_id(1)
    c_neg = scal_ref[1 + h]
    c_pos = scal_ref[1 + A_HEADS + h]
    window = unroll
    w0 = jnp.clip(lax.shift_right_logical(tiles[0], log_ratio) - window // 2, 0, n_chunks - window)

    def far_chunk(f):
        return jnp.where(f < w0, f, f + window)

    def far_shift(f):
        return jnp.where(f < w0, c_neg, c_pos)

    def qk_far(unit, c, s_ref, shift):
        start = pl.multiple_of(c * tk, tk)
        s = jnp.dot(k_ref[0, pl.ds(start, tk), :], wq_refs[unit][...], preferred_element_type=F32)
        s_ref[...] = s
        return jnp.max(s, axis=0, keepdims=True) + shift

    def softmax_pv_far(unit, s_ref, mc, m, c, shift):
        mn = jnp.maximum(m, mc)
        alpha = jnp.exp2(m - mn)
        p = jnp.exp2(s_ref[...] - (mn - shift)).astype(BF16)
        start = pl.multiple_of(c * tk, tk)
        pv = jnp.dot(vt_ref[0, 0, :, pl.ds(start, tk)], p, preferred_element_type=F32)
        acc_refs[unit][...] = acc_refs[unit][...] * alpha + pv
        return mn

    def near_block(_, carries):
        carries = list(carries)
        for u in range(window):
            for unit in range(units):
                m, mc = carries[unit]
                nxt = s_bufs[unit][(u + 1) % 2]
                if u + 1 < window:
                    mc_next = qk(unit, w0 + u + 1, nxt)
                else:
                    mc_next = qk_far(unit, far_chunk(0), nxt, far_shift(0))
                carries[unit] = (softmax_pv(unit, s_bufs[unit][u % 2], mc, m, w0 + u), mc_next)
        return tuple(carries)

    def far_block(_, carries):
        carries = list(carries)
        for f in range(window):
            for unit in range(units):
                m, mc = carries[unit]
                f_next = min(f + 1, window - 1)
                mc_next = qk_far(unit, far_chunk(f_next), s_bufs[unit][(f + 1) % 2], far_shift(f_next))
                carries[unit] = (softmax_pv_far(unit, s_bufs[unit][f % 2], mc, m, far_chunk(f), far_shift(f)),
                                 mc_next)
        return tuple(carries)

    one = jnp.minimum(pl.program_id(0), 0) + 1
    carries = tuple(setup(unit, w0) for unit in range(units))
    carries = lax.fori_loop(0, one, near_block, carries)
    lax.fori_loop(0, one, far_block, carries)
    for unit in range(units):
        finalize(unit)


def _attn_a(scal, qat, ka, vat, bias_tiles, sg_b, gate, tq, tk, out_scale):
    b, s, _ = ka.shape
    n_chunks = s // tk
    ratio = tk // tq
    unroll, units = _loop_shape(n_chunks, s // tq, loop_units=2)
    assert n_chunks % unroll == 0 and ratio * tq == tk and ratio & (ratio - 1) == 0
    kern = functools.partial(_attn_a_kernel, tq=tq, tk=tk, n_chunks=n_chunks, ratio=ratio, unroll=unroll,
                             units=units, out_scale=out_scale)
    return pl.pallas_call(
        kern,
        grid=(b, A_HEADS, s // (tq * units)),
        in_specs=[
            pl.BlockSpec(memory_space=pltpu.SMEM),
            pl.BlockSpec((1, 2 * A_HEAD_DIM, tq * units), lambda i, hh, j: (i, hh, j)),
            pl.BlockSpec((1, s, 2 * A_HEAD_DIM), lambda i, hh, j: (i, 0, hh)),
            pl.BlockSpec((1, 1, A_V_DIM + ONES_ROWS, s), lambda i, hh, j: (i, hh, 0, 0)),
            pl.BlockSpec((1, 3 * ratio + 2, tk, tq), lambda i, hh, j: (hh, 0, 0, 0)),
            pl.BlockSpec((A_V_DIM, tq), lambda i, hh, j: (0, 0)),
            pl.BlockSpec((1, tq * units, A_V_DIM), lambda i, hh, j: (i, j, hh)),
        ],
        out_specs=pl.BlockSpec((1, tq * units, A_V_DIM), lambda i, hh, j: (i, j, hh)),
        out_shape=jax.ShapeDtypeStruct((b, s, A_WIDTH), BF16),
        scratch_shapes=[
            pltpu.VMEM((2 * A_HEAD_DIM, 2 * tq), BF16),
            pltpu.VMEM((A_V_DIM + ONES_ROWS, 2 * tq), F32),
            pltpu.VMEM((tk, 2 * tq), F32),
            pltpu.VMEM((tk, 2 * tq), F32),
        ] * units,
        compiler_params=pltpu.CompilerParams(
            dimension_semantics=("arbitrary", "arbitrary", "arbitrary"), vmem_limit_bytes=VMEM_LIMIT_BYTES),
        name="attn_a",
    )(scal, qat, ka, vat, bias_tiles, sg_b, gate)


def _attn_b_kernel(qt_ref, k_ref, vt_ref, gate_ref, o_ref, *scratch, tq, tk, n_chunks, unroll, units):
    half = B_GROUP * tq
    lanes = B_HEADS * tq
    wq_refs = [scratch[5 * unit] for unit in range(units)]
    acc_refs = [scratch[5 * unit + 1:5 * unit + 3] for unit in range(units)]
    s_bufs = [scratch[5 * unit + 3:5 * unit + 5] for unit in range(units)]

    def qk(unit, c, s_ref):
        start = pl.multiple_of(c * tk, tk)
        s = jnp.dot(k_ref[0, pl.ds(start, tk), :], wq_refs[unit][...], preferred_element_type=F32)
        s_ref[...] = s
        return jnp.max(s, axis=0, keepdims=True)

    def softmax_pv(unit, s_ref, mc, m, c):
        acc0_ref, acc1_ref = acc_refs[unit]
        mn = jnp.maximum(m, mc)
        alpha = jnp.exp2(m - mn)
        p = jnp.exp2(s_ref[...] - mn).astype(BF16)
        start = pl.multiple_of(c * tk, tk)
        v0 = vt_ref[0, 0, :, pl.ds(start, tk)]
        v1 = vt_ref[0, 1, :, pl.ds(start, tk)]
        acc0_ref[...] = acc0_ref[...] * alpha[:, 0:half] + jnp.dot(v0, p[:, 0:half], preferred_element_type=F32)
        acc1_ref[...] = acc1_ref[...] * alpha[:, half:] + jnp.dot(v1, p[:, half:], preferred_element_type=F32)
        return mn

    def setup(unit):
        wq_refs[unit][...] = jnp.zeros(wq_refs[unit].shape, BF16)
        for hh in range(B_HEADS):
            n = hh // B_GROUP
            wq_refs[unit][n * 64:(n + 1) * 64, hh * tq:(hh + 1) * tq] = qt_ref[0, hh * 64:(hh + 1) * 64,
                                                                               unit * tq:(unit + 1) * tq]
        for acc_ref in acc_refs[unit]:
            acc_ref[...] = jnp.zeros(acc_ref.shape, F32)
        return jnp.full((1, lanes), M_INIT, F32), qk(unit, 0, s_bufs[unit][0])

    def sub_step(unit, u, c0, carry):
        m, mc = carry
        mc_next = qk(unit, jnp.minimum(c0 + u + 1, n_chunks - 1), s_bufs[unit][(u + 1) % 2])
        return softmax_pv(unit, s_bufs[unit][u % 2], mc, m, c0 + u), mc_next

    def finalize(unit):
        parts = []
        for acc_ref in acc_refs[unit]:
            acc = acc_ref[...]
            o = acc[0:64, :] * (1.0 / acc[64:65, :])
            for g in range(B_GROUP):
                parts.append(o[:, g * tq:(g + 1) * tq])
        rows = slice(unit * tq, (unit + 1) * tq)
        out = jnp.concatenate(parts, axis=0).T
        o_ref[0, rows, :] = (out * gate_ref[0, rows, :]).astype(BF16)

    _run_tiles(units, unroll, n_chunks, setup, sub_step, finalize)


def _attn_b(qbt, kb, vbt, gate, tq, tk):
    b, s, _ = kb.shape
    n_chunks = s // tk
    unroll, units = _loop_shape(n_chunks, s // tq, loop_units=1)
    kern = functools.partial(_attn_b_kernel, tq=tq, tk=tk, n_chunks=n_chunks, unroll=unroll, units=units)
    return pl.pallas_call(
        kern,
        grid=(b, s // (tq * units)),
        in_specs=[
            pl.BlockSpec((1, B_WIDTH, tq * units), lambda i, j: (i, 0, j)),
            pl.BlockSpec((1, s, B_KV_HEADS * B_HEAD_DIM), lambda i, j: (i, 0, 0)),
            pl.BlockSpec((1, B_KV_HEADS, B_HEAD_DIM + ONES_ROWS, s), lambda i, j: (i, 0, 0, 0)),
            pl.BlockSpec((1, tq * units, B_WIDTH), lambda i, j: (i, j, 1)),
        ],
        out_specs=pl.BlockSpec((1, tq * units, B_WIDTH), lambda i, j: (i, j, 0)),
        out_shape=jax.ShapeDtypeStruct((b, s, B_WIDTH), BF16),
        scratch_shapes=[
            pltpu.VMEM((B_KV_HEADS * B_HEAD_DIM, B_HEADS * tq), BF16),
            pltpu.VMEM((B_HEAD_DIM + ONES_ROWS, B_GROUP * tq), F32),
            pltpu.VMEM((B_HEAD_DIM + ONES_ROWS, B_GROUP * tq), F32),
            pltpu.VMEM((tk, B_HEADS * tq), F32),
            pltpu.VMEM((tk, B_HEADS * tq), F32),
        ] * units,
        compiler_params=pltpu.CompilerParams(
            dimension_semantics=("arbitrary", "arbitrary"), vmem_limit_bytes=VMEM_LIMIT_BYTES),
        name="attn_b",
    )(qbt, kb, vbt, gate)


def _outproj_kernel(ma_ref, mb_ref, x_ref, mod_ref, w_ref, fg_ref, o_ref, *, final):
    y = (jnp.dot(ma_ref[0], w_ref[0:A_WIDTH, :], preferred_element_type=F32)
         + jnp.dot(mb_ref[0], w_ref[A_WIDTH:D_MODEL, :], preferred_element_type=F32))
    xn = x_ref[0] + mod_ref[0, 2:3, :] * y
    if final:
        xn = _rms_rows(xn, fg_ref[...])
    o_ref[0] = xn


def _outproj(ma, mb, x, mod, w, fg, tm, final):
    b, s, _ = x.shape
    tok = lambda i, j: (i, j, 0)
    return pl.pallas_call(
        functools.partial(_outproj_kernel, final=final),
        grid=(b, s // tm),
        in_specs=[
            pl.BlockSpec((1, tm, A_WIDTH), tok),
            pl.BlockSpec((1, tm, B_WIDTH), tok),
            pl.BlockSpec((1, tm, D_MODEL), tok),
            pl.BlockSpec((1, 3, D_MODEL), lambda i, j: (i, 0, 0)),
            pl.BlockSpec((D_MODEL, D_MODEL), lambda i, j: (0, 0)),
            pl.BlockSpec((1, D_MODEL), lambda i, j: (0, 0)),
        ],
        out_specs=pl.BlockSpec((1, tm, D_MODEL), tok),
        out_shape=jax.ShapeDtypeStruct((b, s, D_MODEL), F32),
        compiler_params=pltpu.CompilerParams(
            dimension_semantics=("arbitrary", "arbitrary"), vmem_limit_bytes=VMEM_LIMIT_BYTES),
        name="outproj",
    )(ma, mb, x, mod, w, fg)


def _t5_bucket(rel):
    half = NUM_BUCKETS // 2
    max_exact = half // 2
    ret = jnp.where(rel > 0, half, 0)
    n = jnp.abs(rel)
    nf = jnp.maximum(n, 1).astype(jnp.float32)
    large = max_exact + (jnp.log(nf / max_exact) / math.log(MAX_DISTANCE / max_exact)
                         * (half - max_exact)).astype(jnp.int32)
    large = jnp.minimum(large, half - 1)
    return ret + jnp.where(n < max_exact, n, large)


def _bias_tables(rel_table, s, tq, tk):
    ratio = tk // tq
    assert tk >= MAX_DISTANCE and s - 1 >= MAX_DISTANCE
    offsets = jnp.arange(-MAX_DISTANCE, MAX_DISTANCE + 1, dtype=jnp.int32)
    core = rel_table[_t5_bucket(offsets)].T.astype(F32) * LOG2E
    reach = 2 * tk + tq
    boff_p = jnp.pad(core, ((0, 0), (reach - MAX_DISTANCE, reach - MAX_DISTANCE)), mode="edge")
    span = tk + tq - 1
    tiles = []
    for dc in (-1, 0, 1):
        for r in range(ratio):
            first = dc * tk - r * tq - (tq - 1) + reach
            u = jnp.flip(boff_p[:, first:first + span], axis=1)
            hankel = jnp.tile(u, (1, tk + 1))[:, :tk * (span + 1)].reshape(A_HEADS, tk, span + 1)[:, :, :tq]
            tiles.append(jnp.flip(hankel, axis=1))
    for col in (0, 2 * reach):
        tiles.append(jnp.broadcast_to(boff_p[:, col][:, None, None], (A_HEADS, tk, tq)))
    return jnp.stack(tiles, axis=1), boff_p[:, 0], boff_p[:, 2 * reach]


def _rope_tables_t(s):
    rows = s // GRID_W
    row = jnp.repeat(jnp.arange(rows), GRID_W).astype(F32)
    col = jnp.tile(jnp.arange(GRID_W), rows).astype(F32)
    axis_dim = B_HEAD_DIM // 2
    inv_freq = ROPE_THETA ** (-jnp.arange(0, axis_dim, 2, dtype=F32) / axis_dim)
    ang_r = row[:, None] * inv_freq[None, :]
    ang_c = col[:, None] * inv_freq[None, :]
    ang = jnp.concatenate([ang_r, ang_r, ang_c, ang_c], axis=-1)
    return jnp.cos(ang).T, jnp.sin(ang).T


def _tile_sizes(s):
    del s
    return (512, 1024), 1024, (256, 512), (128, 256)


def _trunk(x, mod, lam, rel_table, norm_g, wn, wt, subln_g, q_norm_g, k_norm_g, w_out, final_g):
    b, s, _ = x.shape
    (ts_in, tm), tm_out, (tq_a, tk_a), (tq_b, tk_b) = _tile_sizes(s)
    cos_t, sin_t = _rope_tables_t(s)
    bias_tiles, c_neg, c_pos = _bias_tables(rel_table, s, tq_a, tk_a)
    fg = final_g.reshape(1, D_MODEL)
    prev = None
    for l in range(DEPTH):
        lam_init = 0.8 - 0.6 * math.exp(-0.3 * l)
        qg_b =jnp.broadcast_to(q_norm_g[l][:, None], (B_HEAD_DIM, ts_in))
        kg_b = jnp.broadcast_to(k_norm_g[l][:, None], (B_HEAD_DIM, ts_in))
        sg_b = jnp.broadcast_to(subln_g[l][:, None], (A_V_DIM, tq_a))
        outs = _inproj(x, mod[l], norm_g[l].reshape(1, D_MODEL), wn[l], wt[l], cos_t, sin_t, qg_b, kg_b, tm, prev=prev)
        if prev is not None:
            x, outs = outs[0], outs[1:]
        ka, gate, qat, vat, qbt, kb, vbt = outs
        scal = jnp.concatenate([lam[l:l + 1], c_neg, c_pos]).astype(F32)
        ma = _attn_a(scal, qat, ka, vat, bias_tiles, sg_b, gate, tq_a, tk_a, 1.0 - lam_init)
        mb = _attn_b(qbt, kb, vbt, gate, tq_b, tk_b)
        prev = (ma, mb, mod[l], w_out[l])
    return _outproj(ma, mb, x, mod[DEPTH - 1], w_out[DEPTH - 1], fg, tm_out, final=True)


def kernel(x_prompt, x_sample, c_prompt, c_sample, rel_table, norm_g, w_ada, b_ada, w_in, lam_q1, lam_k1, lam_q2,
           lam_k2, subln_g, q_norm_g, k_norm_g, w_out, final_g):
    bp = x_prompt.shape[0]
    bs = x_sample.shape[0]
    rows = -(-(bp + bs) // 8) * 8
    c_all = jnp.concatenate([c_prompt, c_sample, jnp.zeros((rows - bp - bs, D_MODEL), F32)], axis=0)
    mod_all = _adaln_mod(c_all, w_ada.astype(BF16), b_ada)
    mod_p = mod_all[:, :bp].reshape(DEPTH, bp, 3, D_MODEL)
    mod_s = mod_all[:, bp:bp + bs].reshape(DEPTH, bs, 3, D_MODEL)
    lam = _lambdas(lam_q1, lam_k1, lam_q2, lam_k2)

    w_bf = w_in.astype(BF16)
    wn = jnp.concatenate([w_bf[:, :, _C_KA:_C_VA], w_bf[:, :, _C_GA:_C_QB], w_bf[:, :, _C_GB:D_IN]], axis=2)
    wt = jnp.concatenate([w_bf[:, :, _C_QA:_C_KA], w_bf[:, :, _C_VA:_C_GA], w_bf[:, :, _C_QB:_C_GB]], axis=2)
    wt = jnp.swapaxes(wt, 1, 2)
    w_out_bf = w_out.astype(BF16)

    args = (rel_table, norm_g, wn, wt, subln_g, q_norm_g, k_norm_g, w_out_bf, final_g)
    y_prompt = _trunk(x_prompt, mod_p, lam, *args)
    y_sample = _trunk(x_sample, mod_s, lam, *args)
    return (y_prompt, y_sample)
```

```python
import functools
import math

import jax
import jax.numpy as jnp
from jax import lax
from jax.experimental import pallas as pl
from jax.experimental.pallas import tpu as pltpu

F32 = jnp.float32
BF16 = jnp.bfloat16

D_MODEL = 1024
DEPTH = 4
GRID_W = 64
A_WIDTH = 512
A_V_DIM = 128
A_HEAD_DIM = 64
A_HEADS = 4
B_WIDTH = 512
B_HEAD_DIM = 64
B_HEADS = 8
B_KV_HEADS = 2
B_GROUP = 4
NUM_BUCKETS = 32
MAX_DISTANCE = 128
ROPE_THETA = 10000.0
EPS = 1e-6

LOG2E = 1.4426950408889634
Q_SCALE = (A_HEAD_DIM ** -0.5) * LOG2E
M_INIT = -1e30
ONES_ROWS = 16
VMEM_LIMIT_BYTES = 56 * 1024 * 1024
UNITS_PER_STEP = 8
MAX_STATIC_TRIPS = 4
LOOP_BODY_CHUNKS = 32

_C_QA, _C_KA, _C_VA, _C_GA, _C_QB, _C_KB, _C_VB, _C_GB = 0, 512, 1024, 1536, 2048, 2560, 2688, 2816
D_IN = 3328
N_NAT = 1536
N_TR = 1792


def _rms_rows(x, g):
    ms = jnp.mean(x * x, axis=-1, keepdims=True)
    return x * lax.rsqrt(ms + EPS) * g


def _mod_kernel(c_ref, w_ref, b_ref, o_ref):
    c = c_ref[...]
    c_act = (c * jax.nn.sigmoid(c)).astype(BF16)
    o_ref[0] = jnp.dot(c_act, w_ref[0], preferred_element_type=F32) + b_ref[0]


def _adaln_mod(c_all, w_ada_bf, b_ada):
    rows = c_all.shape[0]
    return pl.pallas_call(
        _mod_kernel,
        grid=(DEPTH, 3),
        in_specs=[
            pl.BlockSpec((rows, D_MODEL), lambda l, j: (0, 0)),
            pl.BlockSpec((1, D_MODEL, D_MODEL), lambda l, j: (l, 0, j)),
            pl.BlockSpec((1, 1, D_MODEL), lambda l, j: (l, 0, j)),
        ],
        out_specs=pl.BlockSpec((1, rows, D_MODEL), lambda l, j: (l, 0, j)),
        out_shape=jax.ShapeDtypeStruct((DEPTH, rows, 3 * D_MODEL), F32),
        name="adaln_mod",
    )(c_all, w_ada_bf, b_ada.reshape(DEPTH, 1, 3 * D_MODEL))


def _lam_kernel(q1_ref, k1_ref, q2_ref, k2_ref, init_ref, o_ref):
    s1 = jnp.sum(q1_ref[...] * k1_ref[...], axis=-1, keepdims=True)
    s2 = jnp.sum(q2_ref[...] * k2_ref[...], axis=-1, keepdims=True)
    lam = jnp.exp(s1) - jnp.exp(s2) + init_ref[...][:, 0:1]
    o_ref[...] = jnp.broadcast_to(lam, o_ref.shape)


def _lambdas(lam_q1, lam_k1, lam_q2, lam_k2):
    init = jnp.asarray([[0.8 - 0.6 * math.exp(-0.3 * l)] * 128 for l in range(DEPTH)], F32)
    out = pl.pallas_call(
        _lam_kernel,
        out_shape=jax.ShapeDtypeStruct((DEPTH, 128), F32),
        name="diff_lambda",
    )(lam_q1, lam_k1, lam_q2, lam_k2, init)
    return out[:, 0]


def _rope_t(x, cos, sin):
    rot = jnp.concatenate([-x[16:32], x[0:16], -x[48:64], x[32:48]], axis=0)
    return x * cos + rot * sin


def _norm_t(x, g):
    ms = jnp.mean(x * x, axis=0, keepdims=True)
    return x * lax.rsqrt(ms + EPS) * g


def _inproj_kernel(x_ref, mod_ref, ng_ref, wn_ref, wt_ref, cos_ref, sin_ref, qg_ref, kg_ref,
                   ka_ref, gate_ref, qat_ref, vat_ref, qbt_ref, kb_ref, vbt_ref):
    ts = qg_ref.shape[1]
    for sub in range(x_ref.shape[1] // ts):
        rows = slice(sub * ts, (sub + 1) * ts)
        _inproj_subtile(rows, x_ref[0, rows, :], mod_ref, ng_ref, wn_ref, wt_ref, cos_ref, sin_ref,
                        qg_ref, kg_ref, ka_ref, gate_ref, qat_ref, vat_ref, qbt_ref, kb_ref, vbt_ref)


def _mid_kernel(ma_ref, mb_ref, x_ref, modp_ref, wo_ref, mod_ref, ng_ref, wn_ref, wt_ref, cos_ref, sin_ref,
                qg_ref, kg_ref, xn_ref, ka_ref, gate_ref, qat_ref, vat_ref, qbt_ref, kb_ref, vbt_ref):
    ts = qg_ref.shape[1]
    for sub in range(x_ref.shape[1] // ts):
        rows = slice(sub * ts, (sub + 1) * ts)
        y = (jnp.dot(ma_ref[0, rows, :], wo_ref[0:A_WIDTH, :], preferred_element_type=F32)
             + jnp.dot(mb_ref[0, rows, :], wo_ref[A_WIDTH:D_MODEL, :], preferred_element_type=F32))
        xn = x_ref[0, rows, :] + modp_ref[0, 2:3, :] * y
        xn_ref[0, rows, :] = xn
        _inproj_subtile(rows, xn, mod_ref, ng_ref, wn_ref, wt_ref, cos_ref, sin_ref,
                        qg_ref, kg_ref, ka_ref, gate_ref, qat_ref, vat_ref, qbt_ref, kb_ref, vbt_ref)


def _inproj_subtile(rows, x, mod_ref, ng_ref, wn_ref, wt_ref, cos_ref, sin_ref, qg_ref, kg_ref,
                    ka_ref, gate_ref, qat_ref, vat_ref, qbt_ref, kb_ref, vbt_ref):
    ts = rows.stop - rows.start
    shift = mod_ref[0, 0:1, :]
    scale = mod_ref[0, 1:2, :]
    h = (_rms_rows(x, ng_ref[...]) * (1.0 + scale) + shift).astype(BF16)

    pn = jnp.dot(h, wn_ref[...], preferred_element_type=F32)
    ka_ref[0, rows, :] = pn[:, 0:512].astype(BF16)
    g = pn[:, 512:N_NAT]
    gate_ref[0, rows, :] = g * jax.nn.sigmoid(g)

    pt = lax.dot_general(wt_ref[...], h, (((1,), (1,)), ((), ())), preferred_element_type=F32)
    qat_ref[0, :, rows] = (pt[0:512] * Q_SCALE).astype(BF16)
    ones = jnp.ones((ONES_ROWS, ts), BF16)
    for hh in range(A_HEADS):
        vat_ref[0, hh, 0:A_V_DIM, rows] = pt[512 + hh * A_V_DIM:512 + (hh + 1) * A_V_DIM].astype(BF16)
        vat_ref[0, hh, A_V_DIM:A_V_DIM + ONES_ROWS, rows] = ones

    cos = cos_ref[:, rows]
    sin = sin_ref[:, rows]
    qg = qg_ref[...]
    for hh in range(B_HEADS):
        xh = pt[1024 + hh * 64:1024 + (hh + 1) * 64]
        qbt_ref[0, hh * 64:(hh + 1) * 64, rows] = (_rope_t(_norm_t(xh, qg), cos, sin) * Q_SCALE).astype(BF16)
    kg = kg_ref[...]
    kparts = []
    for n in range(B_KV_HEADS):
        xh = pt[1536 + n * 64:1536 + (n + 1) * 64]
        kparts.append(_rope_t(_norm_t(xh, kg), cos, sin))
    kb_ref[0, rows, :] = jnp.concatenate(kparts, axis=0).T.astype(BF16)
    for n in range(B_KV_HEADS):
        vbt_ref[0, n, 0:64, rows] = pt[1664 + n * 64:1664 + (n + 1) * 64].astype(BF16)
        vbt_ref[0, n, 64:64 + ONES_ROWS, rows] = ones


def _inproj(x, mod, ng, wn, wt, cos_t, sin_t, qg_b, kg_b, tm, prev=None):
    b, s, _ = x.shape
    grid = (b, s // tm)
    ts = qg_b.shape[1]
    const2 = lambda i, j: (0, 0)
    tok = lambda i, j: (i, j, 0)
    per_batch = lambda i, j: (i, 0, 0)
    fused_in, fused_args, fused_out_specs, fused_out_shape = [], [], [], []
    if prev is not None:
        ma, mb, mod_prev, w_prev = prev
        fused_in = [pl.BlockSpec((1, tm, A_WIDTH), tok), pl.BlockSpec((1, tm, B_WIDTH), tok)]
        fused_args = [ma, mb]
        fused_out_specs = [pl.BlockSpec((1, tm, D_MODEL), tok)]
        fused_out_shape = [jax.ShapeDtypeStruct((b, s, D_MODEL), F32)]
    return pl.pallas_call(
        _inproj_kernel if prev is None else _mid_kernel,
        grid=grid,
        in_specs=fused_in + [
            pl.BlockSpec((1, tm, D_MODEL), tok),
        ] + ([] if prev is None else [
            pl.BlockSpec((1, 3, D_MODEL), per_batch),
            pl.BlockSpec((D_MODEL, D_MODEL), const2, pipeline_mode=pl.Buffered(1)),
        ]) + [
            pl.BlockSpec((1, 3, D_MODEL), per_batch),
            pl.BlockSpec((1, D_MODEL), const2),
            pl.BlockSpec((D_MODEL, N_NAT), const2, pipeline_mode=pl.Buffered(1)),
            pl.BlockSpec((N_TR, D_MODEL), const2, pipeline_mode=pl.Buffered(1)),
            pl.BlockSpec((B_HEAD_DIM, tm), lambda i, j: (0, j)),
            pl.BlockSpec((B_HEAD_DIM, tm), lambda i, j: (0, j)),
            pl.BlockSpec((B_HEAD_DIM, ts), const2),
            pl.BlockSpec((B_HEAD_DIM, ts), const2),
        ],
        out_specs=fused_out_specs + [
            pl.BlockSpec((1, tm, A_WIDTH), lambda i, j: (i, j, 0)),
            pl.BlockSpec((1, tm, D_MODEL), lambda i, j: (i, j, 0)),
            pl.BlockSpec((1, A_WIDTH, tm), lambda i, j: (i, 0, j)),
            pl.BlockSpec((1, A_HEADS, A_V_DIM + ONES_ROWS, tm), lambda i, j: (i, 0, 0, j)),
            pl.BlockSpec((1, B_WIDTH, tm), lambda i, j: (i, 0, j)),
            pl.BlockSpec((1, tm, B_KV_HEADS * B_HEAD_DIM), lambda i, j: (i, j, 0)),
            pl.BlockSpec((1, B_KV_HEADS, B_HEAD_DIM + ONES_ROWS, tm), lambda i, j: (i, 0, 0, j)),
        ],
        out_shape=fused_out_shape + [
            jax.ShapeDtypeStruct((b, s, A_WIDTH), BF16),
            jax.ShapeDtypeStruct((b, s, D_MODEL), F32),
            jax.ShapeDtypeStruct((b, A_WIDTH, s), BF16),
            jax.ShapeDtypeStruct((b, A_HEADS, A_V_DIM + ONES_ROWS, s), BF16),
            jax.ShapeDtypeStruct((b, B_WIDTH, s), BF16),
            jax.ShapeDtypeStruct((b, s, B_KV_HEADS * B_HEAD_DIM), BF16),
            jax.ShapeDtypeStruct((b, B_KV_HEADS, B_HEAD_DIM + ONES_ROWS, s), BF16),
        ],
        compiler_params=pltpu.CompilerParams(
            dimension_semantics=("arbitrary", "arbitrary"), vmem_limit_bytes=VMEM_LIMIT_BYTES),
        name="inproj" if prev is None else "outproj_inproj",
    )(*fused_args, x, *(() if prev is None else (mod_prev, w_prev)), mod, ng, wn, wt, cos_t, sin_t, qg_b, kg_b)


def _loop_shape(n_chunks, n_tiles, loop_units):
    if n_chunks <= LOOP_BODY_CHUNKS // 2:
        unroll, units = n_chunks, UNITS_PER_STEP
    else:
        unroll, units = LOOP_BODY_CHUNKS // loop_units, loop_units
    assert n_chunks % unroll == 0 and unroll % 2 == 0
    while n_tiles % units:
        units //= 2
    return unroll, units


def _run_tiles(units, unroll, n_chunks, setup, sub_step, finalize):
    if n_chunks == unroll:
        for unit in range(units):
            carry = setup(unit)
            for u in range(unroll):
                carry = sub_step(unit, u, 0, carry)
            finalize(unit)
        return
    carries = tuple(setup(unit) for unit in range(units))

    def body(c0, carries):
        carries = list(carries)
        for u in range(unroll):
            for unit in range(units):
                carries[unit] = sub_step(unit, u, c0, carries[unit])
        return tuple(carries)

    n_trips = n_chunks // unroll
    if n_trips <= MAX_STATIC_TRIPS:
        one = jnp.minimum(pl.program_id(0), 0) + 1
        for trip in range(n_trips):
            carries = lax.fori_loop(0, one, lambda _, c, trip=trip: body(unroll * trip, c), carries)
    else:
        lax.fori_loop(0, n_trips, lambda i, c: body(unroll * i, c), carries)
    for unit in range(units):
        finalize(unit)


def _attn_a_kernel(scal_ref, qt_ref, k_ref, vt_ref, bias_ref, sg_ref, gate_ref, o_ref, *scratch,
                   tq, tk, n_chunks, ratio, unroll, units, out_scale):
    lam = scal_ref[0]
    log_ratio = ratio.bit_length() - 1
    wq_refs = [scratch[4 * unit] for unit in range(units)]
    acc_refs = [scratch[4 * unit + 1] for unit in range(units)]
    s_bufs = [scratch[4 * unit + 2:4 * unit + 4] for unit in range(units)]
    tiles = [pl.program_id(2) * units + unit for unit in range(units)]

    def qk(unit, c, s_ref):
        start = pl.multiple_of(c * tk, tk)
        d = c - lax.shift_right_logical(tiles[unit], log_ratio)
        r = lax.bitwise_and(tiles[unit], ratio - 1)
        tile = jnp.where(d < -1, 3 * ratio, jnp.where(d > 1, 3 * ratio + 1, (d + 1) * ratio + r))
        t = bias_ref[0, tile]
        s = jnp.dot(k_ref[0, pl.ds(start, tk), :], wq_refs[unit][...], preferred_element_type=F32)
        s = s + jnp.concatenate([t, t], axis=1)
        s_ref[...] = s
        return jnp.max(s, axis=0, keepdims=True)

    def softmax_pv(unit, s_ref, mc, m, c):
        mn = jnp.maximum(m, mc)
        alpha = jnp.exp2(m - mn)
        p = jnp.exp2(s_ref[...] - mn).astype(BF16)
        start = pl.multiple_of(c * tk, tk)
        pv = jnp.dot(vt_ref[0, 0, :, pl.ds(start, tk)], p, preferred_element_type=F32)
        acc_refs[unit][...] = acc_refs[unit][...] * alpha + pv
        return mn

    def setup(unit, c_first=0):
        q = qt_ref[0, :, unit * tq:(unit + 1) * tq]
        zero = jnp.zeros((A_HEAD_DIM, tq), BF16)
        wq_refs[unit][0:64, 0:tq] = q[0:64]
        wq_refs[unit][0:64, tq:2 * tq] = zero
        wq_refs[unit][64:128, 0:tq] = zero
        wq_refs[unit][64:128, tq:2 * tq] = q[64:128]
        acc_refs[unit][...] = jnp.zeros(acc_refs[unit].shape, F32)
        return jnp.full((1, 2 * tq), M_INIT, F32), qk(unit, c_first, s_bufs[unit][0])

    def sub_step(unit, u, c0, carry):
        m, mc = carry
        mc_next = qk(unit, jnp.minimum(c0 + u + 1, n_chunks - 1), s_bufs[unit][(u + 1) % 2])
        return softmax_pv(unit, s_bufs[unit][u % 2], mc, m, c0 + u), mc_next

    def finalize(unit):
        acc = acc_refs[unit][...]
        o = acc[0:A_V_DIM, :] * (1.0 / acc[A_V_DIM:A_V_DIM + 1, :])
        w = o[:, 0:tq] - lam * o[:, tq:2 * tq]
        y = _norm_t(w, sg_ref[...]) * out_scale
        rows = slice(unit * tq, (unit + 1) * tq)
        o_ref[0, rows, :] = (y.T * gate_ref[0, rows, :]).astype(BF16)

    if n_chunks != 2 * unroll:
        _run_tiles(units, unroll, n_chunks, setup, sub_step, finalize)
        return

    h = pl.program_id(1)
    c_neg = scal_ref[1 + h]
    c_pos = scal_ref[1 + A_HEADS + h]
    window = unroll
    w0 = jnp.clip(lax.shift_right_logical(tiles[0], log_ratio) - window // 2, 0, n_chunks - window)

    def far_chunk(f):
        return jnp.where(f < w0, f, f + window)

    def far_shift(f):
        return jnp.where(f < w0, c_neg, c_pos)

    def qk_far(unit, c, s_ref, shift):
        start = pl.multiple_of(c * tk, tk)
        s = jnp.dot(k_ref[0, pl.ds(start, tk), :], wq_refs[unit][...], preferred_element_type=F32)
        s_ref[...] = s
        return jnp.max(s, axis=0, keepdims=True) + shift

    def softmax_pv_far(unit, s_ref, mc, m, c, shift):
        mn = jnp.maximum(m, mc)
        alpha = jnp.exp2(m - mn)
        p = jnp.exp2(s_ref[...] - (mn - shift)).astype(BF16)
        start = pl.multiple_of(c * tk, tk)
        pv = jnp.dot(vt_ref[0, 0, :, pl.ds(start, tk)], p, preferred_element_type=F32)
        acc_refs[unit][...] = acc_refs[unit][...] * alpha + pv
        return mn

    def near_block(_, carries):
        carries = list(carries)
        for u in range(window):
            for unit in range(units):
                m, mc = carries[unit]
                nxt = s_bufs[unit][(u + 1) % 2]
                if u + 1 < window:
                    mc_next = qk(unit, w0 + u + 1, nxt)
                else:
                    mc_next = qk_far(unit, far_chunk(0), nxt, far_shift(0))
                carries[unit] = (softmax_pv(unit, s_bufs[unit][u % 2], mc, m, w0 + u), mc_next)
        return tuple(carries)

    def far_block(_, carries):
        carries = list(carries)
        for f in range(window):
            for unit in range(units):
                m, mc = carries[unit]
                f_next = min(f + 1, window - 1)
                mc_next = qk_far(unit, far_chunk(f_next), s_bufs[unit][(f + 1) % 2], far_shift(f_next))
                carries[unit] = (softmax_pv_far(unit, s_bufs[unit][f % 2], mc, m, far_chunk(f), far_shift(f)),
                                 mc_next)
        return tuple(carries)

    one = jnp.minimum(pl.program_id(0), 0) + 1
    carries = tuple(setup(unit, w0) for unit in range(units))
    carries = lax.fori_loop(0, one, near_block, carries)
    lax.fori_loop(0, one, far_block, carries)
    for unit in range(units):
        finalize(unit)


def _attn_a(scal, qat, ka, vat, bias_tiles, sg_b, gate, tq, tk, out_scale):
    b, s, _ = ka.shape
    n_chunks = s // tk
    ratio = tk // tq
    unroll, units = _loop_shape(n_chunks, s // tq, loop_units=2)
    assert n_chunks % unroll == 0 and ratio * tq == tk and ratio & (ratio - 1) == 0
    kern = functools.partial(_attn_a_kernel, tq=tq, tk=tk, n_chunks=n_chunks, ratio=ratio, unroll=unroll,
                             units=units, out_scale=out_scale)
    return pl.pallas_call(
        kern,
        grid=(b, A_HEADS, s // (tq * units)),
        in_specs=[
            pl.BlockSpec(memory_space=pltpu.SMEM),
            pl.BlockSpec((1, 2 * A_HEAD_DIM, tq * units), lambda i, hh, j: (i, hh, j)),
            pl.BlockSpec((1, s, 2 * A_HEAD_DIM), lambda i, hh, j: (i, 0, hh)),
            pl.BlockSpec((1, 1, A_V_DIM + ONES_ROWS, s), lambda i, hh, j: (i, hh, 0, 0)),
            pl.BlockSpec((1, 3 * ratio + 2, tk, tq), lambda i, hh, j: (hh, 0, 0, 0)),
            pl.BlockSpec((A_V_DIM, tq), lambda i, hh, j: (0, 0)),
            pl.BlockSpec((1, tq * units, A_V_DIM), lambda i, hh, j: (i, j, hh)),
        ],
        out_specs=pl.BlockSpec((1, tq * units, A_V_DIM), lambda i, hh, j: (i, j, hh)),
        out_shape=jax.ShapeDtypeStruct((b, s, A_WIDTH), BF16),
        scratch_shapes=[
            pltpu.VMEM((2 * A_HEAD_DIM, 2 * tq), BF16),
            pltpu.VMEM((A_V_DIM + ONES_ROWS, 2 * tq), F32),
            pltpu.VMEM((tk, 2 * tq), F32),
            pltpu.VMEM((tk, 2 * tq), F32),
        ] * units,
        compiler_params=pltpu.CompilerParams(
            dimension_semantics=("arbitrary", "arbitrary", "arbitrary"), vmem_limit_bytes=VMEM_LIMIT_BYTES),
        name="attn_a",
    )(scal, qat, ka, vat, bias_tiles, sg_b, gate)


def _attn_b_kernel(qt_ref, k_ref, vt_ref, gate_ref, o_ref, *scratch, tq, tk, n_chunks, unroll, units):
    half = B_GROUP * tq
    lanes = B_HEADS * tq
    wq_refs = [scratch[5 * unit] for unit in range(units)]
    acc_refs = [scratch[5 * unit + 1:5 * unit + 3] for unit in range(units)]
    s_bufs = [scratch[5 * unit + 3:5 * unit + 5] for unit in range(units)]

    def qk(unit, c, s_ref):
        start = pl.multiple_of(c * tk, tk)
        s = jnp.dot(k_ref[0, pl.ds(start, tk), :], wq_refs[unit][...], preferred_element_type=F32)
        s_ref[...] = s
        return jnp.max(s, axis=0, keepdims=True)

    def softmax_pv(unit, s_ref, mc, m, c):
        acc0_ref, acc1_ref = acc_refs[unit]
        mn = jnp.maximum(m, mc)
        alpha = jnp.exp2(m - mn)
        p = jnp.exp2(s_ref[...] - mn).astype(BF16)
        start = pl.multiple_of(c * tk, tk)
        v0 = vt_ref[0, 0, :, pl.ds(start, tk)]
        v1 = vt_ref[0, 1, :, pl.ds(start, tk)]
        acc0_ref[...] = acc0_ref[...] * alpha[:, 0:half] + jnp.dot(v0, p[:, 0:half], preferred_element_type=F32)
        acc1_ref[...] = acc1_ref[...] * alpha[:, half:] + jnp.dot(v1, p[:, half:], preferred_element_type=F32)
        return mn

    def setup(unit):
        wq_refs[unit][...] = jnp.zeros(wq_refs[unit].shape, BF16)
        for hh in range(B_HEADS):
            n = hh // B_GROUP
            wq_refs[unit][n * 64:(n + 1) * 64, hh * tq:(hh + 1) * tq] = qt_ref[0, hh * 64:(hh + 1) * 64,
                                                                               unit * tq:(unit + 1) * tq]
        for acc_ref in acc_refs[unit]:
            acc_ref[...] = jnp.zeros(acc_ref.shape, F32)
        return jnp.full((1, lanes), M_INIT, F32), qk(unit, 0, s_bufs[unit][0])

    def sub_step(unit, u, c0, carry):
        m, mc = carry
        mc_next = qk(unit, jnp.minimum(c0 + u + 1, n_chunks - 1), s_bufs[unit][(u + 1) % 2])
        return softmax_pv(unit, s_bufs[unit][u % 2], mc, m, c0 + u), mc_next

    def finalize(unit):
        parts = []
        for acc_ref in acc_refs[unit]:
            acc = acc_ref[...]
            o = acc[0:64, :] * (1.0 / acc[64:65, :])
            for g in range(B_GROUP):
                parts.append(o[:, g * tq:(g + 1) * tq])
        rows = slice(unit * tq, (unit + 1) * tq)
        out = jnp.concatenate(parts, axis=0).T
        o_ref[0, rows, :] = (out * gate_ref[0, rows, :]).astype(BF16)

    _run_tiles(units, unroll, n_chunks, setup, sub_step, finalize)


def _attn_b(qbt, kb, vbt, gate, tq, tk):
    b, s, _ = kb.shape
    n_chunks = s // tk
    unroll, units = _loop_shape(n_chunks, s // tq, loop_units=1)
    kern = functools.partial(_attn_b_kernel, tq=tq, tk=tk, n_chunks=n_chunks, unroll=unroll, units=units)
    return pl.pallas_call(
        kern,
        grid=(b, s // (tq * units)),
        in_specs=[
            pl.BlockSpec((1, B_WIDTH, tq * units), lambda i, j: (i, 0, j)),
            pl.BlockSpec((1, s, B_KV_HEADS * B_HEAD_DIM), lambda i, j: (i, 0, 0)),
            pl.BlockSpec((1, B_KV_HEADS, B_HEAD_DIM + ONES_ROWS, s), lambda i, j: (i, 0, 0, 0)),
            pl.BlockSpec((1, tq * units, B_WIDTH), lambda i, j: (i, j, 1)),
        ],
        out_specs=pl.BlockSpec((1, tq * units, B_WIDTH), lambda i, j: (i, j, 0)),
        out_shape=jax.ShapeDtypeStruct((b, s, B_WIDTH), BF16),
        scratch_shapes=[
            pltpu.VMEM((B_KV_HEADS * B_HEAD_DIM, B_HEADS * tq), BF16),
            pltpu.VMEM((B_HEAD_DIM + ONES_ROWS, B_GROUP * tq), F32),
            pltpu.VMEM((B_HEAD_DIM + ONES_ROWS, B_GROUP * tq), F32),
            pltpu.VMEM((tk, B_HEADS * tq), F32),
            pltpu.VMEM((tk, B_HEADS * tq), F32),
        ] * units,
        compiler_params=pltpu.CompilerParams(
            dimension_semantics=("arbitrary", "arbitrary"), vmem_limit_bytes=VMEM_LIMIT_BYTES),
        name="attn_b",
    )(qbt, kb, vbt, gate)


def _outproj_kernel(ma_ref, mb_ref, x_ref, mod_ref, w_ref, fg_ref, o_ref, *, final):
    y = (jnp.dot(ma_ref[0], w_ref[0:A_WIDTH, :], preferred_element_type=F32)
         + jnp.dot(mb_ref[0], w_ref[A_WIDTH:D_MODEL, :], preferred_element_type=F32))
    xn = x_ref[0] + mod_ref[0, 2:3, :] * y
    if final:
        xn = _rms_rows(xn, fg_ref[...])
    o_ref[0] = xn


def _outproj(ma, mb, x, mod, w, fg, tm, final):
    b, s, _ = x.shape
    tok = lambda i, j: (i, j, 0)
    return pl.pallas_call(
        functools.partial(_outproj_kernel, final=final),
        grid=(b, s // tm),
        in_specs=[
            pl.BlockSpec((1, tm, A_WIDTH), tok),
            pl.BlockSpec((1, tm, B_WIDTH), tok),
            pl.BlockSpec((1, tm, D_MODEL), tok),
            pl.BlockSpec((1, 3, D_MODEL), lambda i, j: (i, 0, 0)),
            pl.BlockSpec((D_MODEL, D_MODEL), lambda i, j: (0, 0)),
            pl.BlockSpec((1, D_MODEL), lambda i, j: (0, 0)),
        ],
        out_specs=pl.BlockSpec((1, tm, D_MODEL), tok),
        out_shape=jax.ShapeDtypeStruct((b, s, D_MODEL), F32),
        compiler_params=pltpu.CompilerParams(
            dimension_semantics=("arbitrary", "arbitrary"), vmem_limit_bytes=VMEM_LIMIT_BYTES),
        name="outproj",
    )(ma, mb, x, mod, w, fg)


def _t5_bucket(rel):
    half = NUM_BUCKETS // 2
    max_exact = half // 2
    ret = jnp.where(rel > 0, half, 0)
    n = jnp.abs(rel)
    nf = jnp.maximum(n, 1).astype(jnp.float32)
    large = max_exact + (jnp.log(nf / max_exact) / math.log(MAX_DISTANCE / max_exact)
                         * (half - max_exact)).astype(jnp.int32)
    large = jnp.minimum(large, half - 1)
    return ret + jnp.where(n < max_exact, n, large)


def _bias_tables(rel_table, s, tq, tk):
    ratio = tk // tq
    assert tk >= MAX_DISTANCE and s - 1 >= MAX_DISTANCE
    offsets = jnp.arange(-MAX_DISTANCE, MAX_DISTANCE + 1, dtype=jnp.int32)
    core = rel_table[_t5_bucket(offsets)].T.astype(F32) * LOG2E
    reach = 2 * tk + tq
    boff_p = jnp.pad(core, ((0, 0), (reach - MAX_DISTANCE, reach - MAX_DISTANCE)), mode="edge")
    span = tk + tq - 1
    tiles = []
    for dc in (-1, 0, 1):
        for r in range(ratio):
            first = dc * tk - r * tq - (tq - 1) + reach
            u = jnp.flip(boff_p[:, first:first + span], axis=1)
            hankel = jnp.tile(u, (1, tk + 1))[:, :tk * (span + 1)].reshape(A_HEADS, tk, span + 1)[:, :, :tq]
            tiles.append(jnp.flip(hankel, axis=1))
    for col in (0, 2 * reach):
        tiles.append(jnp.broadcast_to(boff_p[:, col][:, None, None], (A_HEADS, tk, tq)))
    return jnp.stack(tiles, axis=1), boff_p[:, 0], boff_p[:, 2 * reach]


def _rope_tables_t(s):
    rows = s // GRID_W
    row = jnp.repeat(jnp.arange(rows), GRID_W).astype(F32)
    col = jnp.tile(jnp.arange(GRID_W), rows).astype(F32)
    axis_dim = B_HEAD_DIM // 2
    inv_freq = ROPE_THETA ** (-jnp.arange(0, axis_dim, 2, dtype=F32) / axis_dim)
    ang_r = row[:, None] * inv_freq[None, :]
    ang_c = col[:, None] * inv_freq[None, :]
    ang = jnp.concatenate([ang_r, ang_r, ang_c, ang_c], axis=-1)
    return jnp.cos(ang).T, jnp.sin(ang).T


def _tile_sizes(s):
    del s
    return (512, 1024), 1024, (256, 512), (128, 256)


def _trunk(x, mod, lam, rel_table, norm_g, wn, wt, subln_g, q_norm_g, k_norm_g, w_out, final_g):
    b, s, _ = x.shape
    (ts_in, tm), tm_out, (tq_a, tk_a), (tq_b, tk_b) = _tile_sizes(s)
    cos_t, sin_t = _rope_tables_t(s)
    bias_tiles, c_neg, c_pos = _bias_tables(rel_table, s, tq_a, tk_a)
    fg = final_g.reshape(1, D_MODEL)
    prev = None
    for l in range(DEPTH):
        lam_init = 0.8 - 0.6 * math.exp(-0.3 * l)
        qg_b =jnp.broadcast_to(q_norm_g[l][:, None], (B_HEAD_DIM, ts_in))
        kg_b = jnp.broadcast_to(k_norm_g[l][:, None], (B_HEAD_DIM, ts_in))
        sg_b = jnp.broadcast_to(subln_g[l][:, None], (A_V_DIM, tq_a))
        outs = _inproj(x, mod[l], norm_g[l].reshape(1, D_MODEL), wn[l], wt[l], cos_t, sin_t, qg_b, kg_b, tm, prev=prev)
        if prev is not None:
            x, outs = outs[0], outs[1:]
        ka, gate, qat, vat, qbt, kb, vbt = outs
        scal = jnp.concatenate([lam[l:l + 1], c_neg, c_pos]).astype(F32)
        ma = _attn_a(scal, qat, ka, vat, bias_tiles, sg_b, gate, tq_a, tk_a, 1.0 - lam_init)
        mb = _attn_b(qbt, kb, vbt, gate, tq_b, tk_b)
        prev = (ma, mb, mod[l], w_out[l])
    return _outproj(ma, mb, x, mod[DEPTH - 1], w_out[DEPTH - 1], fg, tm_out, final=True)


def kernel(x_prompt, x_sample, c_prompt, c_sample, rel_table, norm_g, w_ada, b_ada, w_in, lam_q1, lam_k1, lam_q2,
           lam_k2, subln_g, q_norm_g, k_norm_g, w_out, final_g):
    bp = x_prompt.shape[0]
    bs = x_sample.shape[0]
    rows = -(-(bp + bs) // 8) * 8
    c_all = jnp.concatenate([c_prompt, c_sample, jnp.zeros((rows - bp - bs, D_MODEL), F32)], axis=0)
    mod_all = _adaln_mod(c_all, w_ada.astype(BF16), b_ada)
    mod_p = mod_all[:, :bp].reshape(DEPTH, bp, 3, D_MODEL)
    mod_s = mod_all[:, bp:bp + bs].reshape(DEPTH, bs, 3, D_MODEL)
    lam = _lambdas(lam_q1, lam_k1, lam_q2, lam_k2)

    w_bf = w_in.astype(BF16)
    wn = jnp.concatenate([w_bf[:, :, _C_KA:_C_VA], w_bf[:, :, _C_GA:_C_QB], w_bf[:, :, _C_GB:D_IN]], axis=2)
    wt = jnp.concatenate([w_bf[:, :, _C_QA:_C_KA], w_bf[:, :, _C_VA:_C_GA], w_bf[:, :, _C_QB:_C_GB]], axis=2)
    wt = jnp.swapaxes(wt, 1, 2)
    w_out_bf = w_out.astype(BF16)

    args = (rel_table, norm_g, wn, wt, subln_g, q_norm_g, k_norm_g, w_out_bf, final_g)
    y_prompt = _trunk(x_prompt, mod_p, lam, *args)
    y_sample = _trunk(x_sample, mod_s, lam, *args)
    return (y_prompt, y_sample)
```

```python
import functools
import math

import jax
import jax.numpy as jnp
from jax import lax
from jax.experimental import pallas as pl
from jax.experimental.pallas import tpu as pltpu

F32 = jnp.float32
BF16 = jnp.bfloat16

D_MODEL = 1024
DEPTH = 4
GRID_W = 64
A_WIDTH = 512
A_V_DIM = 128
A_HEAD_DIM = 64
A_HEADS = 4
B_WIDTH = 512
B_HEAD_DIM = 64
B_HEADS = 8
B_KV_HEADS = 2
B_GROUP = 4
NUM_BUCKETS = 32
MAX_DISTANCE = 128
ROPE_THETA = 10000.0
EPS = 1e-6

LOG2E = 1.4426950408889634
Q_SCALE = (A_HEAD_DIM ** -0.5) * LOG2E
M_INIT = -1e30
ONES_ROWS = 16
VMEM_LIMIT_BYTES = 56 * 1024 * 1024
UNITS_PER_STEP = 8
MAX_STATIC_TRIPS = 4
LOOP_BODY_CHUNKS = 32

_C_QA, _C_KA, _C_VA, _C_GA, _C_QB, _C_KB, _C_VB, _C_GB = 0, 512, 1024, 1536, 2048, 2560, 2688, 2816
D_IN = 3328
N_NAT = 1536
N_TR = 1792


def _rms_rows(x, g):
    ms = jnp.mean(x * x, axis=-1, keepdims=True)
    return x * lax.rsqrt(ms + EPS) * g


def _mod_kernel(c_ref, w_ref, b_ref, o_ref):
    c = c_ref[...]
    c_act = (c * jax.nn.sigmoid(c)).astype(BF16)
    o_ref[0] = jnp.dot(c_act, w_ref[0], preferred_element_type=F32) + b_ref[0]


def _adaln_mod(c_all, w_ada_bf, b_ada):
    rows = c_all.shape[0]
    return pl.pallas_call(
        _mod_kernel,
        grid=(DEPTH, 3),
        in_specs=[
            pl.BlockSpec((rows, D_MODEL), lambda l, j: (0, 0)),
            pl.BlockSpec((1, D_MODEL, D_MODEL), lambda l, j: (l, 0, j)),
            pl.BlockSpec((1, 1, D_MODEL), lambda l, j: (l, 0, j)),
        ],
        out_specs=pl.BlockSpec((1, rows, D_MODEL), lambda l, j: (l, 0, j)),
        out_shape=jax.ShapeDtypeStruct((DEPTH, rows, 3 * D_MODEL), F32),
        name="adaln_mod",
    )(c_all, w_ada_bf, b_ada.reshape(DEPTH, 1, 3 * D_MODEL))


def _lam_kernel(q1_ref, k1_ref, q2_ref, k2_ref, init_ref, o_ref):
    s1 = jnp.sum(q1_ref[...] * k1_ref[...], axis=-1, keepdims=True)
    s2 = jnp.sum(q2_ref[...] * k2_ref[...], axis=-1, keepdims=True)
    lam = jnp.exp(s1) - jnp.exp(s2) + init_ref[...][:, 0:1]
    o_ref[...] = jnp.broadcast_to(lam, o_ref.shape)


def _lambdas(lam_q1, lam_k1, lam_q2, lam_k2):
    init = jnp.asarray([[0.8 - 0.6 * math.exp(-0.3 * l)] * 128 for l in range(DEPTH)], F32)
    out = pl.pallas_call(
        _lam_kernel,
        out_shape=jax.ShapeDtypeStruct((DEPTH, 128), F32),
        name="diff_lambda",
    )(lam_q1, lam_k1, lam_q2, lam_k2, init)
    return out[:, 0]


def _rope_t(x, cos, sin):
    rot = jnp.concatenate([-x[16:32], x[0:16], -x[48:64], x[32:48]], axis=0)
    return x * cos + rot * sin


def _norm_t(x, g):
    ms = jnp.mean(x * x, axis=0, keepdims=True)
    return x * lax.rsqrt(ms + EPS) * g


def _inproj_kernel(x_ref, mod_ref, ng_ref, wn_ref, wt_ref, cos_ref, sin_ref, qg_ref, kg_ref,
                   ka_ref, gate_ref, qat_ref, vat_ref, qbt_ref, kb_ref, vbt_ref):
    ts = qg_ref.shape[1]
    for sub in range(x_ref.shape[1] // ts):
        rows = slice(sub * ts, (sub + 1) * ts)
        _inproj_subtile(rows, x_ref[0, rows, :], mod_ref, ng_ref, wn_ref, wt_ref, cos_ref, sin_ref,
                        qg_ref, kg_ref, ka_ref, gate_ref, qat_ref, vat_ref, qbt_ref, kb_ref, vbt_ref)


def _mid_kernel(ma_ref, mb_ref, x_ref, modp_ref, wo_ref, mod_ref, ng_ref, wn_ref, wt_ref, cos_ref, sin_ref,
                qg_ref, kg_ref, xn_ref, ka_ref, gate_ref, qat_ref, vat_ref, qbt_ref, kb_ref, vbt_ref):
    ts = qg_ref.shape[1]
    for sub in range(x_ref.shape[1] // ts):
        rows = slice(sub * ts, (sub + 1) * ts)
        y = (jnp.dot(ma_ref[0, rows, :], wo_ref[0:A_WIDTH, :], preferred_element_type=F32)
             + jnp.dot(mb_ref[0, rows, :], wo_ref[A_WIDTH:D_MODEL, :], preferred_element_type=F32))
        xn = x_ref[0, rows, :] + modp_ref[0, 2:3, :] * y
        xn_ref[0, rows, :] = xn
        _inproj_subtile(rows, xn, mod_ref, ng_ref, wn_ref, wt_ref, cos_ref, sin_ref,
                        qg_ref, kg_ref, ka_ref, gate_ref, qat_ref, vat_ref, qbt_ref, kb_ref, vbt_ref)


def _inproj_subtile(rows, x, mod_ref, ng_ref, wn_ref, wt_ref, cos_ref, sin_ref, qg_ref, kg_ref,
                    ka_ref, gate_ref, qat_ref, vat_ref, qbt_ref, kb_ref, vbt_ref):
    ts = rows.stop - rows.start
    shift = mod_ref[0, 0:1, :]
    scale = mod_ref[0, 1:2, :]
    h = (_rms_rows(x, ng_ref[...]) * (1.0 + scale) + shift).astype(BF16)

    pn = jnp.dot(h, wn_ref[...], preferred_element_type=F32)
    ka_ref[0, rows, :] = pn[:, 0:512].astype(BF16)
    g = pn[:, 512:N_NAT]
    gate_ref[0, rows, :] = g * jax.nn.sigmoid(g)

    pt = lax.dot_general(wt_ref[...], h, (((1,), (1,)), ((), ())), preferred_element_type=F32)
    qat_ref[0, :, rows] = (pt[0:512] * Q_SCALE).astype(BF16)
    ones = jnp.ones((ONES_ROWS, ts), BF16)
    for hh in range(A_HEADS):
        vat_ref[0, hh, 0:A_V_DIM, rows] = pt[512 + hh * A_V_DIM:512 + (hh + 1) * A_V_DIM].astype(BF16)
        vat_ref[0, hh, A_V_DIM:A_V_DIM + ONES_ROWS, rows] = ones

    cos = cos_ref[:, rows]
    sin = sin_ref[:, rows]
    qg = qg_ref[...]
    for hh in range(B_HEADS):
        xh = pt[1024 + hh * 64:1024 + (hh + 1) * 64]
        qbt_ref[0, hh * 64:(hh + 1) * 64, rows] = (_rope_t(_norm_t(xh, qg), cos, sin) * Q_SCALE).astype(BF16)
    kg = kg_ref[...]
    kparts = []
    for n in range(B_KV_HEADS):
        xh = pt[1536 + n * 64:1536 + (n + 1) * 64]
        kparts.append(_rope_t(_norm_t(xh, kg), cos, sin))
    kb_ref[0, rows, :] = jnp.concatenate(kparts, axis=0).T.astype(BF16)
    for n in range(B_KV_HEADS):
        vbt_ref[0, n, 0:64, rows] = pt[1664 + n * 64:1664 + (n + 1) * 64].astype(BF16)
        vbt_ref[0, n, 64:64 + ONES_ROWS, rows] = ones


def _inproj(x, mod, ng, wn, wt, cos_t, sin_t, qg_b, kg_b, tm, prev=None):
    b, s, _ = x.shape
    grid = (b, s // tm)
    ts = qg_b.shape[1]
    const2 = lambda i, j: (0, 0)
    tok = lambda i, j: (i, j, 0)
    per_batch = lambda i, j: (i, 0, 0)
    fused_in, fused_args, fused_out_specs, fused_out_shape = [], [], [], []
    if prev is not None:
        ma, mb, mod_prev, w_prev = prev
        fused_in = [pl.BlockSpec((1, tm, A_WIDTH), tok), pl.BlockSpec((1, tm, B_WIDTH), tok)]
        fused_args = [ma, mb]
        fused_out_specs = [pl.BlockSpec((1, tm, D_MODEL), tok)]
        fused_out_shape = [jax.ShapeDtypeStruct((b, s, D_MODEL), F32)]
    return pl.pallas_call(
        _inproj_kernel if prev is None else _mid_kernel,
        grid=grid,
        in_specs=fused_in + [
            pl.BlockSpec((1, tm, D_MODEL), tok),
        ] + ([] if prev is None else [
            pl.BlockSpec((1, 3, D_MODEL), per_batch),
            pl.BlockSpec((D_MODEL, D_MODEL), const2, pipeline_mode=pl.Buffered(1)),
        ]) + [
            pl.BlockSpec((1, 3, D_MODEL), per_batch),
            pl.BlockSpec((1, D_MODEL), const2),
            pl.BlockSpec((D_MODEL, N_NAT), const2, pipeline_mode=pl.Buffered(1)),
            pl.BlockSpec((N_TR, D_MODEL), const2, pipeline_mode=pl.Buffered(1)),
            pl.BlockSpec((B_HEAD_DIM, tm), lambda i, j: (0, j)),
            pl.BlockSpec((B_HEAD_DIM, tm), lambda i, j: (0, j)),
            pl.BlockSpec((B_HEAD_DIM, ts), const2),
            pl.BlockSpec((B_HEAD_DIM, ts), const2),
        ],
        out_specs=fused_out_specs + [
            pl.BlockSpec((1, tm, A_WIDTH), lambda i, j: (i, j, 0)),
            pl.BlockSpec((1, tm, D_MODEL), lambda i, j: (i, j, 0)),
            pl.BlockSpec((1, A_WIDTH, tm), lambda i, j: (i, 0, j)),
            pl.BlockSpec((1, A_HEADS, A_V_DIM + ONES_ROWS, tm), lambda i, j: (i, 0, 0, j)),
            pl.BlockSpec((1, B_WIDTH, tm), lambda i, j: (i, 0, j)),
            pl.BlockSpec((1, tm, B_KV_HEADS * B_HEAD_DIM), lambda i, j: (i, j, 0)),
            pl.BlockSpec((1, B_KV_HEADS, B_HEAD_DIM + ONES_ROWS, tm), lambda i, j: (i, 0, 0, j)),
        ],
        out_shape=fused_out_shape + [
            jax.ShapeDtypeStruct((b, s, A_WIDTH), BF16),
            jax.ShapeDtypeStruct((b, s, D_MODEL), F32),
            jax.ShapeDtypeStruct((b, A_WIDTH, s), BF16),
            jax.ShapeDtypeStruct((b, A_HEADS, A_V_DIM + ONES_ROWS, s), BF16),
            jax.ShapeDtypeStruct((b, B_WIDTH, s), BF16),
            jax.ShapeDtypeStruct((b, s, B_KV_HEADS * B_HEAD_DIM), BF16),
            jax.ShapeDtypeStruct((b, B_KV_HEADS, B_HEAD_DIM + ONES_ROWS, s), BF16),
        ],
        compiler_params=pltpu.CompilerParams(
            dimension_semantics=("arbitrary", "arbitrary"), vmem_limit_bytes=VMEM_LIMIT_BYTES),
        name="inproj" if prev is None else "outproj_inproj",
    )(*fused_args, x, *(() if prev is None else (mod_prev, w_prev)), mod, ng, wn, wt, cos_t, sin_t, qg_b, kg_b)


def _loop_shape(n_chunks, n_tiles, loop_units):
    if n_chunks <= LOOP_BODY_CHUNKS // 2:
        unroll, units = n_chunks, UNITS_PER_STEP
    else:
        unroll, units = LOOP_BODY_CHUNKS // loop_units, loop_units
    assert n_chunks % unroll == 0 and unroll % 2 == 0
    while n_tiles % units:
        units //= 2
    return unroll, units


def _run_tiles(units, unroll, n_chunks, setup, sub_step, finalize):
    if n_chunks == unroll:
        carries = [setup(unit, first=(unit == 0)) for unit in range(units)]
        for unit in range(units):
            carry = carries[unit]
            for u in range(unroll):
                if u + 1 < unroll:
                    carry = sub_step(unit, u, 0, carry)
                else:
                    nxt = unit + 1 if unit + 1 < units else -1
                    carry, mc_first = sub_step(unit, u, 0, carry, nxt)
                    if nxt >= 0:
                        carries[nxt] = (carries[nxt][0], mc_first)
            finalize(unit)
        return
    carries = tuple(setup(unit) for unit in range(units))

    def body(c0, carries):
        carries = list(carries)
        for u in range(unroll):
            for unit in range(units):
                carries[unit] = sub_step(unit, u, c0, carries[unit])
        return tuple(carries)

    n_trips = n_chunks // unroll
    if n_trips <= MAX_STATIC_TRIPS:
        one = jnp.minimum(pl.program_id(0), 0) + 1
        for trip in range(n_trips):
            carries = lax.fori_loop(0, one, lambda _, c, trip=trip: body(unroll * trip, c), carries)
    else:
        lax.fori_loop(0, n_trips, lambda i, c: body(unroll * i, c), carries)
    for unit in range(units):
        finalize(unit)


def _attn_a_kernel(scal_ref, qt_ref, k_ref, vt_ref, bias_ref, sg_ref, gate_ref, o_ref, *scratch,
                   tq, tk, n_chunks, ratio, unroll, units, out_scale):
    lam = scal_ref[0]
    log_ratio = ratio.bit_length() - 1
    wq_refs = [scratch[4 * unit] for unit in range(units)]
    acc_refs = [scratch[4 * unit + 1] for unit in range(units)]
    s_bufs = [scratch[4 * unit + 2:4 * unit + 4] for unit in range(units)]
    tiles = [pl.program_id(2) * units + unit for unit in range(units)]

    def qk(unit, c, s_ref):
        start = pl.multiple_of(c * tk, tk)
        d = c - lax.shift_right_logical(tiles[unit], log_ratio)
        r = lax.bitwise_and(tiles[unit], ratio - 1)
        tile = jnp.where(d < -1, 3 * ratio, jnp.where(d > 1, 3 * ratio + 1, (d + 1) * ratio + r))
        t = bias_ref[0, tile]
        s = jnp.dot(k_ref[0, pl.ds(start, tk), :], wq_refs[unit][...], preferred_element_type=F32)
        s = s + jnp.concatenate([t, t], axis=1)
        s_ref[...] = s
        return jnp.max(s, axis=0, keepdims=True)

    def softmax_pv(unit, s_ref, mc, m, c):
        mn = jnp.maximum(m, mc)
        alpha = jnp.exp2(m - mn)
        p = jnp.exp2(s_ref[...] - mn).astype(BF16)
        start = pl.multiple_of(c * tk, tk)
        pv = jnp.dot(vt_ref[0, 0, :, pl.ds(start, tk)], p, preferred_element_type=F32)
        acc_refs[unit][...] = acc_refs[unit][...] * alpha + pv
        return mn

    def setup(unit, c_first=0, first=True):
        q = qt_ref[0, :, unit * tq:(unit + 1) * tq]
        zero = jnp.zeros((A_HEAD_DIM, tq), BF16)
        wq_refs[unit][0:64, 0:tq] = q[0:64]
        wq_refs[unit][0:64, tq:2 * tq] = zero
        wq_refs[unit][64:128, 0:tq] = zero
        wq_refs[unit][64:128, tq:2 * tq] = q[64:128]
        acc_refs[unit][...] = jnp.zeros(acc_refs[unit].shape, F32)
        return jnp.full((1, 2 * tq), M_INIT, F32), qk(unit, c_first, s_bufs[unit][0]) if first else None

    def sub_step(unit, u, c0, carry, nxt=None):
        m, mc = carry
        if nxt is None:
            mc_next = qk(unit, jnp.minimum(c0 + u + 1, n_chunks - 1), s_bufs[unit][(u + 1) % 2])
            return softmax_pv(unit, s_bufs[unit][u % 2], mc, m, c0 + u), mc_next
        mc_first = qk(nxt, 0, s_bufs[nxt][0]) if nxt >= 0 else None
        return (softmax_pv(unit, s_bufs[unit][u % 2], mc, m, c0 + u), None), mc_first

    def finalize(unit):
        acc = acc_refs[unit][...]
        o = acc[0:A_V_DIM, :] * (1.0 / acc[A_V_DIM:A_V_DIM + 1, :])
        w = o[:, 0:tq] - lam * o[:, tq:2 * tq]
        y = _norm_t(w, sg_ref[...]) * out_scale
        rows = slice(unit * tq, (unit + 1) * tq)
        o_ref[0, rows, :] = (y.T * gate_ref[0, rows, :]).astype(BF16)

    if n_chunks != 2 * unroll:
        _run_tiles(units, unroll, n_chunks, setup, sub_step, finalize)
        return

    h = pl.program_id(1)
    c_neg = scal_ref[1 + h]
    c_pos = scal_ref[1 + A_HEADS + h]
    window = unroll
    w0 = jnp.clip(lax.shift_right_logical(tiles[0], log_ratio) - window // 2, 0, n_chunks - window)

    def far_chunk(f):
        return jnp.where(f < w0, f, f + window)

    def far_shift(f):
        return jnp.where(f < w0, c_neg, c_pos)

    def qk_far(unit, c, s_ref, shift):
        start = pl.multiple_of(c * tk, tk)
        s = jnp.dot(k_ref[0, pl.ds(start, tk), :], wq_refs[unit][...], preferred_element_type=F32)
        s_ref[...] = s
        return jnp.max(s, axis=0, keepdims=True) + shift

    def softmax_pv_far(unit, s_ref, mc, m, c, shift):
        mn = jnp.maximum(m, mc)
        alpha = jnp.exp2(m - mn)
        p = jnp.exp2(s_ref[...] - (mn - shift)).astype(BF16)
        start = pl.multiple_of(c * tk, tk)
        pv = jnp.dot(vt_ref[0, 0, :, pl.ds(start, tk)], p, preferred_element_type=F32)
        acc_refs[unit][...] = acc_refs[unit][...] * alpha + pv
        return mn

    def near_block(_, carries):
        carries = list(carries)
        for u in range(window):
            for unit in range(units):
                m, mc = carries[unit]
                nxt = s_bufs[unit][(u + 1) % 2]
                if u + 1 < window:
                    mc_next = qk(unit, w0 + u + 1, nxt)
                else:
                    mc_next = qk_far(unit, far_chunk(0), nxt, far_shift(0))
                carries[unit] = (softmax_pv(unit, s_bufs[unit][u % 2], mc, m, w0 + u), mc_next)
        return tuple(carries)

    def far_block(_, carries):
        carries = list(carries)
        for f in range(window):
            for unit in range(units):
                m, mc = carries[unit]
                f_next = min(f + 1, window - 1)
                mc_next = qk_far(unit, far_chunk(f_next), s_bufs[unit][(f + 1) % 2], far_shift(f_next))
                carries[unit] = (softmax_pv_far(unit, s_bufs[unit][f % 2], mc, m, far_chunk(f), far_shift(f)),
                                 mc_next)
        return tuple(carries)

    one = jnp.minimum(pl.program_id(0), 0) + 1
    carries = tuple(setup(unit, w0) for unit in range(units))
    carries = lax.fori_loop(0, one, near_block, carries)
    lax.fori_loop(0, one, far_block, carries)
    for unit in range(units):
        finalize(unit)


def _attn_a(scal, qat, ka, vat, bias_tiles, sg_b, gate, tq, tk, out_scale):
    b, s, _ = ka.shape
    n_chunks = s // tk
    ratio = tk // tq
    unroll, units = _loop_shape(n_chunks, s // tq, loop_units=2)
    assert n_chunks % unroll == 0 and ratio * tq == tk and ratio & (ratio - 1) == 0
    kern = functools.partial(_attn_a_kernel, tq=tq, tk=tk, n_chunks=n_chunks, ratio=ratio, unroll=unroll,
                             units=units, out_scale=out_scale)
    return pl.pallas_call(
        kern,
        grid=(b, A_HEADS, s // (tq * units)),
        in_specs=[
            pl.BlockSpec(memory_space=pltpu.SMEM),
            pl.BlockSpec((1, 2 * A_HEAD_DIM, tq * units), lambda i, hh, j: (i, hh, j)),
            pl.BlockSpec((1, s, 2 * A_HEAD_DIM), lambda i, hh, j: (i, 0, hh)),
            pl.BlockSpec((1, 1, A_V_DIM + ONES_ROWS, s), lambda i, hh, j: (i, hh, 0, 0)),
            pl.BlockSpec((1, 3 * ratio + 2, tk, tq), lambda i, hh, j: (hh, 0, 0, 0)),
            pl.BlockSpec((A_V_DIM, tq), lambda i, hh, j: (0, 0)),
            pl.BlockSpec((1, tq * units, A_V_DIM), lambda i, hh, j: (i, j, hh)),
        ],
        out_specs=pl.BlockSpec((1, tq * units, A_V_DIM), lambda i, hh, j: (i, j, hh)),
        out_shape=jax.ShapeDtypeStruct((b, s, A_WIDTH), BF16),
        scratch_shapes=[
            pltpu.VMEM((2 * A_HEAD_DIM, 2 * tq), BF16),
            pltpu.VMEM((A_V_DIM + ONES_ROWS, 2 * tq), F32),
            pltpu.VMEM((tk, 2 * tq), F32),
            pltpu.VMEM((tk, 2 * tq), F32),
        ] * units,
        compiler_params=pltpu.CompilerParams(
            dimension_semantics=("arbitrary", "arbitrary", "arbitrary"), vmem_limit_bytes=VMEM_LIMIT_BYTES),
        name="attn_a",
    )(scal, qat, ka, vat, bias_tiles, sg_b, gate)


def _attn_b_kernel(qt_ref, k_ref, vt_ref, gate_ref, o_ref, *scratch, tq, tk, n_chunks, unroll, units):
    half = B_GROUP * tq
    lanes = B_HEADS * tq
    wq_refs = [scratch[5 * unit] for unit in range(units)]
    acc_refs = [scratch[5 * unit + 1:5 * unit + 3] for unit in range(units)]
    s_bufs = [scratch[5 * unit + 3:5 * unit + 5] for unit in range(units)]

    def qk(unit, c, s_ref):
        start = pl.multiple_of(c * tk, tk)
        s = jnp.dot(k_ref[0, pl.ds(start, tk), :], wq_refs[unit][...], preferred_element_type=F32)
        s_ref[...] = s
        return jnp.max(s, axis=0, keepdims=True)

    def softmax_pv(unit, s_ref, mc, m, c):
        acc0_ref, acc1_ref = acc_refs[unit]
        mn = jnp.maximum(m, mc)
        alpha = jnp.exp2(m - mn)
        p = jnp.exp2(s_ref[...] - mn).astype(BF16)
        start = pl.multiple_of(c * tk, tk)
        v0 = vt_ref[0, 0, :, pl.ds(start, tk)]
        v1 = vt_ref[0, 1, :, pl.ds(start, tk)]
        acc0_ref[...] = acc0_ref[...] * alpha[:, 0:half] + jnp.dot(v0, p[:, 0:half], preferred_element_type=F32)
        acc1_ref[...] = acc1_ref[...] * alpha[:, half:] + jnp.dot(v1, p[:, half:], preferred_element_type=F32)
        return mn

    def setup(unit, first=True):
        wq_refs[unit][...] = jnp.zeros(wq_refs[unit].shape, BF16)
        for hh in range(B_HEADS):
            n = hh // B_GROUP
            wq_refs[unit][n * 64:(n + 1) * 64, hh * tq:(hh + 1) * tq] = qt_ref[0, hh * 64:(hh + 1) * 64,
                                                                               unit * tq:(unit + 1) * tq]
        for acc_ref in acc_refs[unit]:
            acc_ref[...] = jnp.zeros(acc_ref.shape, F32)
        return jnp.full((1, lanes), M_INIT, F32), qk(unit, 0, s_bufs[unit][0]) if first else None

    def sub_step(unit, u, c0, carry, nxt=None):
        m, mc = carry
        if nxt is None:
            mc_next = qk(unit, jnp.minimum(c0 + u + 1, n_chunks - 1), s_bufs[unit][(u + 1) % 2])
            return softmax_pv(unit, s_bufs[unit][u % 2], mc, m, c0 + u), mc_next
        mc_first = qk(nxt, 0, s_bufs[nxt][0]) if nxt >= 0 else None
        return (softmax_pv(unit, s_bufs[unit][u % 2], mc, m, c0 + u), None), mc_first

    def finalize(unit):
        parts = []
        for acc_ref in acc_refs[unit]:
            acc = acc_ref[...]
            o = acc[0:64, :] * (1.0 / acc[64:65, :])
            for g in range(B_GROUP):
                parts.append(o[:, g * tq:(g + 1) * tq])
        rows = slice(unit * tq, (unit + 1) * tq)
        out = jnp.concatenate(parts, axis=0).T
        o_ref[0, rows, :] = (out * gate_ref[0, rows, :]).astype(BF16)

    _run_tiles(units, unroll, n_chunks, setup, sub_step, finalize)


def _attn_b(qbt, kb, vbt, gate, tq, tk):
    b, s, _ = kb.shape
    n_chunks = s // tk
    unroll, units = _loop_shape(n_chunks, s // tq, loop_units=1)
    kern = functools.partial(_attn_b_kernel, tq=tq, tk=tk, n_chunks=n_chunks, unroll=unroll, units=units)
    return pl.pallas_call(
        kern,
        grid=(b, s // (tq * units)),
        in_specs=[
            pl.BlockSpec((1, B_WIDTH, tq * units), lambda i, j: (i, 0, j)),
            pl.BlockSpec((1, s, B_KV_HEADS * B_HEAD_DIM), lambda i, j: (i, 0, 0)),
            pl.BlockSpec((1, B_KV_HEADS, B_HEAD_DIM + ONES_ROWS, s), lambda i, j: (i, 0, 0, 0)),
            pl.BlockSpec((1, tq * units, B_WIDTH), lambda i, j: (i, j, 1)),
        ],
        out_specs=pl.BlockSpec((1, tq * units, B_WIDTH), lambda i, j: (i, j, 0)),
        out_shape=jax.ShapeDtypeStruct((b, s, B_WIDTH), BF16),
        scratch_shapes=[
            pltpu.VMEM((B_KV_HEADS * B_HEAD_DIM, B_HEADS * tq), BF16),
            pltpu.VMEM((B_HEAD_DIM + ONES_ROWS, B_GROUP * tq), F32),
            pltpu.VMEM((B_HEAD_DIM + ONES_ROWS, B_GROUP * tq), F32),
            pltpu.VMEM((tk, B_HEADS * tq), F32),
            pltpu.VMEM((tk, B_HEADS * tq), F32),
        ] * units,
        compiler_params=pltpu.CompilerParams(
            dimension_semantics=("arbitrary", "arbitrary"), vmem_limit_bytes=VMEM_LIMIT_BYTES),
        name="attn_b",
    )(qbt, kb, vbt, gate)


def _outproj_kernel(ma_ref, mb_ref, x_ref, mod_ref, w_ref, fg_ref, o_ref, *, final):
    y = (jnp.dot(ma_ref[0], w_ref[0:A_WIDTH, :], preferred_element_type=F32)
         + jnp.dot(mb_ref[0], w_ref[A_WIDTH:D_MODEL, :], preferred_element_type=F32))
    xn = x_ref[0] + mod_ref[0, 2:3, :] * y
    if final:
        xn = _rms_rows(xn, fg_ref[...])
    o_ref[0] = xn


def _outproj(ma, mb, x, mod, w, fg, tm, final):
    b, s, _ = x.shape
    tok = lambda i, j: (i, j, 0)
    return pl.pallas_call(
        functools.partial(_outproj_kernel, final=final),
        grid=(b, s // tm),
        in_specs=[
            pl.BlockSpec((1, tm, A_WIDTH), tok),
            pl.BlockSpec((1, tm, B_WIDTH), tok),
            pl.BlockSpec((1, tm, D_MODEL), tok),
            pl.BlockSpec((1, 3, D_MODEL), lambda i, j: (i, 0, 0)),
            pl.BlockSpec((D_MODEL, D_MODEL), lambda i, j: (0, 0)),
            pl.BlockSpec((1, D_MODEL), lambda i, j: (0, 0)),
        ],
        out_specs=pl.BlockSpec((1, tm, D_MODEL), tok),
        out_shape=jax.ShapeDtypeStruct((b, s, D_MODEL), F32),
        compiler_params=pltpu.CompilerParams(
            dimension_semantics=("arbitrary", "arbitrary"), vmem_limit_bytes=VMEM_LIMIT_BYTES),
        name="outproj",
    )(ma, mb, x, mod, w, fg)


def _t5_bucket(rel):
    half = NUM_BUCKETS // 2
    max_exact = half // 2
    ret = jnp.where(rel > 0, half, 0)
    n = jnp.abs(rel)
    nf = jnp.maximum(n, 1).astype(jnp.float32)
    large = max_exact + (jnp.log(nf / max_exact) / math.log(MAX_DISTANCE / max_exact)
                         * (half - max_exact)).astype(jnp.int32)
    large = jnp.minimum(large, half - 1)
    return ret + jnp.where(n < max_exact, n, large)


def _bias_tables(rel_table, s, tq, tk):
    ratio = tk // tq
    assert tk >= MAX_DISTANCE and s - 1 >= MAX_DISTANCE
    offsets = jnp.arange(-MAX_DISTANCE, MAX_DISTANCE + 1, dtype=jnp.int32)
    core = rel_table[_t5_bucket(offsets)].T.astype(F32) * LOG2E
    reach = 2 * tk + tq
    boff_p = jnp.pad(core, ((0, 0), (reach - MAX_DISTANCE, reach - MAX_DISTANCE)), mode="edge")
    span = tk + tq - 1
    tiles = []
    for dc in (-1, 0, 1):
        for r in range(ratio):
            first = dc * tk - r * tq - (tq - 1) + reach
            u = jnp.flip(boff_p[:, first:first + span], axis=1)
            hankel = jnp.tile(u, (1, tk + 1))[:, :tk * (span + 1)].reshape(A_HEADS, tk, span + 1)[:, :, :tq]
            tiles.append(jnp.flip(hankel, axis=1))
    for col in (0, 2 * reach):
        tiles.append(jnp.broadcast_to(boff_p[:, col][:, None, None], (A_HEADS, tk, tq)))
    return jnp.stack(tiles, axis=1), boff_p[:, 0], boff_p[:, 2 * reach]


def _rope_tables_t(s):
    rows = s // GRID_W
    row = jnp.repeat(jnp.arange(rows), GRID_W).astype(F32)
    col = jnp.tile(jnp.arange(GRID_W), rows).astype(F32)
    axis_dim = B_HEAD_DIM // 2
    inv_freq = ROPE_THETA ** (-jnp.arange(0, axis_dim, 2, dtype=F32) / axis_dim)
    ang_r = row[:, None] * inv_freq[None, :]
    ang_c = col[:, None] * inv_freq[None, :]
    ang = jnp.concatenate([ang_r, ang_r, ang_c, ang_c], axis=-1)
    return jnp.cos(ang).T, jnp.sin(ang).T


def _tile_sizes(s):
    del s
    return (512, 1024), 1024, (256, 512), (128, 256)


def _trunk(x, mod, lam, rel_table, norm_g, wn, wt, subln_g, q_norm_g, k_norm_g, w_out, final_g):
    b, s, _ = x.shape
    (ts_in, tm), tm_out, (tq_a, tk_a), (tq_b, tk_b) = _tile_sizes(s)
    cos_t, sin_t = _rope_tables_t(s)
    bias_tiles, c_neg, c_pos = _bias_tables(rel_table, s, tq_a, tk_a)
    fg = final_g.reshape(1, D_MODEL)
    prev = None
    for l in range(DEPTH):
        lam_init = 0.8 - 0.6 * math.exp(-0.3 * l)
        qg_b =jnp.broadcast_to(q_norm_g[l][:, None], (B_HEAD_DIM, ts_in))
        kg_b = jnp.broadcast_to(k_norm_g[l][:, None], (B_HEAD_DIM, ts_in))
        sg_b = jnp.broadcast_to(subln_g[l][:, None], (A_V_DIM, tq_a))
        outs = _inproj(x, mod[l], norm_g[l].reshape(1, D_MODEL), wn[l], wt[l], cos_t, sin_t, qg_b, kg_b, tm, prev=prev)
        if prev is not None:
            x, outs = outs[0], outs[1:]
        ka, gate, qat, vat, qbt, kb, vbt = outs
        scal = jnp.concatenate([lam[l:l + 1], c_neg, c_pos]).astype(F32)
        ma = _attn_a(scal, qat, ka, vat, bias_tiles, sg_b, gate, tq_a, tk_a, 1.0 - lam_init)
        mb = _attn_b(qbt, kb, vbt, gate, tq_b, tk_b)
        prev = (ma, mb, mod[l], w_out[l])
    return _outproj(ma, mb, x, mod[DEPTH - 1], w_out[DEPTH - 1], fg, tm_out, final=True)


def kernel(x_prompt, x_sample, c_prompt, c_sample, rel_table, norm_g, w_ada, b_ada, w_in, lam_q1, lam_k1, lam_q2,
           lam_k2, subln_g, q_norm_g, k_norm_g, w_out, final_g):
    bp = x_prompt.shape[0]
    bs = x_sample.shape[0]
    rows = -(-(bp + bs) // 8) * 8
    c_all = jnp.concatenate([c_prompt, c_sample, jnp.zeros((rows - bp - bs, D_MODEL), F32)], axis=0)
    mod_all = _adaln_mod(c_all, w_ada.astype(BF16), b_ada)
    mod_p = mod_all[:, :bp].reshape(DEPTH, bp, 3, D_MODEL)
    mod_s = mod_all[:, bp:bp + bs].reshape(DEPTH, bs, 3, D_MODEL)
    lam = _lambdas(lam_q1, lam_k1, lam_q2, lam_k2)

    w_bf = w_in.astype(BF16)
    wn = jnp.concatenate([w_bf[:, :, _C_KA:_C_VA], w_bf[:, :, _C_GA:_C_QB], w_bf[:, :, _C_GB:D_IN]], axis=2)
    wt = jnp.concatenate([w_bf[:, :, _C_QA:_C_KA], w_bf[:, :, _C_VA:_C_GA], w_bf[:, :, _C_QB:_C_GB]], axis=2)
    wt = jnp.swapaxes(wt, 1, 2)
    w_out_bf = w_out.astype(BF16)

    args = (rel_table, norm_g, wn, wt, subln_g, q_norm_g, k_norm_g, w_out_bf, final_g)
    y_prompt = _trunk(x_prompt, mod_p, lam, *args)
    y_sample = _trunk(x_sample, mod_s, lam, *args)
    return (y_prompt, y_sample)
```

```python
import functools
import math

import jax
import jax.numpy as jnp
from jax import lax
from jax.experimental import pallas as pl
from jax.experimental.pallas import tpu as pltpu

F32 = jnp.float32
BF16 = jnp.bfloat16

D_MODEL = 1024
DEPTH = 4
GRID_W = 64
A_WIDTH = 512
A_V_DIM = 128
A_HEAD_DIM = 64
A_HEADS = 4
B_WIDTH = 512
B_HEAD_DIM = 64
B_HEADS = 8
B_KV_HEADS = 2
B_GROUP = 4
NUM_BUCKETS = 32
MAX_DISTANCE = 128
ROPE_THETA = 10000.0
EPS = 1e-6

LOG2E = 1.4426950408889634
Q_SCALE = (A_HEAD_DIM ** -0.5) * LOG2E
M_INIT = -1e30
ONES_ROWS = 16
VMEM_LIMIT_BYTES = 56 * 1024 * 1024
UNITS_PER_STEP = 8
MAX_STATIC_TRIPS = 4
LOOP_BODY_CHUNKS = 32

_C_QA, _C_KA, _C_VA, _C_GA, _C_QB, _C_KB, _C_VB, _C_GB = 0, 512, 1024, 1536, 2048, 2560, 2688, 2816
D_IN = 3328
N_NAT = 1536
N_TR = 1792


def _rms_rows(x, g):
    ms = jnp.mean(x * x, axis=-1, keepdims=True)
    return x * lax.rsqrt(ms + EPS) * g


def _mod_kernel(c_ref, w_ref, b_ref, o_ref):
    c = c_ref[...]
    c_act = (c * jax.nn.sigmoid(c)).astype(BF16)
    o_ref[0] = jnp.dot(c_act, w_ref[0], preferred_element_type=F32) + b_ref[0]


def _adaln_mod(c_all, w_ada_bf, b_ada):
    rows = c_all.shape[0]
    return pl.pallas_call(
        _mod_kernel,
        grid=(DEPTH, 3),
        in_specs=[
            pl.BlockSpec((rows, D_MODEL), lambda l, j: (0, 0)),
            pl.BlockSpec((1, D_MODEL, D_MODEL), lambda l, j: (l, 0, j)),
            pl.BlockSpec((1, 1, D_MODEL), lambda l, j: (l, 0, j)),
        ],
        out_specs=pl.BlockSpec((1, rows, D_MODEL), lambda l, j: (l, 0, j)),
        out_shape=jax.ShapeDtypeStruct((DEPTH, rows, 3 * D_MODEL), F32),
        name="adaln_mod",
    )(c_all, w_ada_bf, b_ada.reshape(DEPTH, 1, 3 * D_MODEL))


def _lam_kernel(q1_ref, k1_ref, q2_ref, k2_ref, init_ref, o_ref):
    s1 = jnp.sum(q1_ref[...] * k1_ref[...], axis=-1, keepdims=True)
    s2 = jnp.sum(q2_ref[...] * k2_ref[...], axis=-1, keepdims=True)
    lam = jnp.exp(s1) - jnp.exp(s2) + init_ref[...][:, 0:1]
    o_ref[...] = jnp.broadcast_to(lam, o_ref.shape)


def _lambdas(lam_q1, lam_k1, lam_q2, lam_k2):
    init = jnp.asarray([[0.8 - 0.6 * math.exp(-0.3 * l)] * 128 for l in range(DEPTH)], F32)
    out = pl.pallas_call(
        _lam_kernel,
        out_shape=jax.ShapeDtypeStruct((DEPTH, 128), F32),
        name="diff_lambda",
    )(lam_q1, lam_k1, lam_q2, lam_k2, init)
    return out[:, 0]


def _rope_t(x, cos, sin):
    rot = jnp.concatenate([-x[16:32], x[0:16], -x[48:64], x[32:48]], axis=0)
    return x * cos + rot * sin


def _norm_t(x, g):
    ms = jnp.mean(x * x, axis=0, keepdims=True)
    return x * lax.rsqrt(ms + EPS) * g


def _inproj_kernel(x_ref, mod_ref, ng_ref, wn_ref, wt_ref, cos_ref, sin_ref, qg_ref, kg_ref,
                   ka_ref, gate_ref, qat_ref, vat_ref, qbt_ref, kb_ref, vbt_ref):
    ts = qg_ref.shape[1]
    for sub in range(x_ref.shape[1] // ts):
        rows = slice(sub * ts, (sub + 1) * ts)
        _inproj_subtile(rows, x_ref[0, rows, :], mod_ref, ng_ref, wn_ref, wt_ref, cos_ref, sin_ref,
                        qg_ref, kg_ref, ka_ref, gate_ref, qat_ref, vat_ref, qbt_ref, kb_ref, vbt_ref)


def _mid_kernel(ma_ref, mb_ref, x_ref, modp_ref, wo_ref, mod_ref, ng_ref, wn_ref, wt_ref, cos_ref, sin_ref,
                qg_ref, kg_ref, xn_ref, ka_ref, gate_ref, qat_ref, vat_ref, qbt_ref, kb_ref, vbt_ref):
    ts = qg_ref.shape[1]
    for sub in range(x_ref.shape[1] // ts):
        rows = slice(sub * ts, (sub + 1) * ts)
        y = (jnp.dot(ma_ref[0, rows, :], wo_ref[0:A_WIDTH, :], preferred_element_type=F32)
             + jnp.dot(mb_ref[0, rows, :], wo_ref[A_WIDTH:D_MODEL, :], preferred_element_type=F32))
        xn = x_ref[0, rows, :] + modp_ref[0, 2:3, :] * y
        xn_ref[0, rows, :] = xn
        _inproj_subtile(rows, xn, mod_ref, ng_ref, wn_ref, wt_ref, cos_ref, sin_ref,
                        qg_ref, kg_ref, ka_ref, gate_ref, qat_ref, vat_ref, qbt_ref, kb_ref, vbt_ref)


def _inproj_subtile(rows, x, mod_ref, ng_ref, wn_ref, wt_ref, cos_ref, sin_ref, qg_ref, kg_ref,
                    ka_ref, gate_ref, qat_ref, vat_ref, qbt_ref, kb_ref, vbt_ref):
    ts = rows.stop - rows.start
    shift = mod_ref[0, 0:1, :]
    scale = mod_ref[0, 1:2, :]
    h = (_rms_rows(x, ng_ref[...]) * (1.0 + scale) + shift).astype(BF16)

    pn = jnp.dot(h, wn_ref[...], preferred_element_type=F32)
    ka_ref[0, rows, :] = pn[:, 0:512].astype(BF16)
    g = pn[:, 512:N_NAT]
    gate_ref[0, rows, :] = g * jax.nn.sigmoid(g)

    pt = lax.dot_general(wt_ref[...], h, (((1,), (1,)), ((), ())), preferred_element_type=F32)
    qat_ref[0, :, rows] = (pt[0:512] * Q_SCALE).astype(BF16)
    ones = jnp.ones((ONES_ROWS, ts), BF16)
    for hh in range(A_HEADS):
        vat_ref[0, hh, 0:A_V_DIM, rows] = pt[512 + hh * A_V_DIM:512 + (hh + 1) * A_V_DIM].astype(BF16)
        vat_ref[0, hh, A_V_DIM:A_V_DIM + ONES_ROWS, rows] = ones

    cos = cos_ref[:, rows]
    sin = sin_ref[:, rows]
    qg = qg_ref[...]
    for hh in range(B_HEADS):
        xh = pt[1024 + hh * 64:1024 + (hh + 1) * 64]
        qbt_ref[0, hh * 64:(hh + 1) * 64, rows] = (_rope_t(_norm_t(xh, qg), cos, sin) * Q_SCALE).astype(BF16)
    kg = kg_ref[...]
    kparts = []
    for n in range(B_KV_HEADS):
        xh = pt[1536 + n * 64:1536 + (n + 1) * 64]
        kparts.append(_rope_t(_norm_t(xh, kg), cos, sin))
    kb_ref[0, rows, :] = jnp.concatenate(kparts, axis=0).T.astype(BF16)
    for n in range(B_KV_HEADS):
        vbt_ref[0, n, 0:64, rows] = pt[1664 + n * 64:1664 + (n + 1) * 64].astype(BF16)
        vbt_ref[0, n, 64:64 + ONES_ROWS, rows] = ones


def _inproj(x, mod, ng, wn, wt, cos_t, sin_t, qg_b, kg_b, tm, prev=None):
    b, s, _ = x.shape
    grid = (b, s // tm)
    ts = qg_b.shape[1]
    const2 = lambda i, j: (0, 0)
    tok = lambda i, j: (i, j, 0)
    per_batch = lambda i, j: (i, 0, 0)
    fused_in, fused_args, fused_out_specs, fused_out_shape = [], [], [], []
    if prev is not None:
        ma, mb, mod_prev, w_prev = prev
        fused_in = [pl.BlockSpec((1, tm, A_WIDTH), tok), pl.BlockSpec((1, tm, B_WIDTH), tok)]
        fused_args = [ma, mb]
        fused_out_specs = [pl.BlockSpec((1, tm, D_MODEL), tok)]
        fused_out_shape = [jax.ShapeDtypeStruct((b, s, D_MODEL), F32)]
    return pl.pallas_call(
        _inproj_kernel if prev is None else _mid_kernel,
        grid=grid,
        in_specs=fused_in + [
            pl.BlockSpec((1, tm, D_MODEL), tok),
        ] + ([] if prev is None else [
            pl.BlockSpec((1, 3, D_MODEL), per_batch),
            pl.BlockSpec((D_MODEL, D_MODEL), const2, pipeline_mode=pl.Buffered(1)),
        ]) + [
            pl.BlockSpec((1, 3, D_MODEL), per_batch),
            pl.BlockSpec((1, D_MODEL), const2),
            pl.BlockSpec((D_MODEL, N_NAT), const2, pipeline_mode=pl.Buffered(1)),
            pl.BlockSpec((N_TR, D_MODEL), const2, pipeline_mode=pl.Buffered(1)),
            pl.BlockSpec((B_HEAD_DIM, tm), lambda i, j: (0, j)),
            pl.BlockSpec((B_HEAD_DIM, tm), lambda i, j: (0, j)),
            pl.BlockSpec((B_HEAD_DIM, ts), const2),
            pl.BlockSpec((B_HEAD_DIM, ts), const2),
        ],
        out_specs=fused_out_specs + [
            pl.BlockSpec((1, tm, A_WIDTH), lambda i, j: (i, j, 0)),
            pl.BlockSpec((1, tm, D_MODEL), lambda i, j: (i, j, 0)),
            pl.BlockSpec((1, A_WIDTH, tm), lambda i, j: (i, 0, j)),
            pl.BlockSpec((1, A_HEADS, A_V_DIM + ONES_ROWS, tm), lambda i, j: (i, 0, 0, j)),
            pl.BlockSpec((1, B_WIDTH, tm), lambda i, j: (i, 0, j)),
            pl.BlockSpec((1, tm, B_KV_HEADS * B_HEAD_DIM), lambda i, j: (i, j, 0)),
            pl.BlockSpec((1, B_KV_HEADS, B_HEAD_DIM + ONES_ROWS, tm), lambda i, j: (i, 0, 0, j)),
        ],
        out_shape=fused_out_shape + [
            jax.ShapeDtypeStruct((b, s, A_WIDTH), BF16),
            jax.ShapeDtypeStruct((b, s, D_MODEL), F32),
            jax.ShapeDtypeStruct((b, A_WIDTH, s), BF16),
            jax.ShapeDtypeStruct((b, A_HEADS, A_V_DIM + ONES_ROWS, s), BF16),
            jax.ShapeDtypeStruct((b, B_WIDTH, s), BF16),
            jax.ShapeDtypeStruct((b, s, B_KV_HEADS * B_HEAD_DIM), BF16),
            jax.ShapeDtypeStruct((b, B_KV_HEADS, B_HEAD_DIM + ONES_ROWS, s), BF16),
        ],
        compiler_params=pltpu.CompilerParams(
            dimension_semantics=("arbitrary", "arbitrary"), vmem_limit_bytes=VMEM_LIMIT_BYTES),
        name="inproj" if prev is None else "outproj_inproj",
    )(*fused_args, x, *(() if prev is None else (mod_prev, w_prev)), mod, ng, wn, wt, cos_t, sin_t, qg_b, kg_b)


def _loop_shape(n_chunks, n_tiles, loop_units):
    if n_chunks <= LOOP_BODY_CHUNKS // 2:
        unroll, units = n_chunks, UNITS_PER_STEP
    else:
        unroll, units = LOOP_BODY_CHUNKS // loop_units, loop_units
    assert n_chunks % unroll == 0 and unroll % 2 == 0
    while n_tiles % units:
        units //= 2
    return unroll, units


def _run_tiles(units, unroll, n_chunks, setup, sub_step, finalize):
    if n_chunks == unroll:
        carries = [setup(unit, first=(unit == 0)) for unit in range(units)]
        for unit in range(units):
            carry = carries[unit]
            for u in range(unroll):
                if u + 1 < unroll:
                    carry = sub_step(unit, u, 0, carry)
                else:
                    nxt = unit + 1 if unit + 1 < units else -1
                    carry, mc_first = sub_step(unit, u, 0, carry, nxt)
                    if nxt >= 0:
                        carries[nxt] = (carries[nxt][0], mc_first)
            finalize(unit)
        return
    carries = tuple(setup(unit) for unit in range(units))

    def body(c0, carries, last_trip=False):
        carries = list(carries)
        for u in range(unroll):
            for unit in range(units):
                if last_trip and u + 1 == unroll:
                    (m, _), _ = sub_step(unit, u, c0, carries[unit], -1)
                    carries[unit] = (m, carries[unit][1])
                else:
                    carries[unit] = sub_step(unit, u, c0, carries[unit])
        return tuple(carries)

    n_trips = n_chunks // unroll
    if n_trips <= MAX_STATIC_TRIPS:
        one = jnp.minimum(pl.program_id(0), 0) + 1
        for trip in range(n_trips):
            carries = lax.fori_loop(
                0, one, lambda _, c, trip=trip: body(unroll * trip, c, last_trip=(trip == n_trips - 1)), carries)
    else:
        lax.fori_loop(0, n_trips, lambda i, c: body(unroll * i, c), carries)
    for unit in range(units):
        finalize(unit)


def _attn_a_kernel(scal_ref, qt_ref, k_ref, vt_ref, bias_ref, sg_ref, gate_ref, o_ref, *scratch,
                   tq, tk, n_chunks, ratio, unroll, units, out_scale):
    lam = scal_ref[0]
    log_ratio = ratio.bit_length() - 1
    wq_refs = [scratch[4 * unit] for unit in range(units)]
    acc_refs = [scratch[4 * unit + 1] for unit in range(units)]
    s_bufs = [scratch[4 * unit + 2:4 * unit + 4] for unit in range(units)]
    tiles = [pl.program_id(2) * units + unit for unit in range(units)]

    def qk(unit, c, s_ref):
        start = pl.multiple_of(c * tk, tk)
        d = c - lax.shift_right_logical(tiles[unit], log_ratio)
        r = lax.bitwise_and(tiles[unit], ratio - 1)
        tile = jnp.where(d < -1, 3 * ratio, jnp.where(d > 1, 3 * ratio + 1, (d + 1) * ratio + r))
        t = bias_ref[0, tile]
        s = jnp.dot(k_ref[0, pl.ds(start, tk), :], wq_refs[unit][...], preferred_element_type=F32)
        s = s + jnp.concatenate([t, t], axis=1)
        s_ref[...] = s
        return jnp.max(s, axis=0, keepdims=True)

    def softmax_pv(unit, s_ref, mc, m, c):
        mn = jnp.maximum(m, mc)
        alpha = jnp.exp2(m - mn)
        p = jnp.exp2(s_ref[...] - mn).astype(BF16)
        start = pl.multiple_of(c * tk, tk)
        pv = jnp.dot(vt_ref[0, 0, :, pl.ds(start, tk)], p, preferred_element_type=F32)
        acc_refs[unit][...] = acc_refs[unit][...] * alpha + pv
        return mn

    def setup(unit, c_first=0, first=True):
        q = qt_ref[0, :, unit * tq:(unit + 1) * tq]
        zero = jnp.zeros((A_HEAD_DIM, tq), BF16)
        wq_refs[unit][0:64, 0:tq] = q[0:64]
        wq_refs[unit][0:64, tq:2 * tq] = zero
        wq_refs[unit][64:128, 0:tq] = zero
        wq_refs[unit][64:128, tq:2 * tq] = q[64:128]
        acc_refs[unit][...] = jnp.zeros(acc_refs[unit].shape, F32)
        return jnp.full((1, 2 * tq), M_INIT, F32), qk(unit, c_first, s_bufs[unit][0]) if first else None

    def sub_step(unit, u, c0, carry, nxt=None):
        m, mc = carry
        if nxt is None:
            mc_next = qk(unit, jnp.minimum(c0 + u + 1, n_chunks - 1), s_bufs[unit][(u + 1) % 2])
            return softmax_pv(unit, s_bufs[unit][u % 2], mc, m, c0 + u), mc_next
        mc_first = qk(nxt, 0, s_bufs[nxt][0]) if nxt >= 0 else None
        return (softmax_pv(unit, s_bufs[unit][u % 2], mc, m, c0 + u), None), mc_first

    def finalize(unit):
        acc = acc_refs[unit][...]
        o = acc[0:A_V_DIM, :] * (1.0 / acc[A_V_DIM:A_V_DIM + 1, :])
        w = o[:, 0:tq] - lam * o[:, tq:2 * tq]
        y = _norm_t(w, sg_ref[...]) * out_scale
        rows = slice(unit * tq, (unit + 1) * tq)
        o_ref[0, rows, :] = (y.T * gate_ref[0, rows, :]).astype(BF16)

    if n_chunks != 2 * unroll:
        _run_tiles(units, unroll, n_chunks, setup, sub_step, finalize)
        return

    h = pl.program_id(1)
    c_neg = scal_ref[1 + h]
    c_pos = scal_ref[1 + A_HEADS + h]
    window = unroll
    w0 = jnp.clip(lax.shift_right_logical(tiles[0], log_ratio) - window // 2, 0, n_chunks - window)

    def far_chunk(f):
        return jnp.where(f < w0, f, f + window)

    def far_shift(f):
        return jnp.where(f < w0, c_neg, c_pos)

    def qk_far(unit, c, s_ref, shift):
        start = pl.multiple_of(c * tk, tk)
        s = jnp.dot(k_ref[0, pl.ds(start, tk), :], wq_refs[unit][...], preferred_element_type=F32)
        s_ref[...] = s
        return jnp.max(s, axis=0, keepdims=True) + shift

    def softmax_pv_far(unit, s_ref, mc, m, c, shift):
        mn = jnp.maximum(m, mc)
        alpha = jnp.exp2(m - mn)
        p = jnp.exp2(s_ref[...] - (mn - shift)).astype(BF16)
        start = pl.multiple_of(c * tk, tk)
        pv = jnp.dot(vt_ref[0, 0, :, pl.ds(start, tk)], p, preferred_element_type=F32)
        acc_refs[unit][...] = acc_refs[unit][...] * alpha + pv
        return mn

    def near_block(_, carries):
        carries = list(carries)
        for u in range(window):
            for unit in range(units):
                m, mc = carries[unit]
                nxt = s_bufs[unit][(u + 1) % 2]
                if u + 1 < window:
                    mc_next = qk(unit, w0 + u + 1, nxt)
                else:
                    mc_next = qk_far(unit, far_chunk(0), nxt, far_shift(0))
                carries[unit] = (softmax_pv(unit, s_bufs[unit][u % 2], mc, m, w0 + u), mc_next)
        return tuple(carries)

    def far_block(_, carries):
        carries = list(carries)
        for f in range(window):
            for unit in range(units):
                m, mc = carries[unit]
                if f + 1 < window:
                    mc_next = qk_far(unit, far_chunk(f + 1), s_bufs[unit][(f + 1) % 2], far_shift(f + 1))
                else:
                    mc_next = mc
                carries[unit] = (softmax_pv_far(unit, s_bufs[unit][f % 2], mc, m, far_chunk(f), far_shift(f)),
                                 mc_next)
        return tuple(carries)

    one = jnp.minimum(pl.program_id(0), 0) + 1
    carries = tuple(setup(unit, w0) for unit in range(units))
    carries = lax.fori_loop(0, one, near_block, carries)
    lax.fori_loop(0, one, far_block, carries)
    for unit in range(units):
        finalize(unit)


def _attn_a(scal, qat, ka, vat, bias_tiles, sg_b, gate, tq, tk, out_scale):
    b, s, _ = ka.shape
    n_chunks = s // tk
    ratio = tk // tq
    unroll, units = _loop_shape(n_chunks, s // tq, loop_units=2)
    assert n_chunks % unroll == 0 and ratio * tq == tk and ratio & (ratio - 1) == 0
    kern = functools.partial(_attn_a_kernel, tq=tq, tk=tk, n_chunks=n_chunks, ratio=ratio, unroll=unroll,
                             units=units, out_scale=out_scale)
    return pl.pallas_call(
        kern,
        grid=(b, A_HEADS, s // (tq * units)),
        in_specs=[
            pl.BlockSpec(memory_space=pltpu.SMEM),
            pl.BlockSpec((1, 2 * A_HEAD_DIM, tq * units), lambda i, hh, j: (i, hh, j)),
            pl.BlockSpec((1, s, 2 * A_HEAD_DIM), lambda i, hh, j: (i, 0, hh)),
            pl.BlockSpec((1, 1, A_V_DIM + ONES_ROWS, s), lambda i, hh, j: (i, hh, 0, 0)),
            pl.BlockSpec((1, 3 * ratio + 2, tk, tq), lambda i, hh, j: (hh, 0, 0, 0)),
            pl.BlockSpec((A_V_DIM, tq), lambda i, hh, j: (0, 0)),
            pl.BlockSpec((1, tq * units, A_V_DIM), lambda i, hh, j: (i, j, hh)),
        ],
        out_specs=pl.BlockSpec((1, tq * units, A_V_DIM), lambda i, hh, j: (i, j, hh)),
        out_shape=jax.ShapeDtypeStruct((b, s, A_WIDTH), BF16),
        scratch_shapes=[
            pltpu.VMEM((2 * A_HEAD_DIM, 2 * tq), BF16),
            pltpu.VMEM((A_V_DIM + ONES_ROWS, 2 * tq), F32),
            pltpu.VMEM((tk, 2 * tq), F32),
            pltpu.VMEM((tk, 2 * tq), F32),
        ] * units,
        compiler_params=pltpu.CompilerParams(
            dimension_semantics=("arbitrary", "arbitrary", "arbitrary"), vmem_limit_bytes=VMEM_LIMIT_BYTES),
        name="attn_a",
    )(scal, qat, ka, vat, bias_tiles, sg_b, gate)


def _attn_b_kernel(qt_ref, k_ref, vt_ref, gate_ref, o_ref, *scratch, tq, tk, n_chunks, unroll, units):
    half = B_GROUP * tq
    lanes = B_HEADS * tq
    wq_refs = [scratch[5 * unit] for unit in range(units)]
    acc_refs = [scratch[5 * unit + 1:5 * unit + 3] for unit in range(units)]
    s_bufs = [scratch[5 * unit + 3:5 * unit + 5] for unit in range(units)]

    def qk(unit, c, s_ref):
        start = pl.multiple_of(c * tk, tk)
        s = jnp.dot(k_ref[0, pl.ds(start, tk), :], wq_refs[unit][...], preferred_element_type=F32)
        s_ref[...] = s
        return jnp.max(s, axis=0, keepdims=True)

    def softmax_pv(unit, s_ref, mc, m, c):
        acc0_ref, acc1_ref = acc_refs[unit]
        mn = jnp.maximum(m, mc)
        alpha = jnp.exp2(m - mn)
        p = jnp.exp2(s_ref[...] - mn).astype(BF16)
        start = pl.multiple_of(c * tk, tk)
        v0 = vt_ref[0, 0, :, pl.ds(start, tk)]
        v1 = vt_ref[0, 1, :, pl.ds(start, tk)]
        acc0_ref[...] = acc0_ref[...] * alpha[:, 0:half] + jnp.dot(v0, p[:, 0:half], preferred_element_type=F32)
        acc1_ref[...] = acc1_ref[...] * alpha[:, half:] + jnp.dot(v1, p[:, half:], preferred_element_type=F32)
        return mn

    def setup(unit, first=True):
        wq_refs[unit][...] = jnp.zeros(wq_refs[unit].shape, BF16)
        for hh in range(B_HEADS):
            n = hh // B_GROUP
            wq_refs[unit][n * 64:(n + 1) * 64, hh * tq:(hh + 1) * tq] = qt_ref[0, hh * 64:(hh + 1) * 64,
                                                                               unit * tq:(unit + 1) * tq]
        for acc_ref in acc_refs[unit]:
            acc_ref[...] = jnp.zeros(acc_ref.shape, F32)
        return jnp.full((1, lanes), M_INIT, F32), qk(unit, 0, s_bufs[unit][0]) if first else None

    def sub_step(unit, u, c0, carry, nxt=None):
        m, mc = carry
        if nxt is None:
            mc_next = qk(unit, jnp.minimum(c0 + u + 1, n_chunks - 1), s_bufs[unit][(u + 1) % 2])
            return softmax_pv(unit, s_bufs[unit][u % 2], mc, m, c0 + u), mc_next
        mc_first = qk(nxt, 0, s_bufs[nxt][0]) if nxt >= 0 else None
        return (softmax_pv(unit, s_bufs[unit][u % 2], mc, m, c0 + u), None), mc_first

    def finalize(unit):
        parts = []
        for acc_ref in acc_refs[unit]:
            acc = acc_ref[...]
            o = acc[0:64, :] * (1.0 / acc[64:65, :])
            for g in range(B_GROUP):
                parts.append(o[:, g * tq:(g + 1) * tq])
        rows = slice(unit * tq, (unit + 1) * tq)
        out = jnp.concatenate(parts, axis=0).T
        o_ref[0, rows, :] = (out * gate_ref[0, rows, :]).astype(BF16)

    _run_tiles(units, unroll, n_chunks, setup, sub_step, finalize)


def _attn_b(qbt, kb, vbt, gate, tq, tk):
    b, s, _ = kb.shape
    n_chunks = s // tk
    unroll, units = _loop_shape(n_chunks, s // tq, loop_units=1)
    kern = functools.partial(_attn_b_kernel, tq=tq, tk=tk, n_chunks=n_chunks, unroll=unroll, units=units)
    return pl.pallas_call(
        kern,
        grid=(b, s // (tq * units)),
        in_specs=[
            pl.BlockSpec((1, B_WIDTH, tq * units), lambda i, j: (i, 0, j)),
            pl.BlockSpec((1, s, B_KV_HEADS * B_HEAD_DIM), lambda i, j: (i, 0, 0)),
            pl.BlockSpec((1, B_KV_HEADS, B_HEAD_DIM + ONES_ROWS, s), lambda i, j: (i, 0, 0, 0)),
            pl.BlockSpec((1, tq * units, B_WIDTH), lambda i, j: (i, j, 1)),
        ],
        out_specs=pl.BlockSpec((1, tq * units, B_WIDTH), lambda i, j: (i, j, 0)),
        out_shape=jax.ShapeDtypeStruct((b, s, B_WIDTH), BF16),
        scratch_shapes=[
            pltpu.VMEM((B_KV_HEADS * B_HEAD_DIM, B_HEADS * tq), BF16),
            pltpu.VMEM((B_HEAD_DIM + ONES_ROWS, B_GROUP * tq), F32),
            pltpu.VMEM((B_HEAD_DIM + ONES_ROWS, B_GROUP * tq), F32),
            pltpu.VMEM((tk, B_HEADS * tq), F32),
            pltpu.VMEM((tk, B_HEADS * tq), F32),
        ] * units,
        compiler_params=pltpu.CompilerParams(
            dimension_semantics=("arbitrary", "arbitrary"), vmem_limit_bytes=VMEM_LIMIT_BYTES),
        name="attn_b",
    )(qbt, kb, vbt, gate)


def _outproj_kernel(ma_ref, mb_ref, x_ref, mod_ref, w_ref, fg_ref, o_ref, *, final):
    y = (jnp.dot(ma_ref[0], w_ref[0:A_WIDTH, :], preferred_element_type=F32)
         + jnp.dot(mb_ref[0], w_ref[A_WIDTH:D_MODEL, :], preferred_element_type=F32))
    xn = x_ref[0] + mod_ref[0, 2:3, :] * y
    if final:
        xn = _rms_rows(xn, fg_ref[...])
    o_ref[0] = xn


def _outproj(ma, mb, x, mod, w, fg, tm, final):
    b, s, _ = x.shape
    tok = lambda i, j: (i, j, 0)
    return pl.pallas_call(
        functools.partial(_outproj_kernel, final=final),
        grid=(b, s // tm),
        in_specs=[
            pl.BlockSpec((1, tm, A_WIDTH), tok),
            pl.BlockSpec((1, tm, B_WIDTH), tok),
            pl.BlockSpec((1, tm, D_MODEL), tok),
            pl.BlockSpec((1, 3, D_MODEL), lambda i, j: (i, 0, 0)),
            pl.BlockSpec((D_MODEL, D_MODEL), lambda i, j: (0, 0)),
            pl.BlockSpec((1, D_MODEL), lambda i, j: (0, 0)),
        ],
        out_specs=pl.BlockSpec((1, tm, D_MODEL), tok),
        out_shape=jax.ShapeDtypeStruct((b, s, D_MODEL), F32),
        compiler_params=pltpu.CompilerParams(
            dimension_semantics=("arbitrary", "arbitrary"), vmem_limit_bytes=VMEM_LIMIT_BYTES),
        name="outproj",
    )(ma, mb, x, mod, w, fg)


def _t5_bucket(rel):
    half = NUM_BUCKETS // 2
    max_exact = half // 2
    ret = jnp.where(rel > 0, half, 0)
    n = jnp.abs(rel)
    nf = jnp.maximum(n, 1).astype(jnp.float32)
    large = max_exact + (jnp.log(nf / max_exact) / math.log(MAX_DISTANCE / max_exact)
                         * (half - max_exact)).astype(jnp.int32)
    large = jnp.minimum(large, half - 1)
    return ret + jnp.where(n < max_exact, n, large)


def _bias_tables(rel_table, s, tq, tk):
    ratio = tk // tq
    assert tk >= MAX_DISTANCE and s - 1 >= MAX_DISTANCE
    offsets = jnp.arange(-MAX_DISTANCE, MAX_DISTANCE + 1, dtype=jnp.int32)
    core = rel_table[_t5_bucket(offsets)].T.astype(F32) * LOG2E
    reach = 2 * tk + tq
    boff_p = jnp.pad(core, ((0, 0), (reach - MAX_DISTANCE, reach - MAX_DISTANCE)), mode="edge")
    span = tk + tq - 1
    tiles = []
    for dc in (-1, 0, 1):
        for r in range(ratio):
            first = dc * tk - r * tq - (tq - 1) + reach
            u = jnp.flip(boff_p[:, first:first + span], axis=1)
            hankel = jnp.tile(u, (1, tk + 1))[:, :tk * (span + 1)].reshape(A_HEADS, tk, span + 1)[:, :, :tq]
            tiles.append(jnp.flip(hankel, axis=1))
    for col in (0, 2 * reach):
        tiles.append(jnp.broadcast_to(boff_p[:, col][:, None, None], (A_HEADS, tk, tq)))
    return jnp.stack(tiles, axis=1), boff_p[:, 0], boff_p[:, 2 * reach]


def _rope_tables_t(s):
    rows = s // GRID_W
    row = jnp.repeat(jnp.arange(rows), GRID_W).astype(F32)
    col = jnp.tile(jnp.arange(GRID_W), rows).astype(F32)
    axis_dim = B_HEAD_DIM // 2
    inv_freq = ROPE_THETA ** (-jnp.arange(0, axis_dim, 2, dtype=F32) / axis_dim)
    ang_r = row[:, None] * inv_freq[None, :]
    ang_c = col[:, None] * inv_freq[None, :]
    ang = jnp.concatenate([ang_r, ang_r, ang_c, ang_c], axis=-1)
    return jnp.cos(ang).T, jnp.sin(ang).T


def _tile_sizes(s):
    del s
    return (512, 1024), 1024, (256, 512), (128, 256)


def _trunk(x, mod, lam, rel_table, norm_g, wn, wt, subln_g, q_norm_g, k_norm_g, w_out, final_g):
    b, s, _ = x.shape
    (ts_in, tm), tm_out, (tq_a, tk_a), (tq_b, tk_b) = _tile_sizes(s)
    cos_t, sin_t = _rope_tables_t(s)
    bias_tiles, c_neg, c_pos = _bias_tables(rel_table, s, tq_a, tk_a)
    fg = final_g.reshape(1, D_MODEL)
    prev = None
    for l in range(DEPTH):
        lam_init = 0.8 - 0.6 * math.exp(-0.3 * l)
        qg_b =jnp.broadcast_to(q_norm_g[l][:, None], (B_HEAD_DIM, ts_in))
        kg_b = jnp.broadcast_to(k_norm_g[l][:, None], (B_HEAD_DIM, ts_in))
        sg_b = jnp.broadcast_to(subln_g[l][:, None], (A_V_DIM, tq_a))
        outs = _inproj(x, mod[l], norm_g[l].reshape(1, D_MODEL), wn[l], wt[l], cos_t, sin_t, qg_b, kg_b, tm, prev=prev)
        if prev is not None:
            x, outs = outs[0], outs[1:]
        ka, gate, qat, vat, qbt, kb, vbt = outs
        scal = jnp.concatenate([lam[l:l + 1], c_neg, c_pos]).astype(F32)
        ma = _attn_a(scal, qat, ka, vat, bias_tiles, sg_b, gate, tq_a, tk_a, 1.0 - lam_init)
        mb = _attn_b(qbt, kb, vbt, gate, tq_b, tk_b)
        prev = (ma, mb, mod[l], w_out[l])
    return _outproj(ma, mb, x, mod[DEPTH - 1], w_out[DEPTH - 1], fg, tm_out, final=True)


def kernel(x_prompt, x_sample, c_prompt, c_sample, rel_table, norm_g, w_ada, b_ada, w_in, lam_q1, lam_k1, lam_q2,
           lam_k2, subln_g, q_norm_g, k_norm_g, w_out, final_g):
    bp = x_prompt.shape[0]
    bs = x_sample.shape[0]
    rows = -(-(bp + bs) // 8) * 8
    c_all = jnp.concatenate([c_prompt, c_sample, jnp.zeros((rows - bp - bs, D_MODEL), F32)], axis=0)
    mod_all = _adaln_mod(c_all, w_ada.astype(BF16), b_ada)
    mod_p = mod_all[:, :bp].reshape(DEPTH, bp, 3, D_MODEL)
    mod_s = mod_all[:, bp:bp + bs].reshape(DEPTH, bs, 3, D_MODEL)
    lam = _lambdas(lam_q1, lam_k1, lam_q2, lam_k2)

    w_bf = w_in.astype(BF16)
    wn = jnp.concatenate([w_bf[:, :, _C_KA:_C_VA], w_bf[:, :, _C_GA:_C_QB], w_bf[:, :, _C_GB:D_IN]], axis=2)
    wt = jnp.concatenate([w_bf[:, :, _C_QA:_C_KA], w_bf[:, :, _C_VA:_C_GA], w_bf[:, :, _C_QB:_C_GB]], axis=2)
    wt = jnp.swapaxes(wt, 1, 2)
    w_out_bf = w_out.astype(BF16)

    args = (rel_table, norm_g, wn, wt, subln_g, q_norm_g, k_norm_g, w_out_bf, final_g)
    y_prompt = _trunk(x_prompt, mod_p, lam, *args)
    y_sample = _trunk(x_sample, mod_s, lam, *args)
    return (y_prompt, y_sample)
```
